```python
import jax, jax.numpy as jnp
from jax import lax
import numpy as np

D_MODEL = 1024
BATCH = 8
SEQ = 4096
DEPTH = 1

HEAD_DIM = 64
RWKV_HEADS = 8
FOX_HEADS = 8
RWKV_WIDTH = RWKV_HEADS * HEAD_DIM
FOX_WIDTH = FOX_HEADS * HEAD_DIM
DECAY_RANK = 64
ICLR_RANK = 64
GATE_RANK = 128
D_FF = 4 * D_MODEL
Q_BLOCK = 128
NORM_EPS = 1e-6
GN_EPS = 64e-5
N_MOD = 6

RWKV_COLS = (RWKV_WIDTH, DECAY_RANK, RWKV_WIDTH, RWKV_WIDTH, ICLR_RANK, GATE_RANK)
FOX_COLS = (FOX_WIDTH, FOX_WIDTH, FOX_WIDTH, FOX_HEADS)
GATE_COLS = (D_MODEL, D_MODEL)
N_RWKV = 3 * RWKV_WIDTH + DECAY_RANK + ICLR_RANK + GATE_RANK
N_FOX = 3 * FOX_WIDTH + FOX_HEADS
N_GATE = 2 * D_MODEL
N_IN = N_RWKV + N_FOX + N_GATE

kernel_name = "rwkv7_fox_gated_hybrid_block"


def _split(t, sizes):
    idx = np.cumsum(np.array(sizes))[:-1].tolist()
    return jnp.split(t, idx, axis=-1)


def _rmsnorm(x, g):
    xf = x.astype(jnp.float32)
    y = xf * lax.rsqrt(jnp.mean(xf * xf, axis=-1, keepdims=True) + NORM_EPS)
    return (y * g.astype(jnp.float32)).astype(x.dtype)


def _modulate(x, g, shift, scale):
    return _rmsnorm(x, g) * (1.0 + scale[:, None, :]) + shift[:, None, :]


def _rwkv7_mix(p, w_decay_up, decay_base, w_iclr_up, iclr_base, w_gate_up,
               kk_scale, k_iclr_mix, r_bonus, lnx_w, lnx_b):
    B, S, _ = p.shape
    H, N = RWKV_HEADS, HEAD_DIM
    f32 = jnp.float32
    r, wd, k, v, ad, gd = _split(p, RWKV_COLS)
    w = -jax.nn.softplus(-(decay_base + jnp.tanh(wd) @ w_decay_up)) - 0.5
    decay = jnp.exp(-jnp.exp(w.astype(f32)))
    a = jax.nn.sigmoid(iclr_base + ad @ w_iclr_up)
    g = jax.nn.sigmoid(gd) @ w_gate_up
    heads = lambda t: t.reshape(B, S, H, N).astype(f32)
    kk = heads(k * kk_scale)
    kk = kk / jnp.maximum(jnp.sqrt(jnp.sum(kk * kk, axis=-1, keepdims=True)), 1e-12)
    k = heads(k * (1.0 + (a - 1.0) * k_iclr_mix))
    r_h, v_h, a_h, w_h = heads(r), heads(v), heads(a), heads(decay)
    a_vec = -kk
    b_vec = kk * a_h
    tm = lambda t: jnp.transpose(t, (1, 0, 2, 3))

    def step(state, inp):
        r_t, w_t, k_t, v_t, a_t, b_t = inp
        sa = jnp.einsum('bhij,bhj->bhi', state, a_t)
        state = (state * w_t[:, :, None, :] + sa[..., None] * b_t[:, :, None, :]
                 + v_t[..., None] * k_t[:, :, None, :])
        y_t = jnp.einsum('bhij,bhj->bhi', state, r_t)
        return state, y_t

    state0 = jnp.zeros((B, H, N, N), f32)
    _, y = lax.scan(step, state0, (tm(r_h), tm(w_h), tm(k), tm(v_h), tm(a_vec), tm(b_vec)))
    y = jnp.transpose(y, (1, 0, 2, 3))
    mu = jnp.mean(y, axis=-1, keepdims=True)
    var = jnp.mean(jnp.square(y - mu), axis=-1, keepdims=True)
    y = ((y - mu) * lax.rsqrt(var + GN_EPS)).reshape(B, S, H * N)
    y = y * lnx_w.astype(f32) + lnx_b.astype(f32)
    bonus = jnp.sum(r_h * k * r_bonus.astype(f32), axis=-1, keepdims=True) * v_h
    y = y + bonus.reshape(B, S, H * N)
    return (y.astype(p.dtype) * g)


def _forgetting_attention(q, k, v, f_logit, f_bias):
    B, S, _ = q.shape
    H, Dh = FOX_HEADS, HEAD_DIM
    nb = S // Q_BLOCK
    scale = 1.0 / np.sqrt(Dh).astype(np.float32)
    to_h = lambda t: t.reshape(B, S, H, Dh).transpose(0, 2, 1, 3)
    q, k, v = to_h(q), to_h(k), to_h(v)
    logf = jax.nn.log_sigmoid((f_logit + f_bias).astype(jnp.float32))
    cum = jnp.cumsum(logf, axis=1).transpose(0, 2, 1)
    qb = q.reshape(B, H, nb, Q_BLOCK, Dh).transpose(2, 0, 1, 3, 4)
    cb = cum.reshape(B, H, nb, Q_BLOCK).transpose(2, 0, 1, 3)
    kpos = jnp.arange(S)

    def block(args):
        qi, ci, i = args
        qpos = i * Q_BLOCK + jnp.arange(Q_BLOCK)
        s = (jnp.einsum('bhqd,bhkd->bhqk', qi, k).astype(jnp.float32) * scale
             + ci[..., :, None] - cum[..., None, :])
        s = jnp.where(kpos[None, :] <= qpos[:, None], s, -jnp.inf)
        pr = jax.nn.softmax(s, axis=-1)
        return jnp.einsum('bhqk,bhkd->bhqd', pr.astype(v.dtype), v)

    o = lax.map(block, (qb, cb, jnp.arange(nb)))
    return o.transpose(1, 0, 3, 2, 4).reshape(B, S, H * Dh)


def setup_inputs(seed: int = 0) -> dict:
    key = jax.random.key(seed)
    ks = jax.random.split(key, 32)
    L, D = DEPTH, D_MODEL
    nrm = lambda k, shape, s: jax.random.normal(k, shape, jnp.float32) * s
    uni = lambda k, shape, lo, hi: jax.random.uniform(k, shape, jnp.float32, lo, hi)
    return {
        "x": nrm(ks[0], (BATCH, SEQ, D), 1.0),
        "c": nrm(ks[1], (BATCH, D), 1.0),
        "w_ada": nrm(ks[2], (L, D, N_MOD * D), 0.5 * D ** -0.5),
        "b_ada": nrm(ks[3], (L, N_MOD * D), 0.02),
        "norm1_g": 1.0 + nrm(ks[4], (L, D), 0.02),
        "w_in": nrm(ks[5], (L, D, N_IN), D ** -0.5),
        "mu_shift": uni(ks[6], (L, N_RWKV), 0.0, 1.0),
        "w_decay_up": nrm(ks[7], (L, DECAY_RANK, RWKV_WIDTH), DECAY_RANK ** -0.5),
        "decay_base": uni(ks[8], (L, RWKV_WIDTH), -5.0, 1.0),
        "w_iclr_up": nrm(ks[9], (L, ICLR_RANK, RWKV_WIDTH), ICLR_RANK ** -0.5),
        "iclr_base": nrm(ks[10], (L, RWKV_WIDTH), 0.1),
        "w_gate_up": nrm(ks[11], (L, GATE_RANK, RWKV_WIDTH), GATE_RANK ** -0.5),
        "kk_scale": 0.85 + nrm(ks[12], (L, RWKV_WIDTH), 0.05),
        "k_iclr_mix": 1.0 + nrm(ks[13], (L, RWKV_WIDTH), 0.05),
        "r_bonus": nrm(ks[14], (L, RWKV_HEADS, HEAD_DIM), 0.1),
        "lnx_w": 1.0 + nrm(ks[15], (L, RWKV_WIDTH), 0.02),
        "lnx_b": nrm(ks[16], (L, RWKV_WIDTH), 0.02),
        "fox_f_bias": uni(ks[17], (L, FOX_HEADS), 1.0, 4.0),
        "w_o_rwkv": nrm(ks[18], (L, RWKV_WIDTH, D), RWKV_WIDTH ** -0.5),
        "w_o_fox": nrm(ks[19], (L, FOX_WIDTH, D), FOX_WIDTH ** -0.5),
        "w_out": nrm(ks[20], (L, D, D), D ** -0.5),
        "norm2_g": 1.0 + nrm(ks[21], (L, D), 0.02),
        "w_ff1": nrm(ks[22], (L, D, D_FF), D ** -0.5),
        "w_ff2": nrm(ks[23], (L, D_FF, D), D_FF ** -0.5),
        "final_g": 1.0 + nrm(ks[24], (D,), 0.02),
    }


def reference(x, c, w_ada, b_ada, norm1_g, w_in, mu_shift, w_decay_up, decay_base, w_iclr_up,
              iclr_base, w_gate_up, kk_scale, k_iclr_mix, r_bonus, lnx_w, lnx_b, fox_f_bias,
              w_o_rwkv, w_o_fox, w_out, norm2_g, w_ff1, w_ff2, final_g):
    c_act = jax.nn.silu(c)
    for l in range(DEPTH):
        mod = c_act @ w_ada[l] + b_ada[l]
        sh1, sc1, gt1, sh2, sc2, gt2 = jnp.split(mod, N_MOD, axis=-1)

        h = _modulate(x, norm1_g[l], sh1, sc1)
        proj = h @ w_in[l]
        p_rwkv, p_fox, p_gate = _split(proj, (N_RWKV, N_FOX, N_GATE))
        prev = jnp.pad(p_rwkv[:, :-1], ((0, 0), (1, 0), (0, 0)))
        p_rwkv = p_rwkv + mu_shift[l] * (prev - p_rwkv)
        y_a = _rwkv7_mix(p_rwkv, w_decay_up[l], decay_base[l], w_iclr_up[l], iclr_base[l],
                         w_gate_up[l], kk_scale[l], k_iclr_mix[l], r_bonus[l], lnx_w[l], lnx_b[l])
        fq, fk, fv, ff = _split(p_fox, FOX_COLS)
        y_b = _forgetting_attention(fq, fk, fv, ff, fox_f_bias[l])
        g_a, g_b = _split(p_gate, GATE_COLS)
        merged = (jax.nn.sigmoid(g_a) * (y_a @ w_o_rwkv[l])
                  + jax.nn.sigmoid(g_b) * (y_b @ w_o_fox[l]))
        x = x + gt1[:, None, :] * (merged @ w_out[l])

        h2 = _modulate(x, norm2_g[l], sh2, sc2)
        ff_out = jnp.square(jax.nn.relu(h2 @ w_ff1[l])) @ w_ff2[l]
        x = x + gt2[:, None, :] * ff_out
    return _rmsnorm(x, final_g)
```

```python
import functools

import jax
import jax.numpy as jnp
from jax import lax
from jax.experimental import pallas as pl
from jax.experimental.pallas import tpu as pltpu

F32 = jnp.float32
BF16 = jnp.bfloat16

D_MODEL = 1024
HEAD_DIM = 64
HEADS = 8
WIDTH = HEADS * HEAD_DIM
DECAY_RANK = 64
ICLR_RANK = 64
GATE_RANK = 128
D_FF = 4 * D_MODEL
N_MOD = 6
NORM_EPS = 1e-6
GN_EPS = 64e-5

LANES = 128
PAIR = 2 * HEAD_DIM
N_PAIRS = HEADS // 2

N_SMALL = DECAY_RANK + ICLR_RANK + GATE_RANK
N_RKV = 3 * WIDTH
N_FOX = 3 * WIDTH
N_GATE = 2 * D_MODEL
OFF_RKV = 0
OFF_SMALL = OFF_RKV + N_RKV
OFF_FOX = OFF_SMALL + N_SMALL
OFF_FF = OFF_FOX + N_FOX
OFF_GATE = OFF_FF + LANES
N_PROJ = OFF_GATE + N_GATE

PROJ_ROWS = 256
CHUNK = 128
FOX_BLOCK = 512
TAIL_ROWS = 512
FF_CHUNK = 1024

VMEM_LIMIT_BYTES = 56 * 1024 * 1024


def _dot(a, b):
    return jnp.dot(a.astype(BF16), b.astype(BF16), preferred_element_type=F32)


def _dot_nt(a, b):
    return lax.dot_general(a.astype(BF16), b.astype(BF16), (((1,), (1,)), ((), ())),
                           preferred_element_type=F32)


def _softplus(z):
    return jnp.maximum(z, 0.0) + jnp.log(1.0 + jnp.exp(-jnp.abs(z)))


def _sigmoid(z):
    return 1.0 / (1.0 + jnp.exp(-z))


def _split3(x):
    hi = x.astype(BF16)
    r1 = x - hi.astype(F32)
    mid = r1.astype(BF16)
    lo = (r1 - mid.astype(F32)).astype(BF16)
    return hi, mid, lo


def _cumsum_rows(tri, x):
    hi, mid, lo = _split3(x)
    d = lambda p: jnp.dot(tri, p, preferred_element_type=F32)
    return d(hi) + d(mid) + d(lo)


def _lower_tri(n, dtype):
    row = lax.broadcasted_iota(jnp.int32, (n, n), 0)
    col = lax.broadcasted_iota(jnp.int32, (n, n), 1)
    return (col <= row).astype(dtype)


def _mod_kernel(c_ref, w_ref, b_ref, o_ref):
    c = c_ref[...]
    c_act = c * _sigmoid(c)
    o_ref[...] = jnp.dot(c_act, w_ref[...], preferred_element_type=F32,
                         precision=lax.Precision.HIGHEST) + b_ref[...]


def _modulation(c, w_ada, b_ada):
    batch = c.shape[0]
    n = w_ada.shape[1]
    tn = 1536
    return pl.pallas_call(
        _mod_kernel,
        grid=(n // tn,),
        in_specs=[pl.BlockSpec((batch, D_MODEL), lambda j: (0, 0)),
                  pl.BlockSpec((D_MODEL, tn), lambda j: (0, j)),
                  pl.BlockSpec((1, tn), lambda j: (0, j))],
        out_specs=pl.BlockSpec((batch, tn), lambda j: (0, j)),
        out_shape=jax.ShapeDtypeStruct((batch, n), F32),
        compiler_params=pltpu.CompilerParams(dimension_semantics=("arbitrary",),
                                             vmem_limit_bytes=VMEM_LIMIT_BYTES),
        name="mod",
    )(c, w_ada, b_ada.reshape(1, n))


def _proj_kernel(x_ref, mod_ref, g_ref, w_ref, mu_ref, fb_ref,
                 rkv_ref, small_ref, fox_ref, cum_ref, gate_ref,
                 shift_ref, carry_ref):
    s = pl.program_id(1)
    rows = x_ref.shape[1]

    @pl.when(s == 0)
    def _():
        shift_ref[0:8, :] = jnp.zeros((8, N_RKV + N_SMALL), F32)
        carry_ref[...] = jnp.zeros_like(carry_ref)

    x = x_ref[0]
    shift1 = mod_ref[0, 0:1, :]
    scale1 = mod_ref[0, 1:2, :]
    inv = lax.rsqrt(jnp.mean(x * x, axis=-1, keepdims=True) + NORM_EPS)
    h = ((x * inv) * g_ref[...] * (1.0 + scale1) + shift1).astype(BF16)

    n_tok = N_RKV + N_SMALL
    p = jnp.dot(h, w_ref[:, OFF_RKV:OFF_RKV + n_tok], preferred_element_type=F32)
    shift_ref[8:8 + rows, :] = p
    prev = shift_ref[7:7 + rows, :]
    shift_ref[7:8, :] = p[rows - 1:rows, :]
    mixed = p + mu_ref[...] * (prev - p)
    rkv_ref[0] = mixed[:, 0:N_RKV].astype(BF16)
    small_ref[0] = mixed[:, N_RKV:n_tok]

    pf = jnp.dot(h, w_ref[:, OFF_FOX:OFF_FOX + N_FOX], preferred_element_type=F32)
    fox_ref[0, :, 0:WIDTH] = (pf[:, 0:WIDTH] * (HEAD_DIM ** -0.5)).astype(BF16)
    fox_ref[0, :, WIDTH:N_FOX] = pf[:, WIDTH:N_FOX].astype(BF16)

    ff = jnp.dot(h, w_ref[:, OFF_FF:OFF_FF + LANES], preferred_element_type=F32)
    logf = -_softplus(-(ff + fb_ref[...]))
    cum = _cumsum_rows(_lower_tri(rows, BF16), logf) + carry_ref[...]
    cum_ref[0] = cum
    carry_ref[...] = cum[rows - 1:rows, :]

    pg = jnp.dot(h, w_ref[:, OFF_GATE:OFF_GATE + N_GATE], preferred_element_type=F32)
    gate_ref[0] = _sigmoid(pg).astype(BF16)


def _projection(x, mod3, norm_g, w_proj, mu, f_bias):
    batch, seq, _ = x.shape
    tm = PROJ_ROWS
    grid = (batch, seq // tm)
    const = lambda b, s: (0, 0)
    tile = lambda b, s: (b, s, 0)
    return pl.pallas_call(
        _proj_kernel,
        grid=grid,
        in_specs=[pl.BlockSpec((1, tm, D_MODEL), tile),
                  pl.BlockSpec((1, N_MOD, D_MODEL), lambda b, s: (b, 0, 0)),
                  pl.BlockSpec((1, D_MODEL), const),
                  pl.BlockSpec((D_MODEL, N_PROJ), const, pipeline_mode=pl.Buffered(1)),
                  pl.BlockSpec((1, N_RKV + N_SMALL), const),
                  pl.BlockSpec((1, LANES), const)],
        out_specs=[pl.BlockSpec((1, tm, N_RKV), tile),
                   pl.BlockSpec((1, tm, N_SMALL), tile),
                   pl.BlockSpec((1, tm, N_FOX), tile),
                   pl.BlockSpec((1, tm, LANES), tile),
                   pl.BlockSpec((1, tm, N_GATE), tile)],
        out_shape=[jax.ShapeDtypeStruct((batch, seq, N_RKV), BF16),
                   jax.ShapeDtypeStruct((batch, seq, N_SMALL), F32),
                   jax.ShapeDtypeStruct((batch, seq, N_FOX), BF16),
                   jax.ShapeDtypeStruct((batch, seq, LANES), F32),
                   jax.ShapeDtypeStruct((batch, seq, N_GATE), BF16)],
        scratch_shapes=[pltpu.VMEM((tm + 8, N_RKV + N_SMALL), F32),
                        pltpu.VMEM((1, LANES), F32)],
        compiler_params=pltpu.CompilerParams(dimension_semantics=("arbitrary", "arbitrary"),
                                             vmem_limit_bytes=VMEM_LIMIT_BYTES),
        name="proj",
    )(x, mod3, norm_g, w_proj, mu, f_bias)


def _rwkv_kernel(rkv_ref, small_ref, w_lr_ref, w_gate_ref, vec_ref, seg_ref,
                 out_ref, z_ref):
    @pl.when(pl.program_id(1) == 0)
    def _():
        z_ref[...] = jnp.zeros_like(z_ref)

    c = CHUNK
    r = rkv_ref[0, :, 0:WIDTH].astype(F32)
    k_raw = rkv_ref[0, :, WIDTH:2 * WIDTH].astype(F32)
    v = rkv_ref[0, :, 2 * WIDTH:3 * WIDTH].astype(F32)
    small = small_ref[0]
    decay_base, iclr_base, kk_scale, k_mix, r_bonus, ln_w, ln_b = (
        vec_ref[i:i + 1, :] for i in range(7))

    lane = lax.broadcasted_iota(jnp.int32, (1, LANES), 1)
    lr_in = jnp.where(lane < DECAY_RANK, jnp.tanh(small[:, 0:LANES]), small[:, 0:LANES])
    lr = _dot(lr_in, w_lr_ref[...])
    w_log = -_softplus(-(decay_base + lr[:, 0:WIDTH])) - 0.5
    log_decay = -jnp.exp(w_log)
    a = _sigmoid(iclr_base + lr[:, WIDTH:2 * WIDTH])
    g = _dot(_sigmoid(small[:, LANES:2 * LANES]), w_gate_ref[...])

    seg = seg_ref[...]
    kk = k_raw * kk_scale
    kk = kk * lax.rsqrt(jnp.maximum(_dot(kk * kk, seg), 1e-24))
    k = k_raw * (1.0 + (a - 1.0) * k_mix)
    b_vec = kk * a

    cs = _cumsum_rows(_lower_tri(c, BF16), log_decay)
    cs_end = cs[c - 1:c, :]
    w_in = jnp.exp(cs)
    w_inv = jnp.exp(-cs)
    w_end = jnp.exp(cs_end - cs)
    a_t = -kk * jnp.exp(cs - log_decay)
    r_t = r * w_in
    bt_T = jnp.transpose(b_vec * w_inv).astype(BF16)
    kt_T = jnp.transpose(k * w_inv).astype(BF16)
    bh_T = jnp.transpose(b_vec * w_end).astype(BF16)
    kh_T = jnp.transpose(k * w_end).astype(BF16)
    decay_end = jnp.exp(cs_end)

    row = lax.broadcasted_iota(jnp.int32, (c, c), 0)
    col = lax.broadcasted_iota(jnp.int32, (c, c), 1)
    strict = col < row
    incl = col <= row
    eye = (col == row).astype(F32)
    same_head = (row // HEAD_DIM) == (col // HEAD_DIM)

    y_pairs = []
    for p in range(N_PAIRS):
        sl = slice(p * PAIR, (p + 1) * PAIR)
        a_p, r_p, v_p = a_t[:, sl], r_t[:, sl], v[:, sl].astype(BF16)
        z_p = z_ref[p]
        z_b = z_p.astype(BF16)
        rhs = jnp.concatenate([bt_T[sl, :], kt_T[sl, :]], axis=1)
        u_h, y_h = [], []
        for h in range(2):
            head = (lane // HEAD_DIM) == h
            a_m = jnp.where(head, a_p, 0.0).astype(BF16)
            r_m = jnp.where(head, r_p, 0.0).astype(BF16)
            big = jnp.dot(jnp.concatenate([a_m, r_m], axis=0), rhs,
                          preferred_element_type=F32)
            a_ab = jnp.where(strict, big[0:c, 0:c], 0.0)
            a_ak = jnp.where(strict, big[0:c, c:2 * c], 0.0)
            m_rb = jnp.where(incl, big[c:2 * c, 0:c], 0.0)
            m_rk = jnp.where(incl, big[c:2 * c, c:2 * c], 0.0)
            t_inv = eye + a_ab
            m_pow = _dot(a_ab, a_ab)
            for _ in range(5):
                both = _dot(m_pow, jnp.concatenate([m_pow, t_inv], axis=1))
                m_pow = both[:, 0:c]
                t_inv = t_inv + both[:, c:2 * c]
            t_inv = t_inv + _dot(m_pow, t_inv)
            ak_v = _dot(a_ak, v_p)
            pq = _dot(t_inv, jnp.concatenate([a_m.astype(F32), ak_v], axis=1))
            u = _dot(pq[:, 0:PAIR], z_b) + pq[:, PAIR:2 * PAIR]
            y = _dot(jnp.concatenate([r_m.astype(F32), m_rb, m_rk], axis=1),
                     jnp.concatenate([z_p, u, v_p.astype(F32)], axis=0))
            u_h.append(u)
            y_h.append(y)
        first = lane < HEAD_DIM
        u_p = jnp.where(first, u_h[0], u_h[1])
        y_pairs.append(jnp.where(first, y_h[0], y_h[1]))
        d_col = jnp.sum(eye * decay_end[:, sl], axis=1, keepdims=True)
        z_new = d_col * z_p + jnp.dot(
            jnp.concatenate([bh_T[sl, :], kh_T[sl, :]], axis=1),
            jnp.concatenate([u_p.astype(BF16), v_p], axis=0), preferred_element_type=F32)
        z_ref[p] = jnp.where(same_head, z_new, 0.0)

    y = jnp.concatenate(y_pairs, axis=1)
    mean = _dot(y, seg) * (1.0 / HEAD_DIM)
    yc = y - mean
    var = _dot(yc * yc, seg) * (1.0 / HEAD_DIM)
    y = yc * lax.rsqrt(var + GN_EPS) * ln_w + ln_b
    y = y + _dot(r * k * r_bonus, seg) * v
    out_ref[0] = (y * g).astype(BF16)


def _rwkv_mix(rkv, small, w_lr, w_gate, vecs, seg):
    batch, seq, _ = rkv.shape
    grid = (batch, seq // CHUNK)
    const = lambda b, s: (0, 0)
    tile = lambda b, s: (b, s, 0)
    return pl.pallas_call(
        _rwkv_kernel,
        grid=grid,
        in_specs=[pl.BlockSpec((1, CHUNK, N_RKV), tile),
                  pl.BlockSpec((1, CHUNK, N_SMALL), tile),
                  pl.BlockSpec(w_lr.shape, const),
                  pl.BlockSpec(w_gate.shape, const),
                  pl.BlockSpec(vecs.shape, const),
                  pl.BlockSpec(seg.shape, const)],
        out_specs=pl.BlockSpec((1, CHUNK, WIDTH), tile),
        out_shape=jax.ShapeDtypeStruct((batch, seq, WIDTH), BF16),
        scratch_shapes=[pltpu.VMEM((N_PAIRS, PAIR, PAIR), F32)],
        compiler_params=pltpu.CompilerParams(dimension_semantics=("arbitrary", "arbitrary"),
                                             vmem_limit_bytes=VMEM_LIMIT_BYTES),
        name="rwkv",
    )(rkv, small, w_lr, w_gate, vecs, seg)


def _fox_kernel(q_ref, k_ref, v_ref, cq_ref, ck_ref, o_ref):
    hp = pl.program_id(1)
    qi = pl.program_id(2)
    blk = FOX_BLOCK
    lane = lax.broadcasted_iota(jnp.int32, (1, LANES), 1)
    q = q_ref[0]
    cq_all = cq_ref[0]

    q_m, c_q = [], []
    for h in range(2):
        q_m.append(jnp.where((lane // HEAD_DIM) == h, q, jnp.zeros_like(q)))
        c_q.append(jnp.sum(jnp.where(lane == 2 * hp + h, cq_all, 0.0), axis=1, keepdims=True))

    def block(j, carry, masked):
        start = pl.multiple_of(j * blk, blk)
        kb = k_ref[0, pl.ds(start, blk), :]
        vb = v_ref[0, pl.ds(start, blk), :]
        out = []
        for h in range(2):
            m, l, acc = carry[h]
            c_k = ck_ref[0, pl.ds(2 * hp + h, 1), pl.ds(start, blk)]
            s = _dot_nt(q_m[h], kb) + c_q[h] - c_k
            if masked:
                row = lax.broadcasted_iota(jnp.int32, (blk, blk), 0)
                col = lax.broadcasted_iota(jnp.int32, (blk, blk), 1)
                s = jnp.where(col <= row, s, -jnp.inf)
            m_new = jnp.maximum(m, jnp.max(s, axis=1, keepdims=True))
            alpha = jnp.exp(m - m_new)
            pr = jnp.exp(s - m_new)
            l = alpha * l + jnp.sum(pr, axis=1, keepdims=True)
            acc = alpha * acc + jnp.dot(pr.astype(BF16), vb, preferred_element_type=F32)
            out.append((m_new, l, acc))
        return tuple(out)

    init = tuple((jnp.full((blk, 1), -jnp.inf, F32), jnp.zeros((blk, 1), F32),
                  jnp.zeros((blk, LANES), F32)) for _ in range(2))
    carry = lax.fori_loop(0, qi, lambda j, cr: block(j, cr, False), init)
    carry = block(qi, carry, True)
    o = [acc / l for (_, l, acc) in carry]
    o_ref[0] = jnp.where(lane < HEAD_DIM, o[0], o[1]).astype(BF16)


def _forgetting_attention(fox, cum, cum_t):
    batch, seq, _ = fox.shape
    blk = FOX_BLOCK
    grid = (batch, N_PAIRS, seq // blk)
    return pl.pallas_call(
        _fox_kernel,
        grid=grid,
        in_specs=[pl.BlockSpec((1, blk, PAIR), lambda b, p, i: (b, i, p)),
                  pl.BlockSpec((1, seq, PAIR), lambda b, p, i: (b, 0, N_PAIRS + p)),
                  pl.BlockSpec((1, seq, PAIR), lambda b, p, i: (b, 0, 2 * N_PAIRS + p)),
                  pl.BlockSpec((1, blk, LANES), lambda b, p, i: (b, i, 0)),
                  pl.BlockSpec((1, HEADS, seq), lambda b, p, i: (b, 0, 0))],
        out_specs=pl.BlockSpec((1, blk, PAIR), lambda b, p, i: (b, i, p)),
        out_shape=jax.ShapeDtypeStruct((batch, seq, WIDTH), BF16),
        compiler_params=pltpu.CompilerParams(
            dimension_semantics=("arbitrary", "arbitrary", "arbitrary"),
            vmem_limit_bytes=VMEM_LIMIT_BYTES),
        name="fox",
    )(fox, fox, fox, cum, cum_t)


def _tail_kernel(x_ref, ya_ref, yb_ref, gate_ref, mod_ref, g2_ref, gf_ref,
                 woa_ref, wob_ref, wout_ref, w1_ref, w2_ref, o_ref):
    x = x_ref[0]
    gate1 = mod_ref[0, 2:3, :]
    shift2 = mod_ref[0, 3:4, :]
    scale2 = mod_ref[0, 4:5, :]
    gate2 = mod_ref[0, 5:6, :]

    merged = (gate_ref[0, :, 0:D_MODEL].astype(F32)
              * jnp.dot(ya_ref[0], woa_ref[...], preferred_element_type=F32)
              + gate_ref[0, :, D_MODEL:N_GATE].astype(F32)
              * jnp.dot(yb_ref[0], wob_ref[...], preferred_element_type=F32))
    x = x + gate1 * _dot(merged, wout_ref[...])

    inv = lax.rsqrt(jnp.mean(x * x, axis=-1, keepdims=True) + NORM_EPS)
    h2 = ((x * inv) * g2_ref[...] * (1.0 + scale2) + shift2).astype(BF16)
    ff = jnp.zeros_like(x)
    for j in range(D_FF // FF_CHUNK):
        cols = slice(j * FF_CHUNK, (j + 1) * FF_CHUNK)
        hid = jnp.maximum(jnp.dot(h2, w1_ref[:, cols], preferred_element_type=F32), 0.0)
        ff = ff + _dot(hid * hid, w2_ref[cols, :])
    x = x + gate2 * ff

    inv = lax.rsqrt(jnp.mean(x * x, axis=-1, keepdims=True) + NORM_EPS)
    o_ref[0] = (x * inv) * gf_ref[...]


def _tail(x, y_a, y_b, gates, mod3, norm2_g, final_g, w_oa, w_ob, w_out, w_ff1, w_ff2):
    batch, seq, _ = x.shape
    tm = TAIL_ROWS
    grid = (batch, seq // tm)
    const = lambda b, s: (0, 0)
    tile = lambda b, s: (b, s, 0)
    resident = lambda a: pl.BlockSpec(a.shape, const, pipeline_mode=pl.Buffered(1))
    return pl.pallas_call(
        _tail_kernel,
        grid=grid,
        in_specs=[pl.BlockSpec((1, tm, D_MODEL), tile),
                  pl.BlockSpec((1, tm, WIDTH), tile),
                  pl.BlockSpec((1, tm, WIDTH), tile),
                  pl.BlockSpec((1, tm, N_GATE), tile),
                  pl.BlockSpec((1, N_MOD, D_MODEL), lambda b, s: (b, 0, 0)),
                  pl.BlockSpec((1, D_MODEL), const),
                  pl.BlockSpec((1, D_MODEL), const),
                  resident(w_oa), resident(w_ob), resident(w_out),
                  resident(w_ff1), resident(w_ff2)],
        out_specs=pl.BlockSpec((1, tm, D_MODEL), tile),
        out_shape=jax.ShapeDtypeStruct((batch, seq, D_MODEL), F32),
        compiler_params=pltpu.CompilerParams(dimension_semantics=("arbitrary", "arbitrary"),
                                             vmem_limit_bytes=VMEM_LIMIT_BYTES),
        name="tail",
    )(x, y_a, y_b, gates, mod3, norm2_g, final_g, w_oa, w_ob, w_out, w_ff1, w_ff2)


def _reorder_rwkv_cols(t):
    o = 0
    r = t[..., o:o + WIDTH]; o += WIDTH
    wd = t[..., o:o + DECAY_RANK]; o += DECAY_RANK
    k = t[..., o:o + WIDTH]; o += WIDTH
    v = t[..., o:o + WIDTH]; o += WIDTH
    ad = t[..., o:o + ICLR_RANK]; o += ICLR_RANK
    gd = t[..., o:o + GATE_RANK]
    return jnp.concatenate([r, k, v, wd, ad, gd], axis=-1)


def kernel(x, c, w_ada, b_ada, norm1_g, w_in, mu_shift, w_decay_up, decay_base, w_iclr_up, iclr_base, w_gate_up, kk_scale, k_iclr_mix, r_bonus, lnx_w, lnx_b, fox_f_bias, w_o_rwkv, w_o_fox, w_out, norm2_g, w_ff1, w_ff2, final_g):
    assert w_ada.shape[0] == 1, "the tail kernel fuses the final norm: single layer only"
    l = 0
    n_rwkv = N_RKV + N_SMALL
    seg_id = jnp.arange(WIDTH) // HEAD_DIM
    seg = (seg_id[:, None] == seg_id[None, :]).astype(BF16)

    mod3 = _modulation(c, w_ada[l], b_ada[l]).reshape(-1, N_MOD, D_MODEL)

    w = w_in[l]
    w_proj = jnp.concatenate(
        [_reorder_rwkv_cols(w[:, :n_rwkv]),
         w[:, n_rwkv:n_rwkv + N_FOX],
         jnp.pad(w[:, n_rwkv + N_FOX:n_rwkv + N_FOX + HEADS], ((0, 0), (0, LANES - HEADS))),
         w[:, n_rwkv + N_FOX + HEADS:]], axis=1).astype(BF16)
    mu = _reorder_rwkv_cols(mu_shift[l]).reshape(1, n_rwkv)
    f_bias = jnp.pad(fox_f_bias[l], (0, LANES - HEADS)).reshape(1, LANES)
    rkv, small, fox, cum, gates = _projection(
        x, mod3, norm1_g[l].reshape(1, D_MODEL), w_proj, mu, f_bias)

    zeros = jnp.zeros((DECAY_RANK, WIDTH), F32)
    w_lr = jnp.concatenate(
        [jnp.concatenate([w_decay_up[l], zeros], axis=1),
         jnp.concatenate([zeros, w_iclr_up[l]], axis=1)], axis=0).astype(BF16)
    vecs = jnp.stack([decay_base[l], iclr_base[l], kk_scale[l], k_iclr_mix[l],
                      r_bonus[l].reshape(WIDTH), lnx_w[l], lnx_b[l],
                      jnp.zeros((WIDTH,), F32)], axis=0)
    y_a = _rwkv_mix(rkv, small, w_lr, w_gate_up[l].astype(BF16), vecs, seg)

    cum_t = jnp.transpose(cum[:, :, :HEADS], (0, 2, 1))
    y_b = _forgetting_attention(fox, cum, cum_t)

    return _tail(x, y_a, y_b, gates, mod3, norm2_g[l].reshape(1, D_MODEL),
                 final_g.reshape(1, D_MODEL),
                 w_o_rwkv[l].astype(BF16), w_o_fox[l].astype(BF16), w_out[l].astype(BF16),
                 w_ff1[l].astype(BF16), w_ff2[l].astype(BF16))
```

```python
import functools

import jax
import jax.numpy as jnp
from jax import lax
from jax.experimental import pallas as pl
from jax.experimental.pallas import tpu as pltpu

F32 = jnp.float32
BF16 = jnp.bfloat16

D_MODEL = 1024
HEAD_DIM = 64
HEADS = 8
WIDTH = HEADS * HEAD_DIM
DECAY_RANK = 64
ICLR_RANK = 64
GATE_RANK = 128
D_FF = 4 * D_MODEL
N_MOD = 6
NORM_EPS = 1e-6
GN_EPS = 64e-5

LANES = 128
PAIR = 2 * HEAD_DIM
N_PAIRS = HEADS // 2

N_SMALL = DECAY_RANK + ICLR_RANK + GATE_RANK
N_RKV = 3 * WIDTH
N_FOX = 3 * WIDTH
N_GATE = 2 * D_MODEL
OFF_RKV = 0
OFF_SMALL = OFF_RKV + N_RKV
OFF_FOX = OFF_SMALL + N_SMALL
OFF_FF = OFF_FOX + N_FOX
OFF_GATE = OFF_FF + LANES
N_PROJ = OFF_GATE + N_GATE

PROJ_ROWS = 256
CHUNK = 128
FOX_BLOCK = 512
TAIL_ROWS = 512
FF_CHUNK = 1024

VMEM_LIMIT_BYTES = 56 * 1024 * 1024


def _dot(a, b):
    return jnp.dot(a.astype(BF16), b.astype(BF16), preferred_element_type=F32)


def _dot_nt(a, b):
    return lax.dot_general(a.astype(BF16), b.astype(BF16), (((1,), (1,)), ((), ())),
                           preferred_element_type=F32)


def _softplus(z):
    return jnp.maximum(z, 0.0) + jnp.log(1.0 + jnp.exp(-jnp.abs(z)))


def _sigmoid(z):
    return 1.0 / (1.0 + jnp.exp(-z))


def _split3(x):
    hi = x.astype(BF16)
    r1 = x - hi.astype(F32)
    mid = r1.astype(BF16)
    lo = (r1 - mid.astype(F32)).astype(BF16)
    return hi, mid, lo


def _cumsum_rows(tri, x):
    hi, mid, lo = _split3(x)
    d = lambda p: jnp.dot(tri, p, preferred_element_type=F32)
    return d(hi) + d(mid) + d(lo)


def _lower_tri(n, dtype):
    row = lax.broadcasted_iota(jnp.int32, (n, n), 0)
    col = lax.broadcasted_iota(jnp.int32, (n, n), 1)
    return (col <= row).astype(dtype)


def _mod_kernel(c_ref, w_ref, b_ref, o_ref):
    c = c_ref[...]
    c_act = c * _sigmoid(c)
    o_ref[...] = jnp.dot(c_act, w_ref[...], preferred_element_type=F32,
                         precision=lax.Precision.HIGHEST) + b_ref[...]


def _modulation(c, w_ada, b_ada):
    batch = c.shape[0]
    n = w_ada.shape[1]
    tn = 1536
    return pl.pallas_call(
        _mod_kernel,
        grid=(n // tn,),
        in_specs=[pl.BlockSpec((batch, D_MODEL), lambda j: (0, 0)),
                  pl.BlockSpec((D_MODEL, tn), lambda j: (0, j)),
                  pl.BlockSpec((1, tn), lambda j: (0, j))],
        out_specs=pl.BlockSpec((batch, tn), lambda j: (0, j)),
        out_shape=jax.ShapeDtypeStruct((batch, n), F32),
        compiler_params=pltpu.CompilerParams(dimension_semantics=("arbitrary",),
                                             vmem_limit_bytes=VMEM_LIMIT_BYTES),
        name="mod",
    )(c, w_ada, b_ada.reshape(1, n))


def _proj_kernel(x_ref, mod_ref, g_ref, w_ref, mu_ref, fb_ref,
                 rkv_ref, small_ref, fox_ref, cum_ref, gate_ref,
                 shift_ref, carry_ref):
    s = pl.program_id(1)
    rows = x_ref.shape[1]

    @pl.when(s == 0)
    def _():
        shift_ref[0:8, :] = jnp.zeros((8, N_RKV + N_SMALL), F32)
        carry_ref[...] = jnp.zeros_like(carry_ref)

    x = x_ref[0]
    shift1 = mod_ref[0, 0:1, :]
    scale1 = mod_ref[0, 1:2, :]
    inv = lax.rsqrt(jnp.mean(x * x, axis=-1, keepdims=True) + NORM_EPS)
    h = ((x * inv) * g_ref[...] * (1.0 + scale1) + shift1).astype(BF16)

    n_tok = N_RKV + N_SMALL
    p = jnp.dot(h, w_ref[:, OFF_RKV:OFF_RKV + n_tok], preferred_element_type=F32)
    shift_ref[8:8 + rows, :] = p
    prev = shift_ref[7:7 + rows, :]
    shift_ref[7:8, :] = p[rows - 1:rows, :]
    mixed = p + mu_ref[...] * (prev - p)
    rkv_ref[0] = mixed[:, 0:N_RKV].astype(BF16)
    small_ref[0] = mixed[:, N_RKV:n_tok]

    pf = jnp.dot(h, w_ref[:, OFF_FOX:OFF_FOX + N_FOX], preferred_element_type=F32)
    fox_ref[0, :, 0:WIDTH] = (pf[:, 0:WIDTH] * (HEAD_DIM ** -0.5)).astype(BF16)
    fox_ref[0, :, WIDTH:N_FOX] = pf[:, WIDTH:N_FOX].astype(BF16)

    ff = jnp.dot(h, w_ref[:, OFF_FF:OFF_FF + LANES], preferred_element_type=F32)
    logf = -_softplus(-(ff + fb_ref[...]))
    cum = _cumsum_rows(_lower_tri(rows, BF16), logf) + carry_ref[...]
    cum_ref[0] = cum
    carry_ref[...] = cum[rows - 1:rows, :]

    pg = jnp.dot(h, w_ref[:, OFF_GATE:OFF_GATE + N_GATE], preferred_element_type=F32)
    gate_ref[0] = _sigmoid(pg).astype(BF16)


def _projection(x, mod3, norm_g, w_proj, mu, f_bias):
    batch, seq, _ = x.shape
    tm = PROJ_ROWS
    grid = (batch, seq // tm)
    const = lambda b, s: (0, 0)
    tile = lambda b, s: (b, s, 0)
    return pl.pallas_call(
        _proj_kernel,
        grid=grid,
        in_specs=[pl.BlockSpec((1, tm, D_MODEL), tile),
                  pl.BlockSpec((1, N_MOD, D_MODEL), lambda b, s: (b, 0, 0)),
                  pl.BlockSpec((1, D_MODEL), const),
                  pl.BlockSpec((D_MODEL, N_PROJ), const, pipeline_mode=pl.Buffered(1)),
                  pl.BlockSpec((1, N_RKV + N_SMALL), const),
                  pl.BlockSpec((1, LANES), const)],
        out_specs=[pl.BlockSpec((1, tm, N_RKV), tile),
                   pl.BlockSpec((1, tm, N_SMALL), tile),
                   pl.BlockSpec((1, tm, N_FOX), tile),
                   pl.BlockSpec((1, tm, LANES), tile),
                   pl.BlockSpec((1, tm, N_GATE), tile)],
        out_shape=[jax.ShapeDtypeStruct((batch, seq, N_RKV), BF16),
                   jax.ShapeDtypeStruct((batch, seq, N_SMALL), F32),
                   jax.ShapeDtypeStruct((batch, seq, N_FOX), BF16),
                   jax.ShapeDtypeStruct((batch, seq, LANES), F32),
                   jax.ShapeDtypeStruct((batch, seq, N_GATE), BF16)],
        scratch_shapes=[pltpu.VMEM((tm + 8, N_RKV + N_SMALL), F32),
                        pltpu.VMEM((1, LANES), F32)],
        compiler_params=pltpu.CompilerParams(dimension_semantics=("arbitrary", "arbitrary"),
                                             vmem_limit_bytes=VMEM_LIMIT_BYTES),
        name="proj",
    )(x, mod3, norm_g, w_proj, mu, f_bias)


def _rwkv_kernel(rkv_ref, small_ref, w_lr_ref, w_gate_ref, vec_ref, seg_ref,
                 out_ref, z_ref):
    @pl.when(pl.program_id(1) == 0)
    def _():
        z_ref[...] = jnp.zeros_like(z_ref)

    c = CHUNK
    r = rkv_ref[0, :, 0:WIDTH].astype(F32)
    k_raw = rkv_ref[0, :, WIDTH:2 * WIDTH].astype(F32)
    v = rkv_ref[0, :, 2 * WIDTH:3 * WIDTH].astype(F32)
    small = small_ref[0]
    decay_base, iclr_base, kk_scale, k_mix, r_bonus, ln_w, ln_b = (
        vec_ref[i:i + 1, :] for i in range(7))

    lane = lax.broadcasted_iota(jnp.int32, (1, LANES), 1)
    lr_in = jnp.where(lane < DECAY_RANK, jnp.tanh(small[:, 0:LANES]), small[:, 0:LANES])
    lr = _dot(lr_in, w_lr_ref[...])
    w_log = -_softplus(-(decay_base + lr[:, 0:WIDTH])) - 0.5
    log_decay = -jnp.exp(w_log)
    a = _sigmoid(iclr_base + lr[:, WIDTH:2 * WIDTH])
    g = _dot(_sigmoid(small[:, LANES:2 * LANES]), w_gate_ref[...])

    seg = seg_ref[...]
    kk = k_raw * kk_scale
    kk = kk * lax.rsqrt(jnp.maximum(_dot(kk * kk, seg), 1e-24))
    k = k_raw * (1.0 + (a - 1.0) * k_mix)
    b_vec = kk * a

    cs = _cumsum_rows(_lower_tri(c, BF16), log_decay)
    cs_end = cs[c - 1:c, :]
    w_in = jnp.exp(cs)
    w_inv = jnp.exp(-cs)
    w_end = jnp.exp(cs_end - cs)
    a_t = -kk * jnp.exp(cs - log_decay)
    r_t = r * w_in
    bt_T = jnp.transpose(b_vec * w_inv).astype(BF16)
    kt_T = jnp.transpose(k * w_inv).astype(BF16)
    bh_T = jnp.transpose(b_vec * w_end).astype(BF16)
    kh_T = jnp.transpose(k * w_end).astype(BF16)
    decay_end = jnp.exp(cs_end)

    row = lax.broadcasted_iota(jnp.int32, (c, c), 0)
    col = lax.broadcasted_iota(jnp.int32, (c, c), 1)
    strict = col < row
    incl = col <= row
    eye = (col == row).astype(F32)
    same_head = (row // HEAD_DIM) == (col // HEAD_DIM)

    pairs = range(N_PAIRS)
    sls = [slice(p * PAIR, (p + 1) * PAIR) for p in pairs]
    v_b = [v[:, sl].astype(BF16) for sl in sls]
    z_f = [z_ref[p] for p in pairs]
    z_b = [z.astype(BF16) for z in z_f]
    heads = [(p, h) for p in pairs for h in range(2)]
    a_m, r_m, a_ab, a_ak, m_rb, m_rk = {}, {}, {}, {}, {}, {}
    for (p, h) in heads:
        head = (lane // HEAD_DIM) == h
        a_m[p, h] = jnp.where(head, a_t[:, sls[p]], 0.0).astype(BF16)
        r_m[p, h] = jnp.where(head, r_t[:, sls[p]], 0.0).astype(BF16)
        rhs = jnp.concatenate([bt_T[sls[p], :], kt_T[sls[p], :]], axis=1)
        big = jnp.dot(jnp.concatenate([a_m[p, h], r_m[p, h]], axis=0), rhs,
                      preferred_element_type=F32)
        a_ab[p, h] = jnp.where(strict, big[0:c, 0:c], 0.0)
        a_ak[p, h] = jnp.where(strict, big[0:c, c:2 * c], 0.0).astype(BF16)
        m_rb[p, h] = jnp.where(incl, big[c:2 * c, 0:c], 0.0).astype(BF16)
        m_rk[p, h] = jnp.where(incl, big[c:2 * c, c:2 * c], 0.0).astype(BF16)
    t_inv = {hd: eye + a_ab[hd] for hd in heads}
    m_pow = {hd: _dot(a_ab[hd], a_ab[hd]) for hd in heads}
    for _ in range(5):
        for hd in heads:
            both = _dot(m_pow[hd], jnp.concatenate([m_pow[hd], t_inv[hd]], axis=1))
            m_pow[hd] = both[:, 0:c]
            t_inv[hd] = t_inv[hd] + both[:, c:2 * c]
    for hd in heads:
        t_inv[hd] = t_inv[hd] + _dot(m_pow[hd], t_inv[hd])
    ak_v = {(p, h): _dot(a_ak[p, h], v_b[p]) for (p, h) in heads}
    pq = {hd: _dot(t_inv[hd], jnp.concatenate([a_m[hd].astype(F32), ak_v[hd]], axis=1))
          for hd in heads}
    u = {(p, h): _dot(pq[p, h][:, 0:PAIR], z_b[p]) + pq[p, h][:, PAIR:2 * PAIR]
         for (p, h) in heads}
    y_hd = {(p, h): jnp.dot(
        jnp.concatenate([r_m[p, h], m_rb[p, h], m_rk[p, h]], axis=1),
        jnp.concatenate([z_b[p], u[p, h].astype(BF16), v_b[p]], axis=0),
        preferred_element_type=F32) for (p, h) in heads}
    first = lane < HEAD_DIM
    y_pairs = []
    for p in pairs:
        u_p = jnp.where(first, u[p, 0], u[p, 1])
        y_pairs.append(jnp.where(first, y_hd[p, 0], y_hd[p, 1]))
        d_col = jnp.sum(eye * decay_end[:, sls[p]], axis=1, keepdims=True)
        z_new = d_col * z_f[p] + jnp.dot(
            jnp.concatenate([bh_T[sls[p], :], kh_T[sls[p], :]], axis=1),
            jnp.concatenate([u_p.astype(BF16), v_b[p]], axis=0), preferred_element_type=F32)
        z_ref[p] = jnp.where(same_head, z_new, 0.0)

    y = jnp.concatenate(y_pairs, axis=1)
    mean = _dot(y, seg) * (1.0 / HEAD_DIM)
    yc = y - mean
    var = _dot(yc * yc, seg) * (1.0 / HEAD_DIM)
    y = yc * lax.rsqrt(var + GN_EPS) * ln_w + ln_b
    y = y + _dot(r * k * r_bonus, seg) * v
    out_ref[0] = (y * g).astype(BF16)


def _rwkv_mix(rkv, small, w_lr, w_gate, vecs, seg):
    batch, seq, _ = rkv.shape
    grid = (batch, seq // CHUNK)
    const = lambda b, s: (0, 0)
    tile = lambda b, s: (b, s, 0)
    return pl.pallas_call(
        _rwkv_kernel,
        grid=grid,
        in_specs=[pl.BlockSpec((1, CHUNK, N_RKV), tile),
                  pl.BlockSpec((1, CHUNK, N_SMALL), tile),
                  pl.BlockSpec(w_lr.shape, const),
                  pl.BlockSpec(w_gate.shape, const),
                  pl.BlockSpec(vecs.shape, const),
                  pl.BlockSpec(seg.shape, const)],
        out_specs=pl.BlockSpec((1, CHUNK, WIDTH), tile),
        out_shape=jax.ShapeDtypeStruct((batch, seq, WIDTH), BF16),
        scratch_shapes=[pltpu.VMEM((N_PAIRS, PAIR, PAIR), F32)],
        compiler_params=pltpu.CompilerParams(dimension_semantics=("arbitrary", "arbitrary"),
                                             vmem_limit_bytes=VMEM_LIMIT_BYTES),
        name="rwkv",
    )(rkv, small, w_lr, w_gate, vecs, seg)


def _fox_kernel(q_ref, k_ref, v_ref, cq_ref, ck_ref, o_ref):
    hp = pl.program_id(1)
    qi = pl.program_id(2)
    blk = FOX_BLOCK
    lane = lax.broadcasted_iota(jnp.int32, (1, LANES), 1)
    q = q_ref[0]
    cq_all = cq_ref[0]

    q_m, c_q = [], []
    for h in range(2):
        q_m.append(jnp.where((lane // HEAD_DIM) == h, q, jnp.zeros_like(q)))
        c_q.append(jnp.sum(jnp.where(lane == 2 * hp + h, cq_all, 0.0), axis=1, keepdims=True))

    def block(j, carry, masked):
        start = pl.multiple_of(j * blk, blk)
        kb = k_ref[0, pl.ds(start, blk), :]
        vb = v_ref[0, pl.ds(start, blk), :]
        out = []
        for h in range(2):
            m, l, acc = carry[h]
            c_k = ck_ref[0, pl.ds(2 * hp + h, 1), pl.ds(start, blk)]
            s = _dot_nt(q_m[h], kb) + c_q[h] - c_k
            if masked:
                row = lax.broadcasted_iota(jnp.int32, (blk, blk), 0)
                col = lax.broadcasted_iota(jnp.int32, (blk, blk), 1)
                s = jnp.where(col <= row, s, -jnp.inf)
            m_new = jnp.maximum(m, jnp.max(s, axis=1, keepdims=True))
            alpha = jnp.exp(m - m_new)
            pr = jnp.exp(s - m_new)
            l = alpha * l + jnp.sum(pr, axis=1, keepdims=True)
            acc = alpha * acc + jnp.dot(pr.astype(BF16), vb, preferred_element_type=F32)
            out.append((m_new, l, acc))
        return tuple(out)

    init = tuple((jnp.full((blk, 1), -jnp.inf, F32), jnp.zeros((blk, 1), F32),
                  jnp.zeros((blk, LANES), F32)) for _ in range(2))
    carry = lax.fori_loop(0, qi, lambda j, cr: block(j, cr, False), init)
    carry = block(qi, carry, True)
    o = [acc / l for (_, l, acc) in carry]
    o_ref[0] = jnp.where(lane < HEAD_DIM, o[0], o[1]).astype(BF16)


def _forgetting_attention(fox, cum, cum_t):
    batch, seq, _ = fox.shape
    blk = FOX_BLOCK
    grid = (batch, N_PAIRS, seq // blk)
    return pl.pallas_call(
        _fox_kernel,
        grid=grid,
        in_specs=[pl.BlockSpec((1, blk, PAIR), lambda b, p, i: (b, i, p)),
                  pl.BlockSpec((1, seq, PAIR), lambda b, p, i: (b, 0, N_PAIRS + p)),
                  pl.BlockSpec((1, seq, PAIR), lambda b, p, i: (b, 0, 2 * N_PAIRS + p)),
                  pl.BlockSpec((1, blk, LANES), lambda b, p, i: (b, i, 0)),
                  pl.BlockSpec((1, HEADS, seq), lambda b, p, i: (b, 0, 0))],
        out_specs=pl.BlockSpec((1, blk, PAIR), lambda b, p, i: (b, i, p)),
        out_shape=jax.ShapeDtypeStruct((batch, seq, WIDTH), BF16),
        compiler_params=pltpu.CompilerParams(
            dimension_semantics=("arbitrary", "arbitrary", "arbitrary"),
            vmem_limit_bytes=VMEM_LIMIT_BYTES),
        name="fox",
    )(fox, fox, fox, cum, cum_t)


def _tail_kernel(x_ref, ya_ref, yb_ref, gate_ref, mod_ref, g2_ref, gf_ref,
                 woa_ref, wob_ref, wout_ref, w1_ref, w2_ref, o_ref):
    x = x_ref[0]
    gate1 = mod_ref[0, 2:3, :]
    shift2 = mod_ref[0, 3:4, :]
    scale2 = mod_ref[0, 4:5, :]
    gate2 = mod_ref[0, 5:6, :]

    merged = (gate_ref[0, :, 0:D_MODEL].astype(F32)
              * jnp.dot(ya_ref[0], woa_ref[...], preferred_element_type=F32)
              + gate_ref[0, :, D_MODEL:N_GATE].astype(F32)
              * jnp.dot(yb_ref[0], wob_ref[...], preferred_element_type=F32))
    x = x + gate1 * _dot(merged, wout_ref[...])

    inv = lax.rsqrt(jnp.mean(x * x, axis=-1, keepdims=True) + NORM_EPS)
    h2 = ((x * inv) * g2_ref[...] * (1.0 + scale2) + shift2).astype(BF16)
    ff = jnp.zeros_like(x)
    for j in range(D_FF // FF_CHUNK):
        cols = slice(j * FF_CHUNK, (j + 1) * FF_CHUNK)
        hid = jnp.maximum(jnp.dot(h2, w1_ref[:, cols], preferred_element_type=F32), 0.0)
        ff = ff + _dot(hid * hid, w2_ref[cols, :])
    x = x + gate2 * ff

    inv = lax.rsqrt(jnp.mean(x * x, axis=-1, keepdims=True) + NORM_EPS)
    o_ref[0] = (x * inv) * gf_ref[...]


def _tail(x, y_a, y_b, gates, mod3, norm2_g, final_g, w_oa, w_ob, w_out, w_ff1, w_ff2):
    batch, seq, _ = x.shape
    tm = TAIL_ROWS
    grid = (batch, seq // tm)
    const = lambda b, s: (0, 0)
    tile = lambda b, s: (b, s, 0)
    resident = lambda a: pl.BlockSpec(a.shape, const, pipeline_mode=pl.Buffered(1))
    return pl.pallas_call(
        _tail_kernel,
        grid=grid,
        in_specs=[pl.BlockSpec((1, tm, D_MODEL), tile),
                  pl.BlockSpec((1, tm, WIDTH), tile),
                  pl.BlockSpec((1, tm, WIDTH), tile),
                  pl.BlockSpec((1, tm, N_GATE), tile),
                  pl.BlockSpec((1, N_MOD, D_MODEL), lambda b, s: (b, 0, 0)),
                  pl.BlockSpec((1, D_MODEL), const),
                  pl.BlockSpec((1, D_MODEL), const),
                  resident(w_oa), resident(w_ob), resident(w_out),
                  resident(w_ff1), resident(w_ff2)],
        out_specs=pl.BlockSpec((1, tm, D_MODEL), tile),
        out_shape=jax.ShapeDtypeStruct((batch, seq, D_MODEL), F32),
        compiler_params=pltpu.CompilerParams(dimension_semantics=("arbitrary", "arbitrary"),
                                             vmem_limit_bytes=VMEM_LIMIT_BYTES),
        name="tail",
    )(x, y_a, y_b, gates, mod3, norm2_g, final_g, w_oa, w_ob, w_out, w_ff1, w_ff2)


def _reorder_rwkv_cols(t):
    o = 0
    r = t[..., o:o + WIDTH]; o += WIDTH
    wd = t[..., o:o + DECAY_RANK]; o += DECAY_RANK
    k = t[..., o:o + WIDTH]; o += WIDTH
    v = t[..., o:o + WIDTH]; o += WIDTH
    ad = t[..., o:o + ICLR_RANK]; o += ICLR_RANK
    gd = t[..., o:o + GATE_RANK]
    return jnp.concatenate([r, k, v, wd, ad, gd], axis=-1)


def kernel(x, c, w_ada, b_ada, norm1_g, w_in, mu_shift, w_decay_up, decay_base, w_iclr_up, iclr_base, w_gate_up, kk_scale, k_iclr_mix, r_bonus, lnx_w, lnx_b, fox_f_bias, w_o_rwkv, w_o_fox, w_out, norm2_g, w_ff1, w_ff2, final_g):
    assert w_ada.shape[0] == 1, "the tail kernel fuses the final norm: single layer only"
    l = 0
    n_rwkv = N_RKV + N_SMALL
    seg_id = jnp.arange(WIDTH) // HEAD_DIM
    seg = (seg_id[:, None] == seg_id[None, :]).astype(BF16)

    mod3 = _modulation(c, w_ada[l], b_ada[l]).reshape(-1, N_MOD, D_MODEL)

    w = w_in[l]
    w_proj = jnp.concatenate(
        [_reorder_rwkv_cols(w[:, :n_rwkv]),
         w[:, n_rwkv:n_rwkv + N_FOX],
         jnp.pad(w[:, n_rwkv + N_FOX:n_rwkv + N_FOX + HEADS], ((0, 0), (0, LANES - HEADS))),
         w[:, n_rwkv + N_FOX + HEADS:]], axis=1).astype(BF16)
    mu = _reorder_rwkv_cols(mu_shift[l]).reshape(1, n_rwkv)
    f_bias = jnp.pad(fox_f_bias[l], (0, LANES - HEADS)).reshape(1, LANES)
    rkv, small, fox, cum, gates = _projection(
        x, mod3, norm1_g[l].reshape(1, D_MODEL), w_proj, mu, f_bias)

    zeros = jnp.zeros((DECAY_RANK, WIDTH), F32)
    w_lr = jnp.concatenate(
        [jnp.concatenate([w_decay_up[l], zeros], axis=1),
         jnp.concatenate([zeros, w_iclr_up[l]], axis=1)], axis=0).astype(BF16)
    vecs = jnp.stack([decay_base[l], iclr_base[l], kk_scale[l], k_iclr_mix[l],
                      r_bonus[l].reshape(WIDTH), lnx_w[l], lnx_b[l],
                      jnp.zeros((WIDTH,), F32)], axis=0)
    y_a = _rwkv_mix(rkv, small, w_lr, w_gate_up[l].astype(BF16), vecs, seg)

    cum_t = jnp.transpose(cum[:, :, :HEADS], (0, 2, 1))
    y_b = _forgetting_attention(fox, cum, cum_t)

    return _tail(x, y_a, y_b, gates, mod3, norm2_g[l].reshape(1, D_MODEL),
                 final_g.reshape(1, D_MODEL),
                 w_o_rwkv[l].astype(BF16), w_o_fox[l].astype(BF16), w_out[l].astype(BF16),
                 w_ff1[l].astype(BF16), w_ff2[l].astype(BF16))
```

```python
import functools

import jax
import jax.numpy as jnp
from jax import lax
from jax.experimental import pallas as pl
from jax.experimental.pallas import tpu as pltpu

F32 = jnp.float32
BF16 = jnp.bfloat16

D_MODEL = 1024
HEAD_DIM = 64
HEADS = 8
WIDTH = HEADS * HEAD_DIM
DECAY_RANK = 64
ICLR_RANK = 64
GATE_RANK = 128
D_FF = 4 * D_MODEL
N_MOD = 6
NORM_EPS = 1e-6
GN_EPS = 64e-5

LANES = 128
PAIR = 2 * HEAD_DIM
N_PAIRS = HEADS // 2

LOG2E = 1.4426950408889634

N_SMALL = DECAY_RANK + ICLR_RANK + GATE_RANK
N_RKV = 3 * WIDTH
N_FOX = 3 * WIDTH
N_GATE = 2 * D_MODEL
OFF_RKV = 0
OFF_SMALL = OFF_RKV + N_RKV
OFF_FOX = OFF_SMALL + N_SMALL
OFF_FF = OFF_FOX + N_FOX
OFF_GATE = OFF_FF + LANES
N_PROJ = OFF_GATE + N_GATE

PROJ_ROWS = 256
CHUNK = 128
FOX_BLOCK = 512
FOX_SUB = 64
FOX_VROWS = HEAD_DIM + 16
TAIL_ROWS = 512
FF_CHUNK = 1024

VMEM_LIMIT_BYTES = 56 * 1024 * 1024


def _dot(a, b):
    return jnp.dot(a.astype(BF16), b.astype(BF16), preferred_element_type=F32)


def _dot_nt(a, b):
    return lax.dot_general(a.astype(BF16), b.astype(BF16), (((1,), (1,)), ((), ())),
                           preferred_element_type=F32)


def _softplus(z):
    return jnp.maximum(z, 0.0) + jnp.log(1.0 + jnp.exp(-jnp.abs(z)))


def _sigmoid(z):
    return 1.0 / (1.0 + jnp.exp(-z))


def _bf16_parts(x, n):
    parts = []
    for _ in range(n):
        p = x.astype(BF16)
        parts.append(p)
        x = x - p.astype(F32)
    return parts


def _cumsum_rows(tri, x, n_parts):
    return sum(jnp.dot(tri, p, preferred_element_type=F32) for p in _bf16_parts(x, n_parts))


def _head_sums(x, seg):
    half = seg.shape[0]
    return jnp.concatenate([_dot(x[:, 0:half], seg), _dot(x[:, half:2 * half], seg)], axis=1)


def _lower_tri(n, dtype):
    row = lax.broadcasted_iota(jnp.int32, (n, n), 0)
    col = lax.broadcasted_iota(jnp.int32, (n, n), 1)
    return (col <= row).astype(dtype)


def _mod_kernel(c_ref, w_ref, b_ref, o_ref):
    c = c_ref[...]
    c_act = c * _sigmoid(c)
    o_ref[...] = jnp.dot(c_act, w_ref[...], preferred_element_type=F32,
                         precision=lax.Precision.HIGHEST) + b_ref[...]


def _modulation(c, w_ada, b_ada):
    batch = c.shape[0]
    n = w_ada.shape[1]
    tn = 1536
    return pl.pallas_call(
        _mod_kernel,
        grid=(n // tn,),
        in_specs=[pl.BlockSpec((batch, D_MODEL), lambda j: (0, 0)),
                  pl.BlockSpec((D_MODEL, tn), lambda j: (0, j)),
                  pl.BlockSpec((1, tn), lambda j: (0, j))],
        out_specs=pl.BlockSpec((batch, tn), lambda j: (0, j)),
        out_shape=jax.ShapeDtypeStruct((batch, n), F32),
        compiler_params=pltpu.CompilerParams(dimension_semantics=("arbitrary",),
                                             vmem_limit_bytes=VMEM_LIMIT_BYTES),
        name="mod",
    )(c, w_ada, b_ada.reshape(1, n))


def _proj_kernel(x_ref, mod_ref, g_ref, w_ref, mu_ref, fb_ref,
                 rkv_ref, small_ref, qk_ref, vt_ref, cum_ref, gate_ref,
                 shift_ref, carry_ref):
    s = pl.program_id(1)
    rows = x_ref.shape[1]

    @pl.when(s == 0)
    def _():
        shift_ref[0:8, :] = jnp.zeros((8, N_RKV + N_SMALL), F32)
        carry_ref[...] = jnp.zeros_like(carry_ref)

    x = x_ref[0]
    shift1 = mod_ref[0, 0:1, :]
    scale1 = mod_ref[0, 1:2, :]
    inv = lax.rsqrt(jnp.mean(x * x, axis=-1, keepdims=True) + NORM_EPS)
    h = ((x * inv) * g_ref[...] * (1.0 + scale1) + shift1).astype(BF16)

    n_tok = N_RKV + N_SMALL
    p = jnp.dot(h, w_ref[:, OFF_RKV:OFF_RKV + n_tok], preferred_element_type=F32)
    shift_ref[8:8 + rows, :] = p
    prev = shift_ref[7:7 + rows, :]
    shift_ref[7:8, :] = p[rows - 1:rows, :]
    mixed = p + mu_ref[...] * (prev - p)
    rkv_ref[0] = mixed[:, 0:N_RKV].astype(BF16)
    small_ref[0] = mixed[:, N_RKV:n_tok]

    pf = jnp.dot(h, w_ref[:, OFF_FOX:OFF_FOX + N_FOX], preferred_element_type=F32)
    qk_ref[0, :, 0:WIDTH] = (pf[:, 0:WIDTH] * (LOG2E * HEAD_DIM ** -0.5)).astype(BF16)
    qk_ref[0, :, WIDTH:2 * WIDTH] = pf[:, WIDTH:2 * WIDTH].astype(BF16)
    vt_ref[0] = jnp.transpose(pf[:, 2 * WIDTH:N_FOX]).astype(BF16)

    ff = jnp.dot(h, w_ref[:, OFF_FF:OFF_FF + LANES], preferred_element_type=F32)
    logf = -_softplus(-(ff + fb_ref[...]))
    cum = _cumsum_rows(_lower_tri(rows, BF16), logf, 3) + carry_ref[...]
    cum_ref[0] = cum
    carry_ref[...] = cum[rows - 1:rows, :]

    pg = jnp.dot(h, w_ref[:, OFF_GATE:OFF_GATE + N_GATE], preferred_element_type=F32)
    gate_ref[0] = _sigmoid(pg).astype(BF16)


def _projection(x, mod3, norm_g, w_proj, mu, f_bias):
    batch, seq, _ = x.shape
    tm = PROJ_ROWS
    grid = (batch, seq // tm)
    const = lambda b, s: (0, 0)
    tile = lambda b, s: (b, s, 0)
    return pl.pallas_call(
        _proj_kernel,
        grid=grid,
        in_specs=[pl.BlockSpec((1, tm, D_MODEL), tile),
                  pl.BlockSpec((1, N_MOD, D_MODEL), lambda b, s: (b, 0, 0)),
                  pl.BlockSpec((1, D_MODEL), const),
                  pl.BlockSpec((D_MODEL, N_PROJ), const, pipeline_mode=pl.Buffered(1)),
                  pl.BlockSpec((1, N_RKV + N_SMALL), const),
                  pl.BlockSpec((1, LANES), const)],
        out_specs=[pl.BlockSpec((1, tm, N_RKV), tile),
                   pl.BlockSpec((1, tm, N_SMALL), tile),
                   pl.BlockSpec((1, tm, 2 * WIDTH), tile),
                   pl.BlockSpec((1, WIDTH, tm), lambda b, s: (b, 0, s)),
                   pl.BlockSpec((1, tm, LANES), tile),
                   pl.BlockSpec((1, tm, N_GATE), tile)],
        out_shape=[jax.ShapeDtypeStruct((batch, seq, N_RKV), BF16),
                   jax.ShapeDtypeStruct((batch, seq, N_SMALL), F32),
                   jax.ShapeDtypeStruct((batch, seq, 2 * WIDTH), BF16),
                   jax.ShapeDtypeStruct((batch, WIDTH, seq), BF16),
                   jax.ShapeDtypeStruct((batch, seq, LANES), F32),
                   jax.ShapeDtypeStruct((batch, seq, N_GATE), BF16)],
        scratch_shapes=[pltpu.VMEM((tm + 8, N_RKV + N_SMALL), F32),
                        pltpu.VMEM((1, LANES), F32)],
        compiler_params=pltpu.CompilerParams(dimension_semantics=("arbitrary", "arbitrary"),
                                             vmem_limit_bytes=VMEM_LIMIT_BYTES),
        name="proj",
    )(x, mod3, norm_g, w_proj, mu, f_bias)


def _rwkv_kernel(rkv_ref, small_ref, w_lr_ref, w_gate_ref, vec_ref, seg_ref,
                 out_ref, z_ref):
    @pl.when(pl.program_id(1) == 0)
    def _():
        z_ref[...] = jnp.zeros_like(z_ref)

    c = CHUNK
    r = rkv_ref[0, :, 0:WIDTH].astype(F32)
    k_raw = rkv_ref[0, :, WIDTH:2 * WIDTH].astype(F32)
    v = rkv_ref[0, :, 2 * WIDTH:3 * WIDTH].astype(F32)
    small = small_ref[0]
    decay_base, iclr_base, kk_scale, k_mix, r_bonus, ln_w, ln_b = (
        vec_ref[i:i + 1, :] for i in range(7))

    lane = lax.broadcasted_iota(jnp.int32, (1, LANES), 1)
    lr_in = jnp.where(lane < DECAY_RANK, jnp.tanh(small[:, 0:LANES]), small[:, 0:LANES])
    lr = _dot(lr_in, w_lr_ref[...])
    w_log = -_softplus(-(decay_base + lr[:, 0:WIDTH])) - 0.5
    log_decay = -jnp.exp(w_log)
    a = _sigmoid(iclr_base + lr[:, WIDTH:2 * WIDTH])
    g = _dot(_sigmoid(small[:, LANES:2 * LANES]), w_gate_ref[...])

    seg = seg_ref[...]
    kk = k_raw * kk_scale
    kk = kk * lax.rsqrt(jnp.maximum(_head_sums(kk * kk, seg), 1e-24))
    k = k_raw * (1.0 + (a - 1.0) * k_mix)
    b_vec = kk * a

    cs = _cumsum_rows(_lower_tri(c, BF16), log_decay, 2)
    cs_end = cs[c - 1:c, :]
    w_in = jnp.exp(cs)
    w_inv = jnp.exp(-cs)
    w_end = jnp.exp(cs_end - cs)
    a_t = -kk * jnp.exp(cs - log_decay)
    r_t = r * w_in
    bt_T = jnp.transpose(b_vec * w_inv).astype(BF16)
    kt_T = jnp.transpose(k * w_inv).astype(BF16)
    bh_T = jnp.transpose(b_vec * w_end).astype(BF16)
    kh_T = jnp.transpose(k * w_end).astype(BF16)
    decay_end = jnp.exp(cs_end)

    row = lax.broadcasted_iota(jnp.int32, (c, c), 0)
    col = lax.broadcasted_iota(jnp.int32, (c, c), 1)
    strict = col < row
    incl = col <= row
    eye = (col == row).astype(F32)
    same_head = (row // HEAD_DIM) == (col // HEAD_DIM)

    pairs = range(N_PAIRS)
    sls = [slice(p * PAIR, (p + 1) * PAIR) for p in pairs]
    v_b = [v[:, sl].astype(BF16) for sl in sls]
    z_f = [z_ref[p] for p in pairs]
    z_b = [z.astype(BF16) for z in z_f]
    heads = [(p, h) for p in pairs for h in range(2)]
    a_m, r_m, a_ab, a_ak, m_rb, m_rk = {}, {}, {}, {}, {}, {}
    for (p, h) in heads:
        head = (lane // HEAD_DIM) == h
        a_m[p, h] = jnp.where(head, a_t[:, sls[p]], 0.0).astype(BF16)
        r_m[p, h] = jnp.where(head, r_t[:, sls[p]], 0.0).astype(BF16)
        rhs = jnp.concatenate([bt_T[sls[p], :], kt_T[sls[p], :]], axis=1)
        big = jnp.dot(jnp.concatenate([a_m[p, h], r_m[p, h]], axis=0), rhs,
                      preferred_element_type=F32)
        a_ab[p, h] = jnp.where(strict, big[0:c, 0:c], 0.0)
        a_ak[p, h] = jnp.where(strict, big[0:c, c:2 * c], 0.0).astype(BF16)
        m_rb[p, h] = jnp.where(incl, big[c:2 * c, 0:c], 0.0).astype(BF16)
        m_rk[p, h] = jnp.where(incl, big[c:2 * c, c:2 * c], 0.0).astype(BF16)
    t_inv = {hd: eye + a_ab[hd] for hd in heads}
    m_pow = {hd: _dot(a_ab[hd], a_ab[hd]) for hd in heads}
    for _ in range(5):
        for hd in heads:
            both = _dot(m_pow[hd], jnp.concatenate([m_pow[hd], t_inv[hd]], axis=1))
            m_pow[hd] = both[:, 0:c]
            t_inv[hd] = t_inv[hd] + both[:, c:2 * c]
    for hd in heads:
        t_inv[hd] = t_inv[hd] + _dot(m_pow[hd], t_inv[hd])
    ak_v = {(p, h): _dot(a_ak[p, h], v_b[p]) for (p, h) in heads}
    pq = {hd: _dot(t_inv[hd], jnp.concatenate([a_m[hd].astype(F32), ak_v[hd]], axis=1))
          for hd in heads}
    u = {(p, h): _dot(pq[p, h][:, 0:PAIR], z_b[p]) + pq[p, h][:, PAIR:2 * PAIR]
         for (p, h) in heads}
    y_hd = {(p, h): jnp.dot(
        jnp.concatenate([r_m[p, h], m_rb[p, h], m_rk[p, h]], axis=1),
        jnp.concatenate([z_b[p], u[p, h].astype(BF16), v_b[p]], axis=0),
        preferred_element_type=F32) for (p, h) in heads}
    first = lane < HEAD_DIM
    y_pairs = []
    for p in pairs:
        u_p = jnp.where(first, u[p, 0], u[p, 1])
        y_pairs.append(jnp.where(first, y_hd[p, 0], y_hd[p, 1]))
        d_col = jnp.sum(eye * decay_end[:, sls[p]], axis=1, keepdims=True)
        z_new = d_col * z_f[p] + jnp.dot(
            jnp.concatenate([bh_T[sls[p], :], kh_T[sls[p], :]], axis=1),
            jnp.concatenate([u_p.astype(BF16), v_b[p]], axis=0), preferred_element_type=F32)
        z_ref[p] = jnp.where(same_head, z_new, 0.0)

    y = jnp.concatenate(y_pairs, axis=1)
    mean = _head_sums(y, seg) * (1.0 / HEAD_DIM)
    yc = y - mean
    var = _head_sums(yc * yc, seg) * (1.0 / HEAD_DIM)
    y = yc * lax.rsqrt(var + GN_EPS) * ln_w + ln_b
    y = y + _head_sums(r * k * r_bonus, seg) * v
    out_ref[0] = (y * g).astype(BF16)


def _rwkv_mix(rkv, small, w_lr, w_gate, vecs, seg):
    batch, seq, _ = rkv.shape
    grid = (batch, seq // CHUNK)
    const = lambda b, s: (0, 0)
    tile = lambda b, s: (b, s, 0)
    return pl.pallas_call(
        _rwkv_kernel,
        grid=grid,
        in_specs=[pl.BlockSpec((1, CHUNK, N_RKV), tile),
                  pl.BlockSpec((1, CHUNK, N_SMALL), tile),
                  pl.BlockSpec(w_lr.shape, const),
                  pl.BlockSpec(w_gate.shape, const),
                  pl.BlockSpec(vecs.shape, const),
                  pl.BlockSpec(seg.shape, const)],
        out_specs=pl.BlockSpec((1, CHUNK, WIDTH), tile),
        out_shape=jax.ShapeDtypeStruct((batch, seq, WIDTH), BF16),
        scratch_shapes=[pltpu.VMEM((N_PAIRS, PAIR, PAIR), F32)],
        compiler_params=pltpu.CompilerParams(dimension_semantics=("arbitrary", "arbitrary"),
                                             vmem_limit_bytes=VMEM_LIMIT_BYTES),
        name="rwkv",
    )(rkv, small, w_lr, w_gate, vecs, seg)


def _fox_kernel(q_ref, k_ref, vt_ref, cum_ref, cumt_ref, o_ref, kx_ref, vx_ref, t_ref, p_ref):
    hp = pl.program_id(1)
    qi = pl.program_id(2)
    blk = FOX_BLOCK
    seq = k_ref.shape[1]
    lane = lax.broadcasted_iota(jnp.int32, (1, LANES), 1)
    own = [(lane // HEAD_DIM) == h for h in range(2)]
    bias_lane = [HEAD_DIM * (1 - h) for h in range(2)]

    @pl.when(qi == 0)
    def _():
        def fill(i, _):
            rows = pl.ds(pl.multiple_of(i * blk, blk), blk)
            cum_rows = cum_ref[0, rows, :]
            k_rows = k_ref[0, rows, :]
            for h in range(2):
                col = jnp.sum(jnp.where(lane == 2 * hp + h, cum_rows, 0.0), axis=1, keepdims=True)
                parts = _bf16_parts(jnp.broadcast_to(col * (-LOG2E), (blk, LANES)), 3)
                feat = jnp.zeros((blk, LANES), F32)
                for n, part in enumerate(parts):
                    feat = jnp.where(lane == bias_lane[h] + n, part.astype(F32), feat)
                kx_ref[h, rows, :] = jnp.where(own[h], k_rows.astype(F32), feat).astype(BF16)
            return 0
        lax.fori_loop(0, seq // blk, fill, 0)
        ones_row = lax.broadcasted_iota(jnp.int32, (FOX_VROWS - HEAD_DIM, seq), 0) == 0
        for h in range(2):
            vx_ref[h, 0:HEAD_DIM, :] = vt_ref[0, h * HEAD_DIM:(h + 1) * HEAD_DIM, :]
            vx_ref[h, HEAD_DIM:FOX_VROWS, :] = ones_row.astype(BF16)

    q = q_ref[0]
    q_start = pl.multiple_of(qi * blk, blk)
    q_x, c_q = [], []
    for h in range(2):
        is_bias = (lane >= bias_lane[h]) & (lane < bias_lane[h] + 3)
        q_x.append(jnp.where(own[h], q.astype(F32), is_bias.astype(F32)).astype(BF16))
        c_q.append(cumt_ref[0, pl.ds(2 * hp + h, 1), pl.ds(q_start, blk)] * LOG2E)

    def scores(j, slot):
        start = pl.multiple_of(j * blk, blk)
        for h in range(2):
            t_ref[slot, h] = _dot_nt(kx_ref[h, pl.ds(start, blk), :], q_x[h])

    def update(j, slot, carry, masked):
        start = pl.multiple_of(j * blk, blk)
        sub = FOX_SUB
        out = []
        for h in range(2):
            m, acc = carry[h]

            def load(i):
                ts = t_ref[slot, h, i * sub:(i + 1) * sub, :]
                if masked:
                    key = lax.broadcasted_iota(jnp.int32, (sub, blk), 0) + i * sub
                    qry = lax.broadcasted_iota(jnp.int32, (sub, blk), 1)
                    ts = jnp.where(key <= qry, ts, -jnp.inf)
                return ts

            mx = jnp.max(load(0).reshape(sub // 8, 8, blk), axis=0)
            for i in range(1, blk // sub):
                mx = jnp.maximum(mx, jnp.max(load(i).reshape(sub // 8, 8, blk), axis=0))
            m_new = jnp.maximum(m, jnp.max(mx, axis=0, keepdims=True) + c_q[h])
            alpha = jnp.exp2(m - m_new)
            shift = c_q[h] - m_new
            for i in range(blk // sub):
                p_ref[h, i * sub:(i + 1) * sub, :] = jnp.exp2(load(i) + shift).astype(BF16)
            acc = alpha * acc + jnp.dot(vx_ref[h, :, pl.ds(start, blk)], p_ref[h],
                                        preferred_element_type=F32)
            out.append((m_new, acc))
        return tuple(out)

    def two_blocks(i, carry):
        scores(2 * i + 1, 1)
        carry = update(2 * i, 0, carry, False)
        scores(2 * i + 2, 0)
        return update(2 * i + 1, 1, carry, False)

    def tail_odd(carry):
        scores(qi, 1)
        carry = update(qi - 1, 0, carry, False)
        return update(qi, 1, carry, True)

    init = tuple((jnp.full((1, blk), -jnp.inf, F32), jnp.zeros((FOX_VROWS, blk), F32))
                 for _ in range(2))
    scores(0, 0)
    carry = lax.fori_loop(0, qi // 2, two_blocks, init)
    carry = lax.cond(qi % 2 == 1, tail_odd, lambda cr: update(qi, 0, cr, True), carry)
    o_t = jnp.concatenate([acc[0:HEAD_DIM] / acc[HEAD_DIM:HEAD_DIM + 1] for (_, acc) in carry],
                          axis=0)
    o_ref[0] = jnp.transpose(o_t).astype(BF16)


def _forgetting_attention(qk, v_t, cum, cum_t):
    batch, seq, _ = qk.shape
    blk = FOX_BLOCK
    grid = (batch, N_PAIRS, seq // blk)
    return pl.pallas_call(
        _fox_kernel,
        grid=grid,
        in_specs=[pl.BlockSpec((1, blk, PAIR), lambda b, p, i: (b, i, p)),
                  pl.BlockSpec((1, seq, PAIR), lambda b, p, i: (b, 0, N_PAIRS + p)),
                  pl.BlockSpec((1, PAIR, seq), lambda b, p, i: (b, p, 0)),
                  pl.BlockSpec((1, seq, LANES), lambda b, p, i: (b, 0, 0)),
                  pl.BlockSpec((1, HEADS, seq), lambda b, p, i: (b, 0, 0))],
        out_specs=pl.BlockSpec((1, blk, PAIR), lambda b, p, i: (b, i, p)),
        out_shape=jax.ShapeDtypeStruct((batch, seq, WIDTH), BF16),
        scratch_shapes=[pltpu.VMEM((2, seq, LANES), BF16),
                        pltpu.VMEM((2, FOX_VROWS, seq), BF16),
                        pltpu.VMEM((2, 2, blk, blk), F32),
                        pltpu.VMEM((2, blk, blk), BF16)],
        compiler_params=pltpu.CompilerParams(
            dimension_semantics=("arbitrary", "arbitrary", "arbitrary"),
            vmem_limit_bytes=VMEM_LIMIT_BYTES),
        name="fox",
    )(qk, qk, v_t, cum, cum_t)


def _tail_kernel(x_ref, ya_ref, yb_ref, gate_ref, mod_ref, g2_ref, gf_ref,
                 woa_ref, wob_ref, wout_ref, w1_ref, w2_ref, o_ref):
    x = x_ref[0]
    gate1 = mod_ref[0, 2:3, :]
    shift2 = mod_ref[0, 3:4, :]
    scale2 = mod_ref[0, 4:5, :]
    gate2 = mod_ref[0, 5:6, :]

    merged = (gate_ref[0, :, 0:D_MODEL].astype(F32)
              * jnp.dot(ya_ref[0], woa_ref[...], preferred_element_type=F32)
              + gate_ref[0, :, D_MODEL:N_GATE].astype(F32)
              * jnp.dot(yb_ref[0], wob_ref[...], preferred_element_type=F32))
    x = x + gate1 * _dot(merged, wout_ref[...])

    inv = lax.rsqrt(jnp.mean(x * x, axis=-1, keepdims=True) + NORM_EPS)
    h2 = ((x * inv) * g2_ref[...] * (1.0 + scale2) + shift2).astype(BF16)
    ff = jnp.zeros_like(x)
    for j in range(D_FF // FF_CHUNK):
        cols = slice(j * FF_CHUNK, (j + 1) * FF_CHUNK)
        hid = jnp.maximum(jnp.dot(h2, w1_ref[:, cols], preferred_element_type=F32), 0.0)
        ff = ff + _dot(hid * hid, w2_ref[cols, :])
    x = x + gate2 * ff

    inv = lax.rsqrt(jnp.mean(x * x, axis=-1, keepdims=True) + NORM_EPS)
    o_ref[0] = (x * inv) * gf_ref[...]


def _tail(x, y_a, y_b, gates, mod3, norm2_g, final_g, w_oa, w_ob, w_out, w_ff1, w_ff2):
    batch, seq, _ = x.shape
    tm = TAIL_ROWS
    grid = (batch, seq // tm)
    const = lambda b, s: (0, 0)
    tile = lambda b, s: (b, s, 0)
    resident = lambda a: pl.BlockSpec(a.shape, const, pipeline_mode=pl.Buffered(1))
    return pl.pallas_call(
        _tail_kernel,
        grid=grid,
        in_specs=[pl.BlockSpec((1, tm, D_MODEL), tile),
                  pl.BlockSpec((1, tm, WIDTH), tile),
                  pl.BlockSpec((1, tm, WIDTH), tile),
                  pl.BlockSpec((1, tm, N_GATE), tile),
                  pl.BlockSpec((1, N_MOD, D_MODEL), lambda b, s: (b, 0, 0)),
                  pl.BlockSpec((1, D_MODEL), const),
                  pl.BlockSpec((1, D_MODEL), const),
                  resident(w_oa), resident(w_ob), resident(w_out),
                  resident(w_ff1), resident(w_ff2)],
        out_specs=pl.BlockSpec((1, tm, D_MODEL), tile),
        out_shape=jax.ShapeDtypeStruct((batch, seq, D_MODEL), F32),
        compiler_params=pltpu.CompilerParams(dimension_semantics=("arbitrary", "arbitrary"),
                                             vmem_limit_bytes=VMEM_LIMIT_BYTES),
        name="tail",
    )(x, y_a, y_b, gates, mod3, norm2_g, final_g, w_oa, w_ob, w_out, w_ff1, w_ff2)


def _reorder_rwkv_cols(t):
    o = 0
    r = t[..., o:o + WIDTH]; o += WIDTH
    wd = t[..., o:o + DECAY_RANK]; o += DECAY_RANK
    k = t[..., o:o + WIDTH]; o += WIDTH
    v = t[..., o:o + WIDTH]; o += WIDTH
    ad = t[..., o:o + ICLR_RANK]; o += ICLR_RANK
    gd = t[..., o:o + GATE_RANK]
    return jnp.concatenate([r, k, v, wd, ad, gd], axis=-1)


def kernel(x, c, w_ada, b_ada, norm1_g, w_in, mu_shift, w_decay_up, decay_base, w_iclr_up, iclr_base, w_gate_up, kk_scale, k_iclr_mix, r_bonus, lnx_w, lnx_b, fox_f_bias, w_o_rwkv, w_o_fox, w_out, norm2_g, w_ff1, w_ff2, final_g):
    assert w_ada.shape[0] == 1, "the tail kernel fuses the final norm: single layer only"
    l = 0
    n_rwkv = N_RKV + N_SMALL
    seg_id = jnp.arange(WIDTH // 2) // HEAD_DIM
    seg = (seg_id[:, None] == seg_id[None, :]).astype(BF16)

    mod3 = _modulation(c, w_ada[l], b_ada[l]).reshape(-1, N_MOD, D_MODEL)

    w = w_in[l]
    w_proj = jnp.concatenate(
        [_reorder_rwkv_cols(w[:, :n_rwkv]),
         w[:, n_rwkv:n_rwkv + N_FOX],
         jnp.pad(w[:, n_rwkv + N_FOX:n_rwkv + N_FOX + HEADS], ((0, 0), (0, LANES - HEADS))),
         w[:, n_rwkv + N_FOX + HEADS:]], axis=1).astype(BF16)
    mu = _reorder_rwkv_cols(mu_shift[l]).reshape(1, n_rwkv)
    f_bias = jnp.pad(fox_f_bias[l], (0, LANES - HEADS)).reshape(1, LANES)
    rkv, small, qk, v_t, cum, gates = _projection(
        x, mod3, norm1_g[l].reshape(1, D_MODEL), w_proj, mu, f_bias)

    zeros = jnp.zeros((DECAY_RANK, WIDTH), F32)
    w_lr = jnp.concatenate(
        [jnp.concatenate([w_decay_up[l], zeros], axis=1),
         jnp.concatenate([zeros, w_iclr_up[l]], axis=1)], axis=0).astype(BF16)
    vecs = jnp.stack([decay_base[l], iclr_base[l], kk_scale[l], k_iclr_mix[l],
                      r_bonus[l].reshape(WIDTH), lnx_w[l], lnx_b[l],
                      jnp.zeros((WIDTH,), F32)], axis=0)
    y_a = _rwkv_mix(rkv, small, w_lr, w_gate_up[l].astype(BF16), vecs, seg)

    cum_t = jnp.transpose(cum[:, :, :HEADS], (0, 2, 1))
    y_b = _forgetting_attention(qk, v_t, cum, cum_t)

    return _tail(x, y_a, y_b, gates, mod3, norm2_g[l].reshape(1, D_MODEL),
                 final_g.reshape(1, D_MODEL),
                 w_o_rwkv[l].astype(BF16), w_o_fox[l].astype(BF16), w_out[l].astype(BF16),
                 w_ff1[l].astype(BF16), w_ff2[l].astype(BF16))
```

```python
import jax
import jax.numpy as jnp
from jax import lax
from jax.experimental import pallas as pl
from jax.experimental.pallas import tpu as pltpu

F32 = jnp.float32
BF16 = jnp.bfloat16

D_MODEL = 1024
HEAD_DIM = 64
HEADS = 8
WIDTH = HEADS * HEAD_DIM
DECAY_RANK = 64
ICLR_RANK = 64
GATE_RANK = 128
D_FF = 4 * D_MODEL
N_MOD = 6
NORM_EPS = 1e-6
GN_EPS = 64e-5

LANES = 128
PAIR = 2 * HEAD_DIM
N_PAIRS = HEADS // 2

LOG2E = 1.4426950408889634

N_SMALL = DECAY_RANK + ICLR_RANK + GATE_RANK
N_RKV = 3 * WIDTH
N_FOX = 3 * WIDTH
N_GATE = 2 * D_MODEL
OFF_RKV = 0
OFF_SMALL = OFF_RKV + N_RKV
OFF_FOX = OFF_SMALL + N_SMALL
OFF_FF = OFF_FOX + N_FOX
OFF_GATE = OFF_FF + LANES
N_PROJ = OFF_GATE + N_GATE

PROJ_ROWS = 256
CHUNK = 128
FOX_BLOCK = 512
FOX_SUB = 64
FOX_VROWS = HEAD_DIM + 16
FOX_MASKED = -1e30
TAIL_ROWS = 512
FF_CHUNK = 1024

VMEM_LIMIT_BYTES = 56 * 1024 * 1024


def _dot(a, b):
    return jnp.dot(a.astype(BF16), b.astype(BF16), preferred_element_type=F32)


def _dot_nt(a, b):
    return lax.dot_general(a.astype(BF16), b.astype(BF16), (((1,), (1,)), ((), ())),
                           preferred_element_type=F32)


def _softplus(z):
    return jnp.maximum(z, 0.0) + jnp.log(1.0 + jnp.exp(-jnp.abs(z)))


def _sigmoid(z):
    return 1.0 / (1.0 + jnp.exp(-z))


def _bf16_parts(x, n):
    parts = []
    for _ in range(n):
        p = x.astype(BF16)
        parts.append(p)
        x = x - p.astype(F32)
    return parts


def _cumsum_rows(tri, x, n_parts):
    return sum(jnp.dot(tri, p, preferred_element_type=F32) for p in _bf16_parts(x, n_parts))


def _head_sums(x, seg):
    half = seg.shape[0]
    return jnp.concatenate([_dot(x[:, 0:half], seg), _dot(x[:, half:2 * half], seg)], axis=1)


def _lower_tri(n, dtype):
    row = lax.broadcasted_iota(jnp.int32, (n, n), 0)
    col = lax.broadcasted_iota(jnp.int32, (n, n), 1)
    return (col <= row).astype(dtype)


def _mod_kernel(c_ref, w_ref, b_ref, o_ref):
    c = c_ref[...]
    c_act = c * _sigmoid(c)
    o_ref[...] = jnp.dot(c_act, w_ref[...], preferred_element_type=F32,
                         precision=lax.Precision.HIGHEST) + b_ref[...]


def _modulation(c, w_ada, b_ada):
    batch = c.shape[0]
    n = w_ada.shape[1]
    tn = 1536
    return pl.pallas_call(
        _mod_kernel,
        grid=(n // tn,),
        in_specs=[pl.BlockSpec((batch, D_MODEL), lambda j: (0, 0)),
                  pl.BlockSpec((D_MODEL, tn), lambda j: (0, j)),
                  pl.BlockSpec((1, tn), lambda j: (0, j))],
        out_specs=pl.BlockSpec((batch, tn), lambda j: (0, j)),
        out_shape=jax.ShapeDtypeStruct((batch, n), F32),
        compiler_params=pltpu.CompilerParams(dimension_semantics=("arbitrary",),
                                             vmem_limit_bytes=VMEM_LIMIT_BYTES),
        name="mod",
    )(c, w_ada, b_ada.reshape(1, n))


def _proj_kernel(x_ref, mod_ref, g_ref, w_ref, mu_ref, fb_ref,
                 rkv_ref, small_ref, qk_ref, vt_ref, cum_ref, gate_ref,
                 shift_ref, carry_ref):
    s = pl.program_id(1)
    rows = x_ref.shape[1]

    @pl.when(s == 0)
    def _():
        shift_ref[0:8, :] = jnp.zeros((8, N_RKV + N_SMALL), F32)
        carry_ref[...] = jnp.zeros_like(carry_ref)

    x = x_ref[0]
    shift1 = mod_ref[0, 0:1, :]
    scale1 = mod_ref[0, 1:2, :]
    inv = lax.rsqrt(jnp.mean(x * x, axis=-1, keepdims=True) + NORM_EPS)
    h = ((x * inv) * g_ref[...] * (1.0 + scale1) + shift1).astype(BF16)

    n_tok = N_RKV + N_SMALL
    p = jnp.dot(h, w_ref[:, OFF_RKV:OFF_RKV + n_tok], preferred_element_type=F32)
    shift_ref[8:8 + rows, :] = p
    prev = shift_ref[7:7 + rows, :]
    shift_ref[7:8, :] = p[rows - 1:rows, :]
    mixed = p + mu_ref[...] * (prev - p)
    rkv_ref[0] = mixed[:, 0:N_RKV].astype(BF16)
    small_ref[0] = mixed[:, N_RKV:n_tok]

    pf = jnp.dot(h, w_ref[:, OFF_FOX:OFF_FOX + N_FOX], preferred_element_type=F32)
    qk_ref[0, :, 0:WIDTH] = (pf[:, 0:WIDTH] * (LOG2E * HEAD_DIM ** -0.5)).astype(BF16)
    qk_ref[0, :, WIDTH:2 * WIDTH] = pf[:, WIDTH:2 * WIDTH].astype(BF16)
    vt_ref[0] = jnp.transpose(pf[:, 2 * WIDTH:N_FOX]).astype(BF16)

    ff = jnp.dot(h, w_ref[:, OFF_FF:OFF_FF + LANES], preferred_element_type=F32)
    logf = -_softplus(-(ff + fb_ref[...]))
    cum = _cumsum_rows(_lower_tri(rows, BF16), logf, 3) + carry_ref[...]
    cum_ref[0] = cum
    carry_ref[...] = cum[rows - 1:rows, :]

    pg = jnp.dot(h, w_ref[:, OFF_GATE:OFF_GATE + N_GATE], preferred_element_type=F32)
    gate_ref[0] = _sigmoid(pg).astype(BF16)


def _projection(x, mod3, norm_g, w_proj, mu, f_bias):
    batch, seq, _ = x.shape
    tm = PROJ_ROWS
    grid = (batch, seq // tm)
    const = lambda b, s: (0, 0)
    tile = lambda b, s: (b, s, 0)
    return pl.pallas_call(
        _proj_kernel,
        grid=grid,
        in_specs=[pl.BlockSpec((1, tm, D_MODEL), tile),
                  pl.BlockSpec((1, N_MOD, D_MODEL), lambda b, s: (b, 0, 0)),
                  pl.BlockSpec((1, D_MODEL), const),
                  pl.BlockSpec((D_MODEL, N_PROJ), const, pipeline_mode=pl.Buffered(1)),
                  pl.BlockSpec((1, N_RKV + N_SMALL), const),
                  pl.BlockSpec((1, LANES), const)],
        out_specs=[pl.BlockSpec((1, tm, N_RKV), tile),
                   pl.BlockSpec((1, tm, N_SMALL), tile),
                   pl.BlockSpec((1, tm, 2 * WIDTH), tile),
                   pl.BlockSpec((1, WIDTH, tm), lambda b, s: (b, 0, s)),
                   pl.BlockSpec((1, tm, LANES), tile),
                   pl.BlockSpec((1, tm, N_GATE), tile)],
        out_shape=[jax.ShapeDtypeStruct((batch, seq, N_RKV), BF16),
                   jax.ShapeDtypeStruct((batch, seq, N_SMALL), F32),
                   jax.ShapeDtypeStruct((batch, seq, 2 * WIDTH), BF16),
                   jax.ShapeDtypeStruct((batch, WIDTH, seq), BF16),
                   jax.ShapeDtypeStruct((batch, seq, LANES), F32),
                   jax.ShapeDtypeStruct((batch, seq, N_GATE), BF16)],
        scratch_shapes=[pltpu.VMEM((tm + 8, N_RKV + N_SMALL), F32),
                        pltpu.VMEM((1, LANES), F32)],
        compiler_params=pltpu.CompilerParams(dimension_semantics=("arbitrary", "arbitrary"),
                                             vmem_limit_bytes=VMEM_LIMIT_BYTES),
        name="proj",
    )(x, mod3, norm_g, w_proj, mu, f_bias)


def _rwkv_kernel(rkv_ref, small_ref, w_lr_ref, w_gate_ref, vec_ref, seg_ref,
                 out_ref, z_ref):
    @pl.when(pl.program_id(1) == 0)
    def _():
        z_ref[...] = jnp.zeros_like(z_ref)

    c = CHUNK
    r = rkv_ref[0, :, 0:WIDTH].astype(F32)
    k_raw = rkv_ref[0, :, WIDTH:2 * WIDTH].astype(F32)
    v = rkv_ref[0, :, 2 * WIDTH:3 * WIDTH].astype(F32)
    small = small_ref[0]
    decay_base, iclr_base, kk_scale, k_mix, r_bonus, ln_w, ln_b = (
        vec_ref[i:i + 1, :] for i in range(7))

    lane = lax.broadcasted_iota(jnp.int32, (1, LANES), 1)
    lr_in = jnp.where(lane < DECAY_RANK, jnp.tanh(small[:, 0:LANES]), small[:, 0:LANES])
    lr = _dot(lr_in, w_lr_ref[...])
    w_log = -_softplus(-(decay_base + lr[:, 0:WIDTH])) - 0.5
    log_decay = -jnp.exp(w_log)
    a = _sigmoid(iclr_base + lr[:, WIDTH:2 * WIDTH])
    g = _dot(_sigmoid(small[:, LANES:2 * LANES]), w_gate_ref[...])

    seg = seg_ref[...]
    kk = k_raw * kk_scale
    kk = kk * lax.rsqrt(jnp.maximum(_head_sums(kk * kk, seg), 1e-24))
    k = k_raw * (1.0 + (a - 1.0) * k_mix)
    b_vec = kk * a

    cs = _cumsum_rows(_lower_tri(c, BF16), log_decay, 2)
    cs_end = cs[c - 1:c, :]
    w_in = jnp.exp(cs)
    w_inv = jnp.exp(-cs)
    w_end = jnp.exp(cs_end - cs)
    a_t = -kk * jnp.exp(cs - log_decay)
    r_t = r * w_in
    bt_T = jnp.transpose(b_vec * w_inv).astype(BF16)
    kt_T = jnp.transpose(k * w_inv).astype(BF16)
    bh_T = jnp.transpose(b_vec * w_end).astype(BF16)
    kh_T = jnp.transpose(k * w_end).astype(BF16)
    decay_end = jnp.exp(cs_end)

    row = lax.broadcasted_iota(jnp.int32, (c, c), 0)
    col = lax.broadcasted_iota(jnp.int32, (c, c), 1)
    strict = col < row
    incl = col <= row
    eye = (col == row).astype(F32)
    same_head = (row // HEAD_DIM) == (col // HEAD_DIM)

    pairs = range(N_PAIRS)
    sls = [slice(p * PAIR, (p + 1) * PAIR) for p in pairs]
    v_b = [v[:, sl].astype(BF16) for sl in sls]
    z_f = [z_ref[p] for p in pairs]
    z_b = [z.astype(BF16) for z in z_f]
    heads = [(p, h) for p in pairs for h in range(2)]
    a_m, r_m, a_ab, a_ak, m_rb, m_rk = {}, {}, {}, {}, {}, {}
    for (p, h) in heads:
        head = (lane // HEAD_DIM) == h
        a_m[p, h] = jnp.where(head, a_t[:, sls[p]], 0.0).astype(BF16)
        r_m[p, h] = jnp.where(head, r_t[:, sls[p]], 0.0).astype(BF16)
        rhs = jnp.concatenate([bt_T[sls[p], :], kt_T[sls[p], :]], axis=1)
        big = jnp.dot(jnp.concatenate([a_m[p, h], r_m[p, h]], axis=0), rhs,
                      preferred_element_type=F32)
        a_ab[p, h] = jnp.where(strict, big[0:c, 0:c], 0.0)
        a_ak[p, h] = jnp.where(strict, big[0:c, c:2 * c], 0.0).astype(BF16)
        m_rb[p, h] = jnp.where(incl, big[c:2 * c, 0:c], 0.0).astype(BF16)
        m_rk[p, h] = jnp.where(incl, big[c:2 * c, c:2 * c], 0.0).astype(BF16)
    t_inv = {hd: eye + a_ab[hd] for hd in heads}
    m_pow = {hd: _dot(a_ab[hd], a_ab[hd]) for hd in heads}
    for _ in range(5):
        for hd in heads:
            both = _dot(m_pow[hd], jnp.concatenate([m_pow[hd], t_inv[hd]], axis=1))
            m_pow[hd] = both[:, 0:c]
            t_inv[hd] = t_inv[hd] + both[:, c:2 * c]
    for hd in heads:
        t_inv[hd] = t_inv[hd] + _dot(m_pow[hd], t_inv[hd])
    ak_v = {(p, h): _dot(a_ak[p, h], v_b[p]) for (p, h) in heads}
    pq = {hd: _dot(t_inv[hd], jnp.concatenate([a_m[hd].astype(F32), ak_v[hd]], axis=1))
          for hd in heads}
    u = {(p, h): _dot(pq[p, h][:, 0:PAIR], z_b[p]) + pq[p, h][:, PAIR:2 * PAIR]
         for (p, h) in heads}
    y_hd = {(p, h): jnp.dot(
        jnp.concatenate([r_m[p, h], m_rb[p, h], m_rk[p, h]], axis=1),
        jnp.concatenate([z_b[p], u[p, h].astype(BF16), v_b[p]], axis=0),
        preferred_element_type=F32) for (p, h) in heads}
    first = lane < HEAD_DIM
    y_pairs = []
    for p in pairs:
        u_p = jnp.where(first, u[p, 0], u[p, 1])
        y_pairs.append(jnp.where(first, y_hd[p, 0], y_hd[p, 1]))
        d_col = jnp.sum(eye * decay_end[:, sls[p]], axis=1, keepdims=True)
        z_new = d_col * z_f[p] + jnp.dot(
            jnp.concatenate([bh_T[sls[p], :], kh_T[sls[p], :]], axis=1),
            jnp.concatenate([u_p.astype(BF16), v_b[p]], axis=0), preferred_element_type=F32)
        z_ref[p] = jnp.where(same_head, z_new, 0.0)

    y = jnp.concatenate(y_pairs, axis=1)
    mean = _head_sums(y, seg) * (1.0 / HEAD_DIM)
    yc = y - mean
    var = _head_sums(yc * yc, seg) * (1.0 / HEAD_DIM)
    y = yc * lax.rsqrt(var + GN_EPS) * ln_w + ln_b
    y = y + _head_sums(r * k * r_bonus, seg) * v
    out_ref[0] = (y * g).astype(BF16)


def _rwkv_mix(rkv, small, w_lr, w_gate, vecs, seg):
    batch, seq, _ = rkv.shape
    grid = (batch, seq // CHUNK)
    const = lambda b, s: (0, 0)
    tile = lambda b, s: (b, s, 0)
    return pl.pallas_call(
        _rwkv_kernel,
        grid=grid,
        in_specs=[pl.BlockSpec((1, CHUNK, N_RKV), tile),
                  pl.BlockSpec((1, CHUNK, N_SMALL), tile),
                  pl.BlockSpec(w_lr.shape, const),
                  pl.BlockSpec(w_gate.shape, const),
                  pl.BlockSpec(vecs.shape, const),
                  pl.BlockSpec(seg.shape, const)],
        out_specs=pl.BlockSpec((1, CHUNK, WIDTH), tile),
        out_shape=jax.ShapeDtypeStruct((batch, seq, WIDTH), BF16),
        scratch_shapes=[pltpu.VMEM((N_PAIRS, PAIR, PAIR), F32)],
        compiler_params=pltpu.CompilerParams(dimension_semantics=("arbitrary", "arbitrary"),
                                             vmem_limit_bytes=VMEM_LIMIT_BYTES),
        name="rwkv",
    )(rkv, small, w_lr, w_gate, vecs, seg)


def _fox_kernel(q_ref, k_ref, vt_ref, cum_ref, cumt_ref, mask_ref, o_ref,
                qx_ref, kx_ref, vx_ref, t_ref, p_ref):
    hp = pl.program_id(1)
    blk = FOX_BLOCK
    seq = k_ref.shape[1]
    n_blk = seq // blk
    lane = lax.broadcasted_iota(jnp.int32, (1, LANES), 1)
    own = [(lane // HEAD_DIM) == h for h in range(2)]
    bias_lane = [HEAD_DIM * (1 - h) for h in range(2)]

    def fill(i, _):
        rows = pl.ds(pl.multiple_of(i * blk, blk), blk)
        cum_rows = cum_ref[0, rows, :]
        k_rows = k_ref[0, rows, :].astype(F32)
        q_rows = q_ref[0, rows, :].astype(F32)
        for h in range(2):
            col = jnp.sum(jnp.where(lane == 2 * hp + h, cum_rows, 0.0), axis=1, keepdims=True)
            parts = _bf16_parts(jnp.broadcast_to(col * (-LOG2E), (blk, LANES)), 3)
            feat = jnp.zeros((blk, LANES), F32)
            for n, part in enumerate(parts):
                feat = jnp.where(lane == bias_lane[h] + n, part.astype(F32), feat)
            kx_ref[h, rows, :] = jnp.where(own[h], k_rows, feat).astype(BF16)
            is_bias = (lane >= bias_lane[h]) & (lane < bias_lane[h] + 3)
            qx_ref[h, rows, :] = jnp.where(own[h], q_rows, is_bias.astype(F32)).astype(BF16)
        return 0
    lax.fori_loop(0, n_blk, fill, 0)
    ones_row = lax.broadcasted_iota(jnp.int32, (FOX_VROWS - HEAD_DIM, seq), 0) == 0
    for h in range(2):
        vx_ref[h, 0:HEAD_DIM, :] = vt_ref[0, h * HEAD_DIM:(h + 1) * HEAD_DIM, :]
        vx_ref[h, HEAD_DIM:FOX_VROWS, :] = ones_row.astype(BF16)

    def rows_of(b):
        return pl.ds(pl.multiple_of(b * blk, blk), blk)

    def following(qi, kj):
        wrap = kj == qi
        return jnp.where(wrap, qi + 1, qi), jnp.where(wrap, 0, kj + 1)

    def scores(pair, slot):
        qi, kj = pair
        qi = jnp.minimum(qi, n_blk - 1)
        causal = mask_ref[(kj == qi).astype(jnp.int32)]
        for h in range(2):
            t_ref[slot, h] = _dot_nt(kx_ref[h, rows_of(kj), :], qx_ref[h, rows_of(qi), :]) + causal

    def softmax(pair, slot, m):
        qi, kj = pair
        sub = FOX_SUB
        q_rows = rows_of(jnp.minimum(qi, n_blk - 1))
        m_out, alpha_out = [], []
        for h in range(2):
            m_old = jnp.where(kj == 0, -jnp.inf, m[h])
            c_q = cumt_ref[0, pl.ds(2 * hp + h, 1), q_rows] * LOG2E
            load = lambda i: t_ref[slot, h, i * sub:(i + 1) * sub, :]
            mx = jnp.max(load(0).reshape(sub // 8, 8, blk), axis=0)
            for i in range(1, blk // sub):
                mx = jnp.maximum(mx, jnp.max(load(i).reshape(sub // 8, 8, blk), axis=0))
            m_new = jnp.maximum(m_old, jnp.max(mx, axis=0, keepdims=True) + c_q)
            shift = c_q - m_new
            for i in range(blk // sub):
                p_ref[slot, h, i * sub:(i + 1) * sub, :] = jnp.exp2(load(i) + shift).astype(BF16)
            m_out.append(m_new)
            alpha_out.append(jnp.exp2(m_old - m_new))
        return tuple(m_out), tuple(alpha_out)

    def accumulate(pair, slot, alpha, acc):
        qi, kj = pair
        acc = tuple(alpha[h] * acc[h] + jnp.dot(vx_ref[h, :, rows_of(kj)], p_ref[slot, h],
                                                preferred_element_type=F32) for h in range(2))
        o_t = jnp.concatenate([a[0:HEAD_DIM] / a[HEAD_DIM:HEAD_DIM + 1] for a in acc], axis=0)
        o_ref[0, :, rows_of(qi)] = o_t.astype(BF16)
        return acc

    def two_pairs(_, carry):
        pair0, pair1, m, alpha0, acc = carry
        pair2 = following(*pair1)
        scores(pair2, 0)
        m, alpha1 = softmax(pair1, 1, m)
        acc = accumulate(pair0, 0, alpha0, acc)
        pair3 = following(*pair2)
        scores(pair3, 1)
        m, alpha2 = softmax(pair2, 0, m)
        acc = accumulate(pair1, 1, alpha1, acc)
        return pair2, pair3, m, alpha2, acc

    n_pairs = n_blk * (n_blk + 1) // 2
    assert n_pairs % 2 == 0, "two pairs per trip"
    zero = jnp.int32(0)
    pair0 = (zero, zero)
    pair1 = following(*pair0)
    scores(pair0, 0)
    scores(pair1, 1)
    m = tuple(jnp.full((1, blk), -jnp.inf, F32) for _ in range(2))
    m, alpha0 = softmax(pair0, 0, m)
    acc = tuple(jnp.zeros((FOX_VROWS, blk), F32) for _ in range(2))
    lax.fori_loop(0, n_pairs // 2, two_pairs, (pair0, pair1, m, alpha0, acc))


def _forgetting_attention(qk, v_t, cum, cum_t):
    batch, seq, _ = qk.shape
    blk = FOX_BLOCK
    key = lax.broadcasted_iota(jnp.int32, (blk, blk), 0)
    qry = lax.broadcasted_iota(jnp.int32, (blk, blk), 1)
    mask = jnp.stack([jnp.zeros((blk, blk), F32), jnp.where(key <= qry, 0.0, FOX_MASKED)])
    whole = lambda b, p: (b, 0, 0)
    return pl.pallas_call(
        _fox_kernel,
        grid=(batch, N_PAIRS),
        in_specs=[pl.BlockSpec((1, seq, PAIR), lambda b, p: (b, 0, p)),
                  pl.BlockSpec((1, seq, PAIR), lambda b, p: (b, 0, N_PAIRS + p)),
                  pl.BlockSpec((1, PAIR, seq), lambda b, p: (b, p, 0)),
                  pl.BlockSpec((1, seq, LANES), whole),
                  pl.BlockSpec((1, HEADS, seq), whole),
                  pl.BlockSpec((2, blk, blk), lambda b, p: (0, 0, 0))],
        out_specs=pl.BlockSpec((1, PAIR, seq), lambda b, p: (b, p, 0)),
        out_shape=jax.ShapeDtypeStruct((batch, WIDTH, seq), BF16),
        scratch_shapes=[pltpu.VMEM((2, seq, LANES), BF16),
                        pltpu.VMEM((2, seq, LANES), BF16),
                        pltpu.VMEM((2, FOX_VROWS, seq), BF16),
                        pltpu.VMEM((2, 2, blk, blk), F32),
                        pltpu.VMEM((2, 2, blk, blk), BF16)],
        compiler_params=pltpu.CompilerParams(
            dimension_semantics=("arbitrary", "arbitrary"),
            vmem_limit_bytes=VMEM_LIMIT_BYTES),
        name="fox",
    )(qk, qk, v_t, cum, cum_t, mask)


def _tail_kernel(x_ref, ya_ref, ybt_ref, gate_ref, mod_ref, g2_ref, gf_ref,
                 woa_ref, wob_ref, wout_ref, w1_ref, w2_ref, o_ref):
    x = x_ref[0]
    gate1 = mod_ref[0, 2:3, :]
    shift2 = mod_ref[0, 3:4, :]
    scale2 = mod_ref[0, 4:5, :]
    gate2 = mod_ref[0, 5:6, :]

    merged = (gate_ref[0, :, 0:D_MODEL].astype(F32)
              * jnp.dot(ya_ref[0], woa_ref[...], preferred_element_type=F32)
              + gate_ref[0, :, D_MODEL:N_GATE].astype(F32)
              * lax.dot_general(ybt_ref[0], wob_ref[...], (((0,), (0,)), ((), ())),
                                preferred_element_type=F32))
    x = x + gate1 * _dot(merged, wout_ref[...])

    inv = lax.rsqrt(jnp.mean(x * x, axis=-1, keepdims=True) + NORM_EPS)
    h2 = ((x * inv) * g2_ref[...] * (1.0 + scale2) + shift2).astype(BF16)
    ff = jnp.zeros_like(x)
    for j in range(D_FF // FF_CHUNK):
        cols = slice(j * FF_CHUNK, (j + 1) * FF_CHUNK)
        hid = jnp.maximum(jnp.dot(h2, w1_ref[:, cols], preferred_element_type=F32), 0.0)
        ff = ff + _dot(hid * hid, w2_ref[cols, :])
    x = x + gate2 * ff

    inv = lax.rsqrt(jnp.mean(x * x, axis=-1, keepdims=True) + NORM_EPS)
    o_ref[0] = (x * inv) * gf_ref[...]


def _tail(x, y_a, y_b, gates, mod3, norm2_g, final_g, w_oa, w_ob, w_out, w_ff1, w_ff2):
    batch, seq, _ = x.shape
    tm = TAIL_ROWS
    grid = (batch, seq // tm)
    const = lambda b, s: (0, 0)
    tile = lambda b, s: (b, s, 0)
    resident = lambda a: pl.BlockSpec(a.shape, const, pipeline_mode=pl.Buffered(1))
    return pl.pallas_call(
        _tail_kernel,
        grid=grid,
        in_specs=[pl.BlockSpec((1, tm, D_MODEL), tile),
                  pl.BlockSpec((1, tm, WIDTH), tile),
                  pl.BlockSpec((1, WIDTH, tm), lambda b, s: (b, 0, s)),
                  pl.BlockSpec((1, tm, N_GATE), tile),
                  pl.BlockSpec((1, N_MOD, D_MODEL), lambda b, s: (b, 0, 0)),
                  pl.BlockSpec((1, D_MODEL), const),
                  pl.BlockSpec((1, D_MODEL), const),
                  resident(w_oa), resident(w_ob), resident(w_out),
                  resident(w_ff1), resident(w_ff2)],
        out_specs=pl.BlockSpec((1, tm, D_MODEL), tile),
        out_shape=jax.ShapeDtypeStruct((batch, seq, D_MODEL), F32),
        compiler_params=pltpu.CompilerParams(dimension_semantics=("arbitrary", "arbitrary"),
                                             vmem_limit_bytes=VMEM_LIMIT_BYTES),
        name="tail",
    )(x, y_a, y_b, gates, mod3, norm2_g, final_g, w_oa, w_ob, w_out, w_ff1, w_ff2)


def _reorder_rwkv_cols(t):
    o = 0
    r = t[..., o:o + WIDTH]; o += WIDTH
    wd = t[..., o:o + DECAY_RANK]; o += DECAY_RANK
    k = t[..., o:o + WIDTH]; o += WIDTH
    v = t[..., o:o + WIDTH]; o += WIDTH
    ad = t[..., o:o + ICLR_RANK]; o += ICLR_RANK
    gd = t[..., o:o + GATE_RANK]
    return jnp.concatenate([r, k, v, wd, ad, gd], axis=-1)


def kernel(x, c, w_ada, b_ada, norm1_g, w_in, mu_shift, w_decay_up, decay_base, w_iclr_up, iclr_base, w_gate_up, kk_scale, k_iclr_mix, r_bonus, lnx_w, lnx_b, fox_f_bias, w_o_rwkv, w_o_fox, w_out, norm2_g, w_ff1, w_ff2, final_g):
    assert w_ada.shape[0] == 1, "the tail kernel fuses the final norm: single layer only"
    l = 0
    n_rwkv = N_RKV + N_SMALL
    seg_id = jnp.arange(WIDTH // 2) // HEAD_DIM
    seg = (seg_id[:, None] == seg_id[None, :]).astype(BF16)

    mod3 = _modulation(c, w_ada[l], b_ada[l]).reshape(-1, N_MOD, D_MODEL)

    w = w_in[l]
    w_proj = jnp.concatenate(
        [_reorder_rwkv_cols(w[:, :n_rwkv]),
         w[:, n_rwkv:n_rwkv + N_FOX],
         jnp.pad(w[:, n_rwkv + N_FOX:n_rwkv + N_FOX + HEADS], ((0, 0), (0, LANES - HEADS))),
         w[:, n_rwkv + N_FOX + HEADS:]], axis=1).astype(BF16)
    mu = _reorder_rwkv_cols(mu_shift[l]).reshape(1, n_rwkv)
    f_bias = jnp.pad(fox_f_bias[l], (0, LANES - HEADS)).reshape(1, LANES)
    rkv, small, qk, v_t, cum, gates = _projection(
        x, mod3, norm1_g[l].reshape(1, D_MODEL), w_proj, mu, f_bias)

    zeros = jnp.zeros((DECAY_RANK, WIDTH), F32)
    w_lr = jnp.concatenate(
        [jnp.concatenate([w_decay_up[l], zeros], axis=1),
         jnp.concatenate([zeros, w_iclr_up[l]], axis=1)], axis=0).astype(BF16)
    vecs = jnp.stack([decay_base[l], iclr_base[l], kk_scale[l], k_iclr_mix[l],
                      r_bonus[l].reshape(WIDTH), lnx_w[l], lnx_b[l],
                      jnp.zeros((WIDTH,), F32)], axis=0)
    y_a = _rwkv_mix(rkv, small, w_lr, w_gate_up[l].astype(BF16), vecs, seg)

    cum_t = jnp.transpose(cum[:, :, :HEADS], (0, 2, 1))
    y_b = _forgetting_attention(qk, v_t, cum, cum_t)

    return _tail(x, y_a, y_b, gates, mod3, norm2_g[l].reshape(1, D_MODEL),
                 final_g.reshape(1, D_MODEL),
                 w_o_rwkv[l].astype(BF16), w_o_fox[l].astype(BF16), w_out[l].astype(BF16),
                 w_ff1[l].astype(BF16), w_ff2[l].astype(BF16))
```

```python
import jax
import jax.numpy as jnp
from jax import lax
from jax.experimental import pallas as pl
from jax.experimental.pallas import tpu as pltpu

F32 = jnp.float32
BF16 = jnp.bfloat16

D_MODEL = 1024
HEAD_DIM = 64
HEADS = 8
WIDTH = HEADS * HEAD_DIM
DECAY_RANK = 64
ICLR_RANK = 64
GATE_RANK = 128
D_FF = 4 * D_MODEL
N_MOD = 6
NORM_EPS = 1e-6
GN_EPS = 64e-5

LANES = 128
PAIR = 2 * HEAD_DIM
N_PAIRS = HEADS // 2

LOG2E = 1.4426950408889634

N_SMALL = DECAY_RANK + ICLR_RANK + GATE_RANK
N_RKV = 3 * WIDTH
N_FOX = 3 * WIDTH
N_GATE = 2 * D_MODEL
OFF_RKV = 0
OFF_SMALL = OFF_RKV + N_RKV
OFF_FOX = OFF_SMALL + N_SMALL
OFF_FF = OFF_FOX + N_FOX
OFF_GATE = OFF_FF + LANES
N_PROJ = OFF_GATE + N_GATE

PROJ_ROWS = 256
CHUNK = 128
FOX_BLOCK = 512
FOX_SUB = 64
FOX_VROWS = HEAD_DIM + 16
FOX_MASKED = -1e30
TAIL_ROWS = 512
FF_CHUNK = 1024

VMEM_LIMIT_BYTES = 56 * 1024 * 1024


def _dot(a, b):
    return jnp.dot(a.astype(BF16), b.astype(BF16), preferred_element_type=F32)


def _dot_nt(a, b):
    return lax.dot_general(a.astype(BF16), b.astype(BF16), (((1,), (1,)), ((), ())),
                           preferred_element_type=F32)


def _softplus(z):
    return jnp.maximum(z, 0.0) + jnp.log(1.0 + jnp.exp(-jnp.abs(z)))


def _sigmoid(z):
    return 1.0 / (1.0 + jnp.exp(-z))


def _bf16_parts(x, n):
    parts = []
    for _ in range(n):
        p = x.astype(BF16)
        parts.append(p)
        x = x - p.astype(F32)
    return parts


def _cumsum_rows(tri, x, n_parts):
    return sum(jnp.dot(tri, p, preferred_element_type=F32) for p in _bf16_parts(x, n_parts))


def _head_sums(x, seg):
    half = seg.shape[0]
    return jnp.concatenate([_dot(x[:, 0:half], seg), _dot(x[:, half:2 * half], seg)], axis=1)


def _lower_tri(n, dtype):
    row = lax.broadcasted_iota(jnp.int32, (n, n), 0)
    col = lax.broadcasted_iota(jnp.int32, (n, n), 1)
    return (col <= row).astype(dtype)


def _mod_kernel(c_ref, w_ref, b_ref, o_ref):
    c = c_ref[...]
    c_act = c * _sigmoid(c)
    o_ref[...] = jnp.dot(c_act, w_ref[...], preferred_element_type=F32,
                         precision=lax.Precision.HIGHEST) + b_ref[...]


def _modulation(c, w_ada, b_ada):
    batch = c.shape[0]
    n = w_ada.shape[1]
    tn = 1536
    return pl.pallas_call(
        _mod_kernel,
        grid=(n // tn,),
        in_specs=[pl.BlockSpec((batch, D_MODEL), lambda j: (0, 0)),
                  pl.BlockSpec((D_MODEL, tn), lambda j: (0, j)),
                  pl.BlockSpec((1, tn), lambda j: (0, j))],
        out_specs=pl.BlockSpec((batch, tn), lambda j: (0, j)),
        out_shape=jax.ShapeDtypeStruct((batch, n), F32),
        compiler_params=pltpu.CompilerParams(dimension_semantics=("arbitrary",),
                                             vmem_limit_bytes=VMEM_LIMIT_BYTES),
        name="mod",
    )(c, w_ada, b_ada.reshape(1, n))


def _proj_kernel(x_ref, mod_ref, g_ref, w_ref, mu_ref, fb_ref,
                 rkv_ref, small_ref, qk_ref, vt_ref, cum_ref, cumt_ref, gate_ref,
                 shift_ref, carry_ref):
    s = pl.program_id(1)
    rows = x_ref.shape[1]

    @pl.when(s == 0)
    def _():
        shift_ref[0:8, :] = jnp.zeros((8, N_RKV + N_SMALL), F32)
        carry_ref[...] = jnp.zeros_like(carry_ref)

    x = x_ref[0]
    shift1 = mod_ref[0, 0:1, :]
    scale1 = mod_ref[0, 1:2, :]
    inv = lax.rsqrt(jnp.mean(x * x, axis=-1, keepdims=True) + NORM_EPS)
    h = ((x * inv) * g_ref[...] * (1.0 + scale1) + shift1).astype(BF16)

    n_tok = N_RKV + N_SMALL
    p = jnp.dot(h, w_ref[:, OFF_RKV:OFF_RKV + n_tok], preferred_element_type=F32)
    shift_ref[8:8 + rows, :] = p
    prev = shift_ref[7:7 + rows, :]
    shift_ref[7:8, :] = p[rows - 1:rows, :]
    mixed = p + mu_ref[...] * (prev - p)
    rkv_ref[0] = mixed[:, 0:N_RKV].astype(BF16)
    small_ref[0] = mixed[:, N_RKV:n_tok]

    pf = jnp.dot(h, w_ref[:, OFF_FOX:OFF_FOX + N_FOX], preferred_element_type=F32)
    qk_ref[0, :, 0:WIDTH] = (pf[:, 0:WIDTH] * (LOG2E * HEAD_DIM ** -0.5)).astype(BF16)
    qk_ref[0, :, WIDTH:2 * WIDTH] = pf[:, WIDTH:2 * WIDTH].astype(BF16)
    vt_ref[0] = jnp.transpose(pf[:, 2 * WIDTH:N_FOX]).astype(BF16)

    ff = jnp.dot(h, w_ref[:, OFF_FF:OFF_FF + LANES], preferred_element_type=F32)
    logf = -_softplus(-(ff + fb_ref[...]))
    cum = _cumsum_rows(_lower_tri(rows, BF16), logf, 3) + carry_ref[...]
    cum_ref[0] = cum
    cumt_ref[0] = jnp.transpose(cum)[0:HEADS, :]
    carry_ref[...] = cum[rows - 1:rows, :]

    pg = jnp.dot(h, w_ref[:, OFF_GATE:OFF_GATE + N_GATE], preferred_element_type=F32)
    gate_ref[0] = _sigmoid(pg).astype(BF16)


def _projection(x, mod3, norm_g, w_proj, mu, f_bias):
    batch, seq, _ = x.shape
    tm = PROJ_ROWS
    grid = (batch, seq // tm)
    const = lambda b, s: (0, 0)
    tile = lambda b, s: (b, s, 0)
    return pl.pallas_call(
        _proj_kernel,
        grid=grid,
        in_specs=[pl.BlockSpec((1, tm, D_MODEL), tile),
                  pl.BlockSpec((1, N_MOD, D_MODEL), lambda b, s: (b, 0, 0)),
                  pl.BlockSpec((1, D_MODEL), const),
                  pl.BlockSpec((D_MODEL, N_PROJ), const, pipeline_mode=pl.Buffered(1)),
                  pl.BlockSpec((1, N_RKV + N_SMALL), const),
                  pl.BlockSpec((1, LANES), const)],
        out_specs=[pl.BlockSpec((1, tm, N_RKV), tile),
                   pl.BlockSpec((1, tm, N_SMALL), tile),
                   pl.BlockSpec((1, tm, 2 * WIDTH), tile),
                   pl.BlockSpec((1, WIDTH, tm), lambda b, s: (b, 0, s)),
                   pl.BlockSpec((1, tm, LANES), tile),
                   pl.BlockSpec((1, HEADS, tm), lambda b, s: (b, 0, s)),
                   pl.BlockSpec((1, tm, N_GATE), tile)],
        out_shape=[jax.ShapeDtypeStruct((batch, seq, N_RKV), BF16),
                   jax.ShapeDtypeStruct((batch, seq, N_SMALL), F32),
                   jax.ShapeDtypeStruct((batch, seq, 2 * WIDTH), BF16),
                   jax.ShapeDtypeStruct((batch, WIDTH, seq), BF16),
                   jax.ShapeDtypeStruct((batch, seq, LANES), F32),
                   jax.ShapeDtypeStruct((batch, HEADS, seq), F32),
                   jax.ShapeDtypeStruct((batch, seq, N_GATE), BF16)],
        scratch_shapes=[pltpu.VMEM((tm + 8, N_RKV + N_SMALL), F32),
                        pltpu.VMEM((1, LANES), F32)],
        compiler_params=pltpu.CompilerParams(dimension_semantics=("arbitrary", "arbitrary"),
                                             vmem_limit_bytes=VMEM_LIMIT_BYTES),
        name="proj",
    )(x, mod3, norm_g, w_proj, mu, f_bias)


def _rwkv_prepare(rkv, small, w_lr_ref, w_gate_ref, vec_ref, seg_ref, slot, prep):
    am_ref, rm_ref, vb_ref, bt_ref, kt_ref, bh_ref, kh_ref, dend_ref, g_ref, bv_ref = prep
    c = CHUNK
    r = rkv[:, 0:WIDTH].astype(F32)
    k_raw = rkv[:, WIDTH:2 * WIDTH].astype(F32)
    v = rkv[:, 2 * WIDTH:3 * WIDTH].astype(F32)
    decay_base, iclr_base, kk_scale, k_mix, r_bonus = (vec_ref[i:i + 1, :] for i in range(5))

    lane = lax.broadcasted_iota(jnp.int32, (1, LANES), 1)
    lr_in = jnp.where(lane < DECAY_RANK, jnp.tanh(small[:, 0:LANES]), small[:, 0:LANES])
    lr = _dot(lr_in, w_lr_ref[...])
    w_log = -_softplus(-(decay_base + lr[:, 0:WIDTH])) - 0.5
    log_decay = -jnp.exp(w_log)
    a = _sigmoid(iclr_base + lr[:, WIDTH:2 * WIDTH])
    g_ref[slot] = _dot(_sigmoid(small[:, LANES:2 * LANES]), w_gate_ref[...])
    yield

    seg = seg_ref[...]
    kk = k_raw * kk_scale
    kk = kk * lax.rsqrt(jnp.maximum(_head_sums(kk * kk, seg), 1e-24))
    k = k_raw * (1.0 + (a - 1.0) * k_mix)
    b_vec = kk * a
    yield

    cs = _cumsum_rows(_lower_tri(c, BF16), log_decay, 2)
    cs_end = cs[c - 1:c, :]
    am_ref[slot] = (-kk * jnp.exp(cs - log_decay)).astype(BF16)
    rm_ref[slot] = (r * jnp.exp(cs)).astype(BF16)
    vb_ref[slot] = v.astype(BF16)
    yield
    w_inv = jnp.exp(-cs)
    bt_ref[slot] = jnp.transpose(b_vec * w_inv).astype(BF16)
    yield
    kt_ref[slot] = jnp.transpose(k * w_inv).astype(BF16)
    yield
    w_end = jnp.exp(cs_end - cs)
    bh_ref[slot] = jnp.transpose(b_vec * w_end).astype(BF16)
    yield
    kh_ref[slot] = jnp.transpose(k * w_end).astype(BF16)
    dend_ref[slot] = jnp.broadcast_to(jnp.exp(cs_end), (8, WIDTH))
    yield
    bv_ref[slot] = _head_sums(r * k * r_bonus, seg) * v


def _rwkv_chunk(slot, prep, vec_ref, seg_ref, z_ref, out_ref, rows):
    am_ref, rm_ref, vb_ref, bt_ref, kt_ref, bh_ref, kh_ref, dend_ref, g_ref, bv_ref = prep
    c = CHUNK
    ln_w, ln_b = vec_ref[5:6, :], vec_ref[6:7, :]
    lane = lax.broadcasted_iota(jnp.int32, (1, LANES), 1)
    row = lax.broadcasted_iota(jnp.int32, (c, c), 0)
    col = lax.broadcasted_iota(jnp.int32, (c, c), 1)
    strict = col < row
    incl = col <= row
    eye = (col == row).astype(F32)
    same_head = (row // HEAD_DIM) == (col // HEAD_DIM)

    pairs = range(N_PAIRS)
    sls = [slice(p * PAIR, (p + 1) * PAIR) for p in pairs]
    v_b = [vb_ref[slot, :, sl] for sl in sls]
    z_f = [z_ref[p] for p in pairs]
    z_b = [z.astype(BF16) for z in z_f]
    heads = [(p, h) for p in pairs for h in range(2)]
    a_m, r_m, a_ab, a_ak, m_rb, m_rk = {}, {}, {}, {}, {}, {}
    for (p, h) in heads:
        head = (lane // HEAD_DIM) == h
        a_m[p, h] = jnp.where(head, am_ref[slot, :, sls[p]].astype(F32), 0.0).astype(BF16)
        r_m[p, h] = jnp.where(head, rm_ref[slot, :, sls[p]].astype(F32), 0.0).astype(BF16)
        rhs = jnp.concatenate([bt_ref[slot, sls[p], :], kt_ref[slot, sls[p], :]], axis=1)
        big = jnp.dot(jnp.concatenate([a_m[p, h], r_m[p, h]], axis=0), rhs,
                      preferred_element_type=F32)
        a_ab[p, h] = jnp.where(strict, big[0:c, 0:c], 0.0)
        a_ak[p, h] = jnp.where(strict, big[0:c, c:2 * c], 0.0).astype(BF16)
        m_rb[p, h] = jnp.where(incl, big[c:2 * c, 0:c], 0.0).astype(BF16)
        m_rk[p, h] = jnp.where(incl, big[c:2 * c, c:2 * c], 0.0).astype(BF16)
    yield
    t_inv = {hd: eye + a_ab[hd] for hd in heads}
    m_pow = {hd: _dot(a_ab[hd], a_ab[hd]) for hd in heads}
    yield
    for _ in range(5):
        for hd in heads:
            both = _dot(m_pow[hd], jnp.concatenate([m_pow[hd], t_inv[hd]], axis=1))
            m_pow[hd] = both[:, 0:c]
            t_inv[hd] = t_inv[hd] + both[:, c:2 * c]
        yield
    for hd in heads:
        t_inv[hd] = t_inv[hd] + _dot(m_pow[hd], t_inv[hd])
    yield
    ak_v = {(p, h): _dot(a_ak[p, h], v_b[p]) for (p, h) in heads}
    pq = {hd: _dot(t_inv[hd], jnp.concatenate([a_m[hd].astype(F32), ak_v[hd]], axis=1))
          for hd in heads}
    u = {(p, h): _dot(pq[p, h][:, 0:PAIR], z_b[p]) + pq[p, h][:, PAIR:2 * PAIR]
         for (p, h) in heads}
    y_hd = {(p, h): jnp.dot(
        jnp.concatenate([r_m[p, h], m_rb[p, h], m_rk[p, h]], axis=1),
        jnp.concatenate([z_b[p], u[p, h].astype(BF16), v_b[p]], axis=0),
        preferred_element_type=F32) for (p, h) in heads}
    first = lane < HEAD_DIM
    y_pairs = []
    for p in pairs:
        u_p = jnp.where(first, u[p, 0], u[p, 1])
        y_pairs.append(jnp.where(first, y_hd[p, 0], y_hd[p, 1]))
        d_col = jnp.sum(eye * dend_ref[slot, 0:1, sls[p]], axis=1, keepdims=True)
        z_new = d_col * z_f[p] + jnp.dot(
            jnp.concatenate([bh_ref[slot, sls[p], :], kh_ref[slot, sls[p], :]], axis=1),
            jnp.concatenate([u_p.astype(BF16), v_b[p]], axis=0), preferred_element_type=F32)
        z_ref[p] = jnp.where(same_head, z_new, 0.0)

    seg = seg_ref[...]
    y = jnp.concatenate(y_pairs, axis=1)
    mean = _head_sums(y, seg) * (1.0 / HEAD_DIM)
    yc = y - mean
    var = _head_sums(yc * yc, seg) * (1.0 / HEAD_DIM)
    y = yc * lax.rsqrt(var + GN_EPS) * ln_w + ln_b
    out_ref[0, rows, :] = ((y + bv_ref[slot]) * g_ref[slot]).astype(BF16)


def _interleave(*stages):
    live = list(stages)
    while live:
        for gen in list(live):
            try:
                next(gen)
            except StopIteration:
                live.remove(gen)


def _rwkv_kernel(rkv_ref, small_ref, rkv_next_ref, small_next_ref,
                 w_lr_ref, w_gate_ref, vec_ref, seg_ref, out_ref, z_ref, *prep):
    c = CHUNK
    params = (w_lr_ref, w_gate_ref, vec_ref, seg_ref)

    @pl.when(pl.program_id(1) == 0)
    def _():
        z_ref[...] = jnp.zeros_like(z_ref)
        _interleave(_rwkv_prepare(rkv_ref[0, 0:c, :], small_ref[0, 0:c, :], *params, 0, prep))

    _interleave(_rwkv_chunk(0, prep, vec_ref, seg_ref, z_ref, out_ref, slice(0, c)),
                _rwkv_prepare(rkv_ref[0, c:2 * c, :], small_ref[0, c:2 * c, :], *params, 1, prep))
    _interleave(_rwkv_chunk(1, prep, vec_ref, seg_ref, z_ref, out_ref, slice(c, 2 * c)),
                _rwkv_prepare(rkv_next_ref[0], small_next_ref[0], *params, 0, prep))


def _rwkv_mix(rkv, small, w_lr, w_gate, vecs, seg):
    batch, seq, _ = rkv.shape
    c = CHUNK
    n_chunks = seq // c
    grid = (batch, n_chunks // 2)
    const = lambda b, s: (0, 0)
    tile = lambda b, s: (b, s, 0)
    ahead = lambda b, s: (b, jnp.minimum(2 * s + 2, n_chunks - 1), 0)
    slots = lambda shape, dtype: pltpu.VMEM((2,) + shape, dtype)
    return pl.pallas_call(
        _rwkv_kernel,
        grid=grid,
        in_specs=[pl.BlockSpec((1, 2 * c, N_RKV), tile),
                  pl.BlockSpec((1, 2 * c, N_SMALL), tile),
                  pl.BlockSpec((1, c, N_RKV), ahead),
                  pl.BlockSpec((1, c, N_SMALL), ahead),
                  pl.BlockSpec(w_lr.shape, const),
                  pl.BlockSpec(w_gate.shape, const),
                  pl.BlockSpec(vecs.shape, const),
                  pl.BlockSpec(seg.shape, const)],
        out_specs=pl.BlockSpec((1, 2 * c, WIDTH), tile),
        out_shape=jax.ShapeDtypeStruct((batch, seq, WIDTH), BF16),
        scratch_shapes=[pltpu.VMEM((N_PAIRS, PAIR, PAIR), F32),
                        slots((c, WIDTH), BF16),
                        slots((c, WIDTH), BF16),
                        slots((c, WIDTH), BF16),
                        slots((WIDTH, c), BF16),
                        slots((WIDTH, c), BF16),
                        slots((WIDTH, c), BF16),
                        slots((WIDTH, c), BF16),
                        slots((8, WIDTH), F32),
                        slots((c, WIDTH), F32),
                        slots((c, WIDTH), F32)],
        compiler_params=pltpu.CompilerParams(dimension_semantics=("arbitrary", "arbitrary"),
                                             vmem_limit_bytes=VMEM_LIMIT_BYTES),
        name="rwkv",
    )(rkv, small, rkv, small, w_lr, w_gate, vecs, seg)


def _fox_kernel(q_ref, k_ref, vt_ref, cum_ref, cumt_ref, mask_ref, o_ref,
                qx_ref, kx_ref, vx_ref, t_ref, p_ref):
    hp = pl.program_id(1)
    blk = FOX_BLOCK
    seq = k_ref.shape[1]
    n_blk = seq // blk
    lane = lax.broadcasted_iota(jnp.int32, (1, LANES), 1)
    own = [(lane // HEAD_DIM) == h for h in range(2)]
    bias_lane = [HEAD_DIM * (1 - h) for h in range(2)]

    def fill(i, _):
        rows = pl.ds(pl.multiple_of(i * blk, blk), blk)
        cum_rows = cum_ref[0, rows, :]
        k_rows = k_ref[0, rows, :].astype(F32)
        q_rows = q_ref[0, rows, :].astype(F32)
        for h in range(2):
            col = jnp.sum(jnp.where(lane == 2 * hp + h, cum_rows, 0.0), axis=1, keepdims=True)
            parts = _bf16_parts(jnp.broadcast_to(col * (-LOG2E), (blk, LANES)), 3)
            feat = jnp.zeros((blk, LANES), F32)
            for n, part in enumerate(parts):
                feat = jnp.where(lane == bias_lane[h] + n, part.astype(F32), feat)
            kx_ref[h, rows, :] = jnp.where(own[h], k_rows, feat).astype(BF16)
            is_bias = (lane >= bias_lane[h]) & (lane < bias_lane[h] + 3)
            qx_ref[h, rows, :] = jnp.where(own[h], q_rows, is_bias.astype(F32)).astype(BF16)
        return 0
    lax.fori_loop(0, n_blk, fill, 0)
    ones_row = lax.broadcasted_iota(jnp.int32, (FOX_VROWS - HEAD_DIM, seq), 0) == 0
    for h in range(2):
        vx_ref[h, 0:HEAD_DIM, :] = vt_ref[0, h * HEAD_DIM:(h + 1) * HEAD_DIM, :]
        vx_ref[h, HEAD_DIM:FOX_VROWS, :] = ones_row.astype(BF16)

    def rows_of(b):
        return pl.ds(pl.multiple_of(b * blk, blk), blk)

    def following(qi, kj):
        wrap = kj == qi
        return jnp.where(wrap, qi + 1, qi), jnp.where(wrap, 0, kj + 1)

    def scores(pair, slot):
        qi, kj = pair
        qi = jnp.minimum(qi, n_blk - 1)
        causal = mask_ref[(kj == qi).astype(jnp.int32)]
        for h in range(2):
            t_ref[slot, h] = _dot_nt(kx_ref[h, rows_of(kj), :], qx_ref[h, rows_of(qi), :]) + causal

    def softmax(pair, slot, m):
        qi, kj = pair
        sub = FOX_SUB
        q_rows = rows_of(jnp.minimum(qi, n_blk - 1))
        m_out, alpha_out = [], []
        for h in range(2):
            m_old = jnp.where(kj == 0, -jnp.inf, m[h])
            c_q = cumt_ref[0, pl.ds(2 * hp + h, 1), q_rows] * LOG2E
            load = lambda i: t_ref[slot, h, i * sub:(i + 1) * sub, :]
            mx = jnp.max(load(0).reshape(sub // 8, 8, blk), axis=0)
            for i in range(1, blk // sub):
                mx = jnp.maximum(mx, jnp.max(load(i).reshape(sub // 8, 8, blk), axis=0))
            m_new = jnp.maximum(m_old, jnp.max(mx, axis=0, keepdims=True) + c_q)
            shift = c_q - m_new
            for i in range(blk // sub):
                p_ref[slot, h, i * sub:(i + 1) * sub, :] = jnp.exp2(load(i) + shift).astype(BF16)
            m_out.append(m_new)
            alpha_out.append(jnp.exp2(m_old - m_new))
        return tuple(m_out), tuple(alpha_out)

    def accumulate(pair, slot, alpha, acc):
        qi, kj = pair
        acc = tuple(alpha[h] * acc[h] + jnp.dot(vx_ref[h, :, rows_of(kj)], p_ref[slot, h],
                                                preferred_element_type=F32) for h in range(2))
        o_t = jnp.concatenate([a[0:HEAD_DIM] / a[HEAD_DIM:HEAD_DIM + 1] for a in acc], axis=0)
        o_ref[0, :, rows_of(qi)] = o_t.astype(BF16)
        return acc

    def two_pairs(_, carry):
        pair0, pair1, m, alpha0, acc = carry
        pair2 = following(*pair1)
        scores(pair2, 0)
        m, alpha1 = softmax(pair1, 1, m)
        acc = accumulate(pair0, 0, alpha0, acc)
        pair3 = following(*pair2)
        scores(pair3, 1)
        m, alpha2 = softmax(pair2, 0, m)
        acc = accumulate(pair1, 1, alpha1, acc)
        return pair2, pair3, m, alpha2, acc

    n_pairs = n_blk * (n_blk + 1) // 2
    assert n_pairs % 2 == 0, "two pairs per trip"
    zero = jnp.int32(0)
    pair0 = (zero, zero)
    pair1 = following(*pair0)
    scores(pair0, 0)
    scores(pair1, 1)
    m = tuple(jnp.full((1, blk), -jnp.inf, F32) for _ in range(2))
    m, alpha0 = softmax(pair0, 0, m)
    acc = tuple(jnp.zeros((FOX_VROWS, blk), F32) for _ in range(2))
    lax.fori_loop(0, n_pairs // 2, two_pairs, (pair0, pair1, m, alpha0, acc))


def _forgetting_attention(qk, v_t, cum, cum_t):
    batch, seq, _ = qk.shape
    blk = FOX_BLOCK
    key = lax.broadcasted_iota(jnp.int32, (blk, blk), 0)
    qry = lax.broadcasted_iota(jnp.int32, (blk, blk), 1)
    mask = jnp.stack([jnp.zeros((blk, blk), F32), jnp.where(key <= qry, 0.0, FOX_MASKED)])
    whole = lambda b, p: (b, 0, 0)
    return pl.pallas_call(
        _fox_kernel,
        grid=(batch, N_PAIRS),
        in_specs=[pl.BlockSpec((1, seq, PAIR), lambda b, p: (b, 0, p)),
                  pl.BlockSpec((1, seq, PAIR), lambda b, p: (b, 0, N_PAIRS + p)),
                  pl.BlockSpec((1, PAIR, seq), lambda b, p: (b, p, 0)),
                  pl.BlockSpec((1, seq, LANES), whole),
                  pl.BlockSpec((1, HEADS, seq), whole),
                  pl.BlockSpec((2, blk, blk), lambda b, p: (0, 0, 0))],
        out_specs=pl.BlockSpec((1, PAIR, seq), lambda b, p: (b, p, 0)),
        out_shape=jax.ShapeDtypeStruct((batch, WIDTH, seq), BF16),
        scratch_shapes=[pltpu.VMEM((2, seq, LANES), BF16),
                        pltpu.VMEM((2, seq, LANES), BF16),
                        pltpu.VMEM((2, FOX_VROWS, seq), BF16),
                        pltpu.VMEM((2, 2, blk, blk), F32),
                        pltpu.VMEM((2, 2, blk, blk), BF16)],
        compiler_params=pltpu.CompilerParams(
            dimension_semantics=("arbitrary", "arbitrary"),
            vmem_limit_bytes=VMEM_LIMIT_BYTES),
        name="fox",
    )(qk, qk, v_t, cum, cum_t, mask)


def _tail_kernel(x_ref, ya_ref, ybt_ref, gate_ref, mod_ref, g2_ref, gf_ref,
                 woa_ref, wob_ref, wout_ref, w1_ref, w2_ref, o_ref):
    x = x_ref[0]
    gate1 = mod_ref[0, 2:3, :]
    shift2 = mod_ref[0, 3:4, :]
    scale2 = mod_ref[0, 4:5, :]
    gate2 = mod_ref[0, 5:6, :]

    merged = (gate_ref[0, :, 0:D_MODEL].astype(F32)
              * jnp.dot(ya_ref[0], woa_ref[...], preferred_element_type=F32)
              + gate_ref[0, :, D_MODEL:N_GATE].astype(F32)
              * lax.dot_general(ybt_ref[0], wob_ref[...], (((0,), (0,)), ((), ())),
                                preferred_element_type=F32))
    x = x + gate1 * _dot(merged, wout_ref[...])

    inv = lax.rsqrt(jnp.mean(x * x, axis=-1, keepdims=True) + NORM_EPS)
    h2 = ((x * inv) * g2_ref[...] * (1.0 + scale2) + shift2).astype(BF16)
    ff = jnp.zeros_like(x)
    for j in range(D_FF // FF_CHUNK):
        cols = slice(j * FF_CHUNK, (j + 1) * FF_CHUNK)
        hid = jnp.maximum(jnp.dot(h2, w1_ref[:, cols], preferred_element_type=F32), 0.0)
        ff = ff + _dot(hid * hid, w2_ref[cols, :])
    x = x + gate2 * ff

    inv = lax.rsqrt(jnp.mean(x * x, axis=-1, keepdims=True) + NORM_EPS)
    o_ref[0] = (x * inv) * gf_ref[...]


def _tail(x, y_a, y_b, gates, mod3, norm2_g, final_g, w_oa, w_ob, w_out, w_ff1, w_ff2):
    batch, seq, _ = x.shape
    tm = TAIL_ROWS
    grid = (batch, seq // tm)
    const = lambda b, s: (0, 0)
    tile = lambda b, s: (b, s, 0)
    resident = lambda a: pl.BlockSpec(a.shape, const, pipeline_mode=pl.Buffered(1))
    return pl.pallas_call(
        _tail_kernel,
        grid=grid,
        in_specs=[pl.BlockSpec((1, tm, D_MODEL), tile),
                  pl.BlockSpec((1, tm, WIDTH), tile),
                  pl.BlockSpec((1, WIDTH, tm), lambda b, s: (b, 0, s)),
                  pl.BlockSpec((1, tm, N_GATE), tile),
                  pl.BlockSpec((1, N_MOD, D_MODEL), lambda b, s: (b, 0, 0)),
                  pl.BlockSpec((1, D_MODEL), const),
                  pl.BlockSpec((1, D_MODEL), const),
                  resident(w_oa), resident(w_ob), resident(w_out),
                  resident(w_ff1), resident(w_ff2)],
        out_specs=pl.BlockSpec((1, tm, D_MODEL), tile),
        out_shape=jax.ShapeDtypeStruct((batch, seq, D_MODEL), F32),
        compiler_params=pltpu.CompilerParams(dimension_semantics=("arbitrary", "arbitrary"),
                                             vmem_limit_bytes=VMEM_LIMIT_BYTES),
        name="tail",
    )(x, y_a, y_b, gates, mod3, norm2_g, final_g, w_oa, w_ob, w_out, w_ff1, w_ff2)


def _reorder_rwkv_cols(t):
    o = 0
    r = t[..., o:o + WIDTH]; o += WIDTH
    wd = t[..., o:o + DECAY_RANK]; o += DECAY_RANK
    k = t[..., o:o + WIDTH]; o += WIDTH
    v = t[..., o:o + WIDTH]; o += WIDTH
    ad = t[..., o:o + ICLR_RANK]; o += ICLR_RANK
    gd = t[..., o:o + GATE_RANK]
    return jnp.concatenate([r, k, v, wd, ad, gd], axis=-1)


def kernel(x, c, w_ada, b_ada, norm1_g, w_in, mu_shift, w_decay_up, decay_base, w_iclr_up, iclr_base, w_gate_up, kk_scale, k_iclr_mix, r_bonus, lnx_w, lnx_b, fox_f_bias, w_o_rwkv, w_o_fox, w_out, norm2_g, w_ff1, w_ff2, final_g):
    assert w_ada.shape[0] == 1, "the tail kernel fuses the final norm: single layer only"
    l = 0
    n_rwkv = N_RKV + N_SMALL
    seg_id = jnp.arange(WIDTH // 2) // HEAD_DIM
    seg = (seg_id[:, None] == seg_id[None, :]).astype(BF16)

    mod3 = _modulation(c, w_ada[l], b_ada[l]).reshape(-1, N_MOD, D_MODEL)

    w = w_in[l]
    w_proj = jnp.concatenate(
        [_reorder_rwkv_cols(w[:, :n_rwkv]),
         w[:, n_rwkv:n_rwkv + N_FOX],
         jnp.pad(w[:, n_rwkv + N_FOX:n_rwkv + N_FOX + HEADS], ((0, 0), (0, LANES - HEADS))),
         w[:, n_rwkv + N_FOX + HEADS:]], axis=1).astype(BF16)
    mu = _reorder_rwkv_cols(mu_shift[l]).reshape(1, n_rwkv)
    f_bias = jnp.pad(fox_f_bias[l], (0, LANES - HEADS)).reshape(1, LANES)
    rkv, small, qk, v_t, cum, cum_t, gates = _projection(
        x, mod3, norm1_g[l].reshape(1, D_MODEL), w_proj, mu, f_bias)

    zeros = jnp.zeros((DECAY_RANK, WIDTH), F32)
    w_lr = jnp.concatenate(
        [jnp.concatenate([w_decay_up[l], zeros], axis=1),
         jnp.concatenate([zeros, w_iclr_up[l]], axis=1)], axis=0).astype(BF16)
    vecs = jnp.stack([decay_base[l], iclr_base[l], kk_scale[l], k_iclr_mix[l],
                      r_bonus[l].reshape(WIDTH), lnx_w[l], lnx_b[l],
                      jnp.zeros((WIDTH,), F32)], axis=0)
    y_a = _rwkv_mix(rkv, small, w_lr, w_gate_up[l].astype(BF16), vecs, seg)

    y_b = _forgetting_attention(qk, v_t, cum, cum_t)

    return _tail(x, y_a, y_b, gates, mod3, norm2_g[l].reshape(1, D_MODEL),
                 final_g.reshape(1, D_MODEL),
                 w_o_rwkv[l].astype(BF16), w_o_fox[l].astype(BF16), w_out[l].astype(BF16),
                 w_ff1[l].astype(BF16), w_ff2[l].astype(BF16))
```

```python
import jax
import jax.numpy as jnp
from jax import lax
from jax.experimental import pallas as pl
from jax.experimental.pallas import tpu as pltpu

F32 = jnp.float32
BF16 = jnp.bfloat16

D_MODEL = 1024
HEAD_DIM = 64
HEADS = 8
WIDTH = HEADS * HEAD_DIM
DECAY_RANK = 64
ICLR_RANK = 64
GATE_RANK = 128
D_FF = 4 * D_MODEL
N_MOD = 6
NORM_EPS = 1e-6
GN_EPS = 64e-5

LANES = 128
PAIR = 2 * HEAD_DIM
N_PAIRS = HEADS // 2

LOG2E = 1.4426950408889634

N_SMALL = DECAY_RANK + ICLR_RANK + GATE_RANK
N_RKV = 3 * WIDTH
N_FOX = 3 * WIDTH
N_GATE = 2 * D_MODEL

PROJ_ROWS = 256
CHUNK = 128
FOX_BLOCK = 512
FOX_SUB = 64
FOX_VROWS = HEAD_DIM + 16
FOX_MASKED = -1e30
TAIL_ROWS = 512
FF_CHUNK = 1024

VMEM_LIMIT_BYTES = 56 * 1024 * 1024


def _dot(a, b):
    return jnp.dot(a.astype(BF16), b.astype(BF16), preferred_element_type=F32)


def _dot_nt(a, b):
    return lax.dot_general(a.astype(BF16), b.astype(BF16), (((1,), (1,)), ((), ())),
                           preferred_element_type=F32)


def _softplus(z):
    return jnp.maximum(z, 0.0) + jnp.log(1.0 + jnp.exp(-jnp.abs(z)))


def _sigmoid(z):
    return 1.0 / (1.0 + jnp.exp(-z))


def _bf16_parts(x, n):
    parts = []
    for _ in range(n):
        p = x.astype(BF16)
        parts.append(p)
        x = x - p.astype(F32)
    return parts


def _cumsum_rows(tri, x, n_parts):
    return sum(jnp.dot(tri, p, preferred_element_type=F32) for p in _bf16_parts(x, n_parts))


def _head_sums(x, seg):
    half = seg.shape[0]
    return jnp.concatenate([_dot(x[:, 0:half], seg), _dot(x[:, half:2 * half], seg)], axis=1)


def _lower_tri(n, dtype):
    row = lax.broadcasted_iota(jnp.int32, (n, n), 0)
    col = lax.broadcasted_iota(jnp.int32, (n, n), 1)
    return (col <= row).astype(dtype)


def _mod_kernel(c_ref, w_ref, b_ref, o_ref):
    c = c_ref[...]
    c_act = c * _sigmoid(c)
    o_ref[...] = jnp.dot(c_act, w_ref[...], preferred_element_type=F32,
                         precision=lax.Precision.HIGHEST) + b_ref[...]


def _modulation(c, w_ada, b_ada):
    batch = c.shape[0]
    n = w_ada.shape[1]
    tn = 1536
    return pl.pallas_call(
        _mod_kernel,
        grid=(n // tn,),
        in_specs=[pl.BlockSpec((batch, D_MODEL), lambda j: (0, 0)),
                  pl.BlockSpec((D_MODEL, tn), lambda j: (0, j)),
                  pl.BlockSpec((1, tn), lambda j: (0, j))],
        out_specs=pl.BlockSpec((batch, tn), lambda j: (0, j)),
        out_shape=jax.ShapeDtypeStruct((batch, n), F32),
        compiler_params=pltpu.CompilerParams(dimension_semantics=("arbitrary",),
                                             vmem_limit_bytes=VMEM_LIMIT_BYTES),
        name="mod",
    )(c, w_ada, b_ada.reshape(1, n))


def _proj_kernel(x_ref, mod_ref, g_ref, wr_ref, wk_ref, wv_ref, ws_ref, wfox_ref, wff_ref,
                 wgate_ref, mu_ref, fb_ref,
                 rkv_ref, small_ref, qk_ref, vt_ref, cum_ref, cumt_ref, gate_ref,
                 shift_ref, carry_ref):
    s = pl.program_id(1)
    rows = x_ref.shape[1]

    @pl.when(s == 0)
    def _():
        shift_ref[0:8, :] = jnp.zeros((8, N_RKV + N_SMALL), F32)
        carry_ref[...] = jnp.zeros_like(carry_ref)

    x = x_ref[0]
    shift1 = mod_ref[0, 0:1, :]
    scale1 = mod_ref[0, 1:2, :]
    inv = lax.rsqrt(jnp.mean(x * x, axis=-1, keepdims=True) + NORM_EPS)
    h = ((x * inv) * g_ref[...] * (1.0 + scale1) + shift1).astype(BF16)

    n_tok = N_RKV + N_SMALL
    p = jnp.concatenate([jnp.dot(h, w[...], preferred_element_type=F32)
                         for w in (wr_ref, wk_ref, wv_ref, ws_ref)], axis=1)
    shift_ref[8:8 + rows, :] = p
    prev = shift_ref[7:7 + rows, :]
    shift_ref[7:8, :] = p[rows - 1:rows, :]
    mixed = p + mu_ref[...] * (prev - p)
    rkv_ref[0] = mixed[:, 0:N_RKV].astype(BF16)
    small_ref[0] = mixed[:, N_RKV:n_tok]

    pf = jnp.dot(h, wfox_ref[...], preferred_element_type=F32)
    qk_ref[0, :, 0:WIDTH] = (pf[:, 0:WIDTH] * (LOG2E * HEAD_DIM ** -0.5)).astype(BF16)
    qk_ref[0, :, WIDTH:2 * WIDTH] = pf[:, WIDTH:2 * WIDTH].astype(BF16)
    vt_ref[0] = jnp.transpose(pf[:, 2 * WIDTH:N_FOX]).astype(BF16)

    ff = jnp.dot(h, wff_ref[...], preferred_element_type=F32)
    logf = -_softplus(-(ff + fb_ref[...]))
    cum = _cumsum_rows(_lower_tri(rows, BF16), logf, 3) + carry_ref[...]
    cum_ref[0] = cum
    cumt_ref[0] = jnp.transpose(cum)[0:HEADS, :]
    carry_ref[...] = cum[rows - 1:rows, :]

    pg = jnp.dot(h, wgate_ref[...], preferred_element_type=F32)
    gate_ref[0] = _sigmoid(pg).astype(BF16)


def _projection(x, mod3, norm_g, weights, mu, f_bias):
    batch, seq, _ = x.shape
    tm = PROJ_ROWS
    grid = (batch, seq // tm)
    const = lambda b, s: (0, 0)
    tile = lambda b, s: (b, s, 0)
    return pl.pallas_call(
        _proj_kernel,
        grid=grid,
        in_specs=[pl.BlockSpec((1, tm, D_MODEL), tile),
                  pl.BlockSpec((1, N_MOD, D_MODEL), lambda b, s: (b, 0, 0)),
                  pl.BlockSpec((1, D_MODEL), const),
                  *[pl.BlockSpec(w.shape, const, pipeline_mode=pl.Buffered(1)) for w in weights],
                  pl.BlockSpec((1, N_RKV + N_SMALL), const),
                  pl.BlockSpec((1, LANES), const)],
        out_specs=[pl.BlockSpec((1, tm, N_RKV), tile),
                   pl.BlockSpec((1, tm, N_SMALL), tile),
                   pl.BlockSpec((1, tm, 2 * WIDTH), tile),
                   pl.BlockSpec((1, WIDTH, tm), lambda b, s: (b, 0, s)),
                   pl.BlockSpec((1, tm, LANES), tile),
                   pl.BlockSpec((1, HEADS, tm), lambda b, s: (b, 0, s)),
                   pl.BlockSpec((1, tm, N_GATE), tile)],
        out_shape=[jax.ShapeDtypeStruct((batch, seq, N_RKV), BF16),
                   jax.ShapeDtypeStruct((batch, seq, N_SMALL), F32),
                   jax.ShapeDtypeStruct((batch, seq, 2 * WIDTH), BF16),
                   jax.ShapeDtypeStruct((batch, WIDTH, seq), BF16),
                   jax.ShapeDtypeStruct((batch, seq, LANES), F32),
                   jax.ShapeDtypeStruct((batch, HEADS, seq), F32),
                   jax.ShapeDtypeStruct((batch, seq, N_GATE), BF16)],
        scratch_shapes=[pltpu.VMEM((tm + 8, N_RKV + N_SMALL), F32),
                        pltpu.VMEM((1, LANES), F32)],
        compiler_params=pltpu.CompilerParams(dimension_semantics=("arbitrary", "arbitrary"),
                                             vmem_limit_bytes=VMEM_LIMIT_BYTES),
        name="proj",
    )(x, mod3, norm_g, *weights, mu, f_bias)


def _rwkv_prepare(rkv, small, w_lr_ref, w_gate_ref, vec_ref, seg_ref, slot, prep):
    am_ref, rm_ref, vb_ref, bt_ref, kt_ref, bh_ref, kh_ref, dend_ref, g_ref, bv_ref = prep
    c = CHUNK
    r = rkv[:, 0:WIDTH].astype(F32)
    k_raw = rkv[:, WIDTH:2 * WIDTH].astype(F32)
    v = rkv[:, 2 * WIDTH:3 * WIDTH].astype(F32)
    decay_base, iclr_base, kk_scale, k_mix, r_bonus = (vec_ref[i:i + 1, :] for i in range(5))

    lane = lax.broadcasted_iota(jnp.int32, (1, LANES), 1)
    lr_in = jnp.where(lane < DECAY_RANK, jnp.tanh(small[:, 0:LANES]), small[:, 0:LANES])
    lr = _dot(lr_in, w_lr_ref[...])
    w_log = -_softplus(-(decay_base + lr[:, 0:WIDTH])) - 0.5
    log_decay = -jnp.exp(w_log)
    a = _sigmoid(iclr_base + lr[:, WIDTH:2 * WIDTH])
    g_ref[slot] = _dot(_sigmoid(small[:, LANES:2 * LANES]), w_gate_ref[...])
    yield

    seg = seg_ref[...]
    kk = k_raw * kk_scale
    kk = kk * lax.rsqrt(jnp.maximum(_head_sums(kk * kk, seg), 1e-24))
    k = k_raw * (1.0 + (a - 1.0) * k_mix)
    b_vec = kk * a
    yield

    cs = _cumsum_rows(_lower_tri(c, BF16), log_decay, 2)
    cs_end = cs[c - 1:c, :]
    am_ref[slot] = (-kk * jnp.exp(cs - log_decay)).astype(BF16)
    rm_ref[slot] = (r * jnp.exp(cs)).astype(BF16)
    vb_ref[slot] = v.astype(BF16)
    yield
    w_inv = jnp.exp(-cs)
    bt_ref[slot] = jnp.transpose(b_vec * w_inv).astype(BF16)
    yield
    kt_ref[slot] = jnp.transpose(k * w_inv).astype(BF16)
    yield
    w_end = jnp.exp(cs_end - cs)
    bh_ref[slot] = jnp.transpose(b_vec * w_end).astype(BF16)
    yield
    kh_ref[slot] = jnp.transpose(k * w_end).astype(BF16)
    dend_ref[slot] = jnp.broadcast_to(jnp.exp(cs_end), (8, WIDTH))
    yield
    bv_ref[slot] = _head_sums(r * k * r_bonus, seg) * v


def _rwkv_chunk(slot, prep, vec_ref, seg_ref, z_ref, out_ref, rows):
    am_ref, rm_ref, vb_ref, bt_ref, kt_ref, bh_ref, kh_ref, dend_ref, g_ref, bv_ref = prep
    c = CHUNK
    ln_w, ln_b = vec_ref[5:6, :], vec_ref[6:7, :]
    lane = lax.broadcasted_iota(jnp.int32, (1, LANES), 1)
    row = lax.broadcasted_iota(jnp.int32, (c, c), 0)
    col = lax.broadcasted_iota(jnp.int32, (c, c), 1)
    strict = col < row
    incl = col <= row
    eye = (col == row).astype(F32)
    same_head = (row // HEAD_DIM) == (col // HEAD_DIM)

    pairs = range(N_PAIRS)
    sls = [slice(p * PAIR, (p + 1) * PAIR) for p in pairs]
    v_b = [vb_ref[slot, :, sl] for sl in sls]
    z_f = [z_ref[p] for p in pairs]
    z_b = [z.astype(BF16) for z in z_f]
    heads = [(p, h) for p in pairs for h in range(2)]
    a_m, r_m, a_ab, a_ak, m_rb, m_rk = {}, {}, {}, {}, {}, {}
    for (p, h) in heads:
        head = (lane // HEAD_DIM) == h
        a_m[p, h] = jnp.where(head, am_ref[slot, :, sls[p]].astype(F32), 0.0).astype(BF16)
        r_m[p, h] = jnp.where(head, rm_ref[slot, :, sls[p]].astype(F32), 0.0).astype(BF16)
        rhs = jnp.concatenate([bt_ref[slot, sls[p], :], kt_ref[slot, sls[p], :]], axis=1)
        big = jnp.dot(jnp.concatenate([a_m[p, h], r_m[p, h]], axis=0), rhs,
                      preferred_element_type=F32)
        a_ab[p, h] = jnp.where(strict, big[0:c, 0:c], 0.0)
        a_ak[p, h] = jnp.where(strict, big[0:c, c:2 * c], 0.0).astype(BF16)
        m_rb[p, h] = jnp.where(incl, big[c:2 * c, 0:c], 0.0).astype(BF16)
        m_rk[p, h] = jnp.where(incl, big[c:2 * c, c:2 * c], 0.0).astype(BF16)
    yield
    t_inv = {hd: eye + a_ab[hd] for hd in heads}
    m_pow = {hd: _dot(a_ab[hd], a_ab[hd]) for hd in heads}
    yield
    for _ in range(5):
        for hd in heads:
            both = _dot(m_pow[hd], jnp.concatenate([m_pow[hd], t_inv[hd]], axis=1))
            m_pow[hd] = both[:, 0:c]
            t_inv[hd] = t_inv[hd] + both[:, c:2 * c]
        yield
    for hd in heads:
        t_inv[hd] = t_inv[hd] + _dot(m_pow[hd], t_inv[hd])
    yield
    ak_v = {(p, h): _dot(a_ak[p, h], v_b[p]) for (p, h) in heads}
    pq = {hd: _dot(t_inv[hd], jnp.concatenate([a_m[hd].astype(F32), ak_v[hd]], axis=1))
          for hd in heads}
    u = {(p, h): _dot(pq[p, h][:, 0:PAIR], z_b[p]) + pq[p, h][:, PAIR:2 * PAIR]
         for (p, h) in heads}
    y_hd = {(p, h): jnp.dot(
        jnp.concatenate([r_m[p, h], m_rb[p, h], m_rk[p, h]], axis=1),
        jnp.concatenate([z_b[p], u[p, h].astype(BF16), v_b[p]], axis=0),
        preferred_element_type=F32) for (p, h) in heads}
    first = lane < HEAD_DIM
    y_pairs = []
    for p in pairs:
        u_p = jnp.where(first, u[p, 0], u[p, 1])
        y_pairs.append(jnp.where(first, y_hd[p, 0], y_hd[p, 1]))
        d_col = jnp.sum(eye * dend_ref[slot, 0:1, sls[p]], axis=1, keepdims=True)
        z_new = d_col * z_f[p] + jnp.dot(
            jnp.concatenate([bh_ref[slot, sls[p], :], kh_ref[slot, sls[p], :]], axis=1),
            jnp.concatenate([u_p.astype(BF16), v_b[p]], axis=0), preferred_element_type=F32)
        z_ref[p] = jnp.where(same_head, z_new, 0.0)

    seg = seg_ref[...]
    y = jnp.concatenate(y_pairs, axis=1)
    mean = _head_sums(y, seg) * (1.0 / HEAD_DIM)
    yc = y - mean
    var = _head_sums(yc * yc, seg) * (1.0 / HEAD_DIM)
    y = yc * lax.rsqrt(var + GN_EPS) * ln_w + ln_b
    out_ref[0, rows, :] = ((y + bv_ref[slot]) * g_ref[slot]).astype(BF16)


def _interleave(*stages):
    live = list(stages)
    while live:
        for gen in list(live):
            try:
                next(gen)
            except StopIteration:
                live.remove(gen)


def _rwkv_kernel(rkv_ref, small_ref, rkv_next_ref, small_next_ref,
                 w_lr_ref, w_gate_ref, vec_ref, seg_ref, out_ref, z_ref, *prep):
    c = CHUNK
    params = (w_lr_ref, w_gate_ref, vec_ref, seg_ref)

    @pl.when(pl.program_id(1) == 0)
    def _():
        z_ref[...] = jnp.zeros_like(z_ref)
        _interleave(_rwkv_prepare(rkv_ref[0, 0:c, :], small_ref[0, 0:c, :], *params, 0, prep))

    _interleave(_rwkv_chunk(0, prep, vec_ref, seg_ref, z_ref, out_ref, slice(0, c)),
                _rwkv_prepare(rkv_ref[0, c:2 * c, :], small_ref[0, c:2 * c, :], *params, 1, prep))
    _interleave(_rwkv_chunk(1, prep, vec_ref, seg_ref, z_ref, out_ref, slice(c, 2 * c)),
                _rwkv_prepare(rkv_next_ref[0], small_next_ref[0], *params, 0, prep))


def _rwkv_mix(rkv, small, w_lr, w_gate, vecs, seg):
    batch, seq, _ = rkv.shape
    c = CHUNK
    n_chunks = seq // c
    grid = (batch, n_chunks // 2)
    const = lambda b, s: (0, 0)
    tile = lambda b, s: (b, s, 0)
    ahead = lambda b, s: (b, jnp.minimum(2 * s + 2, n_chunks - 1), 0)
    slots = lambda shape, dtype: pltpu.VMEM((2,) + shape, dtype)
    return pl.pallas_call(
        _rwkv_kernel,
        grid=grid,
        in_specs=[pl.BlockSpec((1, 2 * c, N_RKV), tile),
                  pl.BlockSpec((1, 2 * c, N_SMALL), tile),
                  pl.BlockSpec((1, c, N_RKV), ahead),
                  pl.BlockSpec((1, c, N_SMALL), ahead),
                  pl.BlockSpec(w_lr.shape, const),
                  pl.BlockSpec(w_gate.shape, const),
                  pl.BlockSpec(vecs.shape, const),
                  pl.BlockSpec(seg.shape, const)],
        out_specs=pl.BlockSpec((1, 2 * c, WIDTH), tile),
        out_shape=jax.ShapeDtypeStruct((batch, seq, WIDTH), BF16),
        scratch_shapes=[pltpu.VMEM((N_PAIRS, PAIR, PAIR), F32),
                        slots((c, WIDTH), BF16),
                        slots((c, WIDTH), BF16),
                        slots((c, WIDTH), BF16),
                        slots((WIDTH, c), BF16),
                        slots((WIDTH, c), BF16),
                        slots((WIDTH, c), BF16),
                        slots((WIDTH, c), BF16),
                        slots((8, WIDTH), F32),
                        slots((c, WIDTH), F32),
                        slots((c, WIDTH), F32)],
        compiler_params=pltpu.CompilerParams(dimension_semantics=("arbitrary", "arbitrary"),
                                             vmem_limit_bytes=VMEM_LIMIT_BYTES),
        name="rwkv",
    )(rkv, small, rkv, small, w_lr, w_gate, vecs, seg)


def _fox_kernel(q_ref, k_ref, vt_ref, cum_ref, cumt_ref, mask_ref, o_ref,
                qx_ref, kx_ref, vx_ref, t_ref, p_ref):
    hp = pl.program_id(1)
    blk = FOX_BLOCK
    seq = k_ref.shape[1]
    n_blk = seq // blk
    lane = lax.broadcasted_iota(jnp.int32, (1, LANES), 1)
    own = [(lane // HEAD_DIM) == h for h in range(2)]
    bias_lane = [HEAD_DIM * (1 - h) for h in range(2)]

    def fill(i, _):
        rows = pl.ds(pl.multiple_of(i * blk, blk), blk)
        cum_rows = cum_ref[0, rows, :]
        k_rows = k_ref[0, rows, :].astype(F32)
        q_rows = q_ref[0, rows, :].astype(F32)
        for h in range(2):
            col = jnp.sum(jnp.where(lane == 2 * hp + h, cum_rows, 0.0), axis=1, keepdims=True)
            parts = _bf16_parts(jnp.broadcast_to(col * (-LOG2E), (blk, LANES)), 3)
            feat = jnp.zeros((blk, LANES), F32)
            for n, part in enumerate(parts):
                feat = jnp.where(lane == bias_lane[h] + n, part.astype(F32), feat)
            kx_ref[h, rows, :] = jnp.where(own[h], k_rows, feat).astype(BF16)
            is_bias = (lane >= bias_lane[h]) & (lane < bias_lane[h] + 3)
            qx_ref[h, rows, :] = jnp.where(own[h], q_rows, is_bias.astype(F32)).astype(BF16)
        return 0
    lax.fori_loop(0, n_blk, fill, 0)
    ones_row = lax.broadcasted_iota(jnp.int32, (FOX_VROWS - HEAD_DIM, seq), 0) == 0
    for h in range(2):
        vx_ref[h, 0:HEAD_DIM, :] = vt_ref[0, h * HEAD_DIM:(h + 1) * HEAD_DIM, :]
        vx_ref[h, HEAD_DIM:FOX_VROWS, :] = ones_row.astype(BF16)

    def rows_of(b):
        return pl.ds(pl.multiple_of(b * blk, blk), blk)

    def following(qi, kj):
        wrap = kj == qi
        return jnp.where(wrap, qi + 1, qi), jnp.where(wrap, 0, kj + 1)

    def scores(pair, slot, out):
        qi, kj = pair
        qi = jnp.minimum(qi, n_blk - 1)
        causal = mask_ref[(kj == qi).astype(jnp.int32)]
        mx = []
        for h in range(2):
            t = _dot_nt(kx_ref[h, rows_of(kj), :], qx_ref[h, rows_of(qi), :]) + causal
            t_ref[slot, h] = t
            c_q = cumt_ref[0, pl.ds(2 * hp + h, 1), rows_of(qi)] * LOG2E
            mx.append(jnp.max(t, axis=0, keepdims=True) + c_q)
            yield
        out["mx"] = tuple(mx)

    def softmax(pair, slot, m, mx, out):
        qi, kj = pair
        sub = FOX_SUB
        q_rows = rows_of(jnp.minimum(qi, n_blk - 1))
        m_out, alpha_out = [], []
        for h in range(2):
            m_old = jnp.where(kj == 0, -jnp.inf, m[h])
            c_q = cumt_ref[0, pl.ds(2 * hp + h, 1), q_rows] * LOG2E
            m_new = jnp.maximum(m_old, mx[h])
            shift = c_q - m_new
            for i in range(blk // sub):
                rows = slice(i * sub, (i + 1) * sub)
                p_ref[slot, h, rows, :] = jnp.exp2(t_ref[slot, h, rows, :] + shift).astype(BF16)
                if i % 2 == 1:
                    yield
            m_out.append(m_new)
            alpha_out.append(jnp.exp2(m_old - m_new))
        out["m"], out["alpha"] = tuple(m_out), tuple(alpha_out)

    def accumulate(pair, slot, alpha, acc, out):
        qi, kj = pair
        new = []
        for h in range(2):
            new.append(alpha[h] * acc[h] + jnp.dot(vx_ref[h, :, rows_of(kj)], p_ref[slot, h],
                                                   preferred_element_type=F32))
            yield
        o_t = jnp.concatenate([a[0:HEAD_DIM] / a[HEAD_DIM:HEAD_DIM + 1] for a in new], axis=0)
        o_ref[0, :, rows_of(qi)] = o_t.astype(BF16)
        out["acc"] = tuple(new)

    def step(pair_s, pair_p, pair_a, slot_s, m, mx, alpha, acc):
        out = {}
        _interleave(softmax(pair_p, 1 - slot_s, m, mx, out),
                    scores(pair_s, slot_s, out),
                    accumulate(pair_a, slot_s, alpha, acc, out))
        return out

    def two_pairs(_, carry):
        pair0, pair1, m, mx1, alpha0, acc = carry
        pair2 = following(*pair1)
        a = step(pair2, pair1, pair0, 0, m, mx1, alpha0, acc)
        pair3 = following(*pair2)
        b = step(pair3, pair2, pair1, 1, a["m"], a["mx"], a["alpha"], a["acc"])
        return pair2, pair3, b["m"], b["mx"], b["alpha"], b["acc"]

    n_pairs = n_blk * (n_blk + 1) // 2
    assert n_pairs % 2 == 0, "two pairs per trip"
    zero = jnp.int32(0)
    pair0 = (zero, zero)
    pair1 = following(*pair0)
    first, second = {}, {}
    _interleave(scores(pair0, 0, first))
    _interleave(scores(pair1, 1, second))
    m = tuple(jnp.full((1, blk), -jnp.inf, F32) for _ in range(2))
    _interleave(softmax(pair0, 0, m, first["mx"], first))
    acc = tuple(jnp.zeros((FOX_VROWS, blk), F32) for _ in range(2))
    lax.fori_loop(0, n_pairs // 2, two_pairs,
                  (pair0, pair1, first["m"], second["mx"], first["alpha"], acc))


def _forgetting_attention(qk, v_t, cum, cum_t):
    batch, seq, _ = qk.shape
    blk = FOX_BLOCK
    key = lax.broadcasted_iota(jnp.int32, (blk, blk), 0)
    qry = lax.broadcasted_iota(jnp.int32, (blk, blk), 1)
    mask = jnp.stack([jnp.zeros((blk, blk), F32), jnp.where(key <= qry, 0.0, FOX_MASKED)])
    whole = lambda b, p: (b, 0, 0)
    return pl.pallas_call(
        _fox_kernel,
        grid=(batch, N_PAIRS),
        in_specs=[pl.BlockSpec((1, seq, PAIR), lambda b, p: (b, 0, p)),
                  pl.BlockSpec((1, seq, PAIR), lambda b, p: (b, 0, N_PAIRS + p)),
                  pl.BlockSpec((1, PAIR, seq), lambda b, p: (b, p, 0)),
                  pl.BlockSpec((1, seq, LANES), whole),
                  pl.BlockSpec((1, HEADS, seq), whole),
                  pl.BlockSpec((2, blk, blk), lambda b, p: (0, 0, 0))],
        out_specs=pl.BlockSpec((1, PAIR, seq), lambda b, p: (b, p, 0)),
        out_shape=jax.ShapeDtypeStruct((batch, WIDTH, seq), BF16),
        scratch_shapes=[pltpu.VMEM((2, seq, LANES), BF16),
                        pltpu.VMEM((2, seq, LANES), BF16),
                        pltpu.VMEM((2, FOX_VROWS, seq), BF16),
                        pltpu.VMEM((2, 2, blk, blk), F32),
                        pltpu.VMEM((2, 2, blk, blk), BF16)],
        compiler_params=pltpu.CompilerParams(
            dimension_semantics=("arbitrary", "arbitrary"),
            vmem_limit_bytes=VMEM_LIMIT_BYTES),
        name="fox",
    )(qk, qk, v_t, cum, cum_t, mask)


def _tail_kernel(x_ref, ya_ref, ybt_ref, gate_ref, mod_ref, g2_ref, gf_ref,
                 woa_ref, wob_ref, wout_ref, w1_ref, w2_ref, o_ref):
    x = x_ref[0]
    gate1 = mod_ref[0, 2:3, :]
    shift2 = mod_ref[0, 3:4, :]
    scale2 = mod_ref[0, 4:5, :]
    gate2 = mod_ref[0, 5:6, :]

    merged = (gate_ref[0, :, 0:D_MODEL].astype(F32)
              * jnp.dot(ya_ref[0], woa_ref[...], preferred_element_type=F32)
              + gate_ref[0, :, D_MODEL:N_GATE].astype(F32)
              * lax.dot_general(ybt_ref[0], wob_ref[...], (((0,), (0,)), ((), ())),
                                preferred_element_type=F32))
    x = x + gate1 * _dot(merged, wout_ref[...])

    inv = lax.rsqrt(jnp.mean(x * x, axis=-1, keepdims=True) + NORM_EPS)
    h2 = ((x * inv) * g2_ref[...] * (1.0 + scale2) + shift2).astype(BF16)
    ff = jnp.zeros_like(x)
    for j in range(D_FF // FF_CHUNK):
        cols = slice(j * FF_CHUNK, (j + 1) * FF_CHUNK)
        hid = jnp.maximum(jnp.dot(h2, w1_ref[:, cols], preferred_element_type=F32), 0.0)
        ff = ff + _dot(hid * hid, w2_ref[cols, :])
    x = x + gate2 * ff

    inv = lax.rsqrt(jnp.mean(x * x, axis=-1, keepdims=True) + NORM_EPS)
    o_ref[0] = (x * inv) * gf_ref[...]


def _tail(x, y_a, y_b, gates, mod3, norm2_g, final_g, w_oa, w_ob, w_out, w_ff1, w_ff2):
    batch, seq, _ = x.shape
    tm = TAIL_ROWS
    grid = (batch, seq // tm)
    const = lambda b, s: (0, 0)
    tile = lambda b, s: (b, s, 0)
    resident = lambda a: pl.BlockSpec(a.shape, const, pipeline_mode=pl.Buffered(1))
    return pl.pallas_call(
        _tail_kernel,
        grid=grid,
        in_specs=[pl.BlockSpec((1, tm, D_MODEL), tile),
                  pl.BlockSpec((1, tm, WIDTH), tile),
                  pl.BlockSpec((1, WIDTH, tm), lambda b, s: (b, 0, s)),
                  pl.BlockSpec((1, tm, N_GATE), tile),
                  pl.BlockSpec((1, N_MOD, D_MODEL), lambda b, s: (b, 0, 0)),
                  pl.BlockSpec((1, D_MODEL), const),
                  pl.BlockSpec((1, D_MODEL), const),
                  resident(w_oa), resident(w_ob), resident(w_out),
                  resident(w_ff1), resident(w_ff2)],
        out_specs=pl.BlockSpec((1, tm, D_MODEL), tile),
        out_shape=jax.ShapeDtypeStruct((batch, seq, D_MODEL), F32),
        compiler_params=pltpu.CompilerParams(dimension_semantics=("arbitrary", "arbitrary"),
                                             vmem_limit_bytes=VMEM_LIMIT_BYTES),
        name="tail",
    )(x, y_a, y_b, gates, mod3, norm2_g, final_g, w_oa, w_ob, w_out, w_ff1, w_ff2)


def _reorder_rwkv_cols(t):
    o = 0
    r = t[..., o:o + WIDTH]; o += WIDTH
    wd = t[..., o:o + DECAY_RANK]; o += DECAY_RANK
    k = t[..., o:o + WIDTH]; o += WIDTH
    v = t[..., o:o + WIDTH]; o += WIDTH
    ad = t[..., o:o + ICLR_RANK]; o += ICLR_RANK
    gd = t[..., o:o + GATE_RANK]
    return jnp.concatenate([r, k, v, wd, ad, gd], axis=-1)


def kernel(x, c, w_ada, b_ada, norm1_g, w_in, mu_shift, w_decay_up, decay_base, w_iclr_up, iclr_base, w_gate_up, kk_scale, k_iclr_mix, r_bonus, lnx_w, lnx_b, fox_f_bias, w_o_rwkv, w_o_fox, w_out, norm2_g, w_ff1, w_ff2, final_g):
    assert w_ada.shape[0] == 1, "the tail kernel fuses the final norm: single layer only"
    l = 0
    n_rwkv = N_RKV + N_SMALL
    seg_id = jnp.arange(WIDTH // 2) // HEAD_DIM
    seg = (seg_id[:, None] == seg_id[None, :]).astype(BF16)

    mod3 = _modulation(c, w_ada[l], b_ada[l]).reshape(-1, N_MOD, D_MODEL)

    w = w_in[l]
    o_wd, o_k, o_v, o_ad = WIDTH, WIDTH + DECAY_RANK, 2 * WIDTH + DECAY_RANK, 3 * WIDTH + DECAY_RANK
    o_ff = n_rwkv + N_FOX
    weights = [w[:, 0:o_wd], w[:, o_k:o_v], w[:, o_v:o_ad],
               jnp.concatenate([w[:, o_wd:o_k], w[:, o_ad:n_rwkv]], axis=1),
               w[:, n_rwkv:o_ff],
               jnp.pad(w[:, o_ff:o_ff + HEADS], ((0, 0), (0, LANES - HEADS))),
               w[:, o_ff + HEADS:]]
    weights = [g.astype(BF16) for g in weights]
    mu = _reorder_rwkv_cols(mu_shift[l]).reshape(1, n_rwkv)
    f_bias = jnp.pad(fox_f_bias[l], (0, LANES - HEADS)).reshape(1, LANES)
    rkv, small, qk, v_t, cum, cum_t, gates = _projection(
        x, mod3, norm1_g[l].reshape(1, D_MODEL), weights, mu, f_bias)

    zeros = jnp.zeros((DECAY_RANK, WIDTH), F32)
    w_lr = jnp.concatenate(
        [jnp.concatenate([w_decay_up[l], zeros], axis=1),
         jnp.concatenate([zeros, w_iclr_up[l]], axis=1)], axis=0).astype(BF16)
    vecs = jnp.stack([decay_base[l], iclr_base[l], kk_scale[l], k_iclr_mix[l],
                      r_bonus[l].reshape(WIDTH), lnx_w[l], lnx_b[l],
                      jnp.zeros((WIDTH,), F32)], axis=0)
    y_a = _rwkv_mix(rkv, small, w_lr, w_gate_up[l].astype(BF16), vecs, seg)

    y_b = _forgetting_attention(qk, v_t, cum, cum_t)

    return _tail(x, y_a, y_b, gates, mod3, norm2_g[l].reshape(1, D_MODEL),
                 final_g.reshape(1, D_MODEL),
                 w_o_rwkv[l].astype(BF16), w_o_fox[l].astype(BF16), w_out[l].astype(BF16),
                 w_ff1[l].astype(BF16), w_ff2[l].astype(BF16))
```

```python
import jax
import jax.numpy as jnp
from jax import lax
from jax.experimental import pallas as pl
from jax.experimental.pallas import tpu as pltpu

F32 = jnp.float32
BF16 = jnp.bfloat16

D_MODEL = 1024
HEAD_DIM = 64
HEADS = 8
WIDTH = HEADS * HEAD_DIM
DECAY_RANK = 64
ICLR_RANK = 64
GATE_RANK = 128
D_FF = 4 * D_MODEL
N_MOD = 6
NORM_EPS = 1e-6
GN_EPS = 64e-5

LANES = 128
PAIR = 2 * HEAD_DIM
N_PAIRS = HEADS // 2

LOG2E = 1.4426950408889634

N_SMALL = DECAY_RANK + ICLR_RANK + GATE_RANK
N_RKV = 3 * WIDTH
N_FOX = 3 * WIDTH
N_GATE = 2 * D_MODEL

PROJ_ROWS = 256
CHUNK = 128
FOX_BLOCK = 512
FOX_SUB = 64
FOX_VROWS = HEAD_DIM + 16
FOX_MASKED = -1e30
TAIL_ROWS = 512
FF_CHUNK = 1024

VMEM_LIMIT_BYTES = 56 * 1024 * 1024


def _dot(a, b):
    return jnp.dot(a.astype(BF16), b.astype(BF16), preferred_element_type=F32)


def _dot_nt(a, b):
    return lax.dot_general(a.astype(BF16), b.astype(BF16), (((1,), (1,)), ((), ())),
                           preferred_element_type=F32)


def _softplus(z):
    return jnp.maximum(z, 0.0) + jnp.log(1.0 + jnp.exp(-jnp.abs(z)))


def _sigmoid(z):
    return 1.0 / (1.0 + jnp.exp(-z))


def _bf16_parts(x, n):
    parts = []
    for _ in range(n):
        p = x.astype(BF16)
        parts.append(p)
        x = x - p.astype(F32)
    return parts


def _cumsum_rows(tri, x, n_parts):
    return sum(jnp.dot(tri, p, preferred_element_type=F32) for p in _bf16_parts(x, n_parts))


def _head_sums(x, seg):
    half = seg.shape[0]
    return jnp.concatenate([_dot(x[:, 0:half], seg), _dot(x[:, half:2 * half], seg)], axis=1)


def _lower_tri(n, dtype):
    row = lax.broadcasted_iota(jnp.int32, (n, n), 0)
    col = lax.broadcasted_iota(jnp.int32, (n, n), 1)
    return (col <= row).astype(dtype)


def _mod_kernel(c_ref, w_ref, b_ref, o_ref):
    c = c_ref[...]
    c_act = c * _sigmoid(c)
    o_ref[...] = jnp.dot(c_act, w_ref[...], preferred_element_type=F32,
                         precision=lax.Precision.HIGHEST) + b_ref[...]


def _modulation(c, w_ada, b_ada):
    batch = c.shape[0]
    n = w_ada.shape[1]
    tn = 1536
    return pl.pallas_call(
        _mod_kernel,
        grid=(n // tn,),
        in_specs=[pl.BlockSpec((batch, D_MODEL), lambda j: (0, 0)),
                  pl.BlockSpec((D_MODEL, tn), lambda j: (0, j)),
                  pl.BlockSpec((1, tn), lambda j: (0, j))],
        out_specs=pl.BlockSpec((batch, tn), lambda j: (0, j)),
        out_shape=jax.ShapeDtypeStruct((batch, n), F32),
        compiler_params=pltpu.CompilerParams(dimension_semantics=("arbitrary",),
                                             vmem_limit_bytes=VMEM_LIMIT_BYTES),
        name="mod",
    )(c, w_ada, b_ada.reshape(1, n))


def _proj_normalise(x, mod_ref, g_ref):
    shift1 = mod_ref[0, 0:1, :]
    scale1 = mod_ref[0, 1:2, :]
    inv = lax.rsqrt(jnp.mean(x * x, axis=-1, keepdims=True) + NORM_EPS)
    return ((x * inv) * (g_ref[...] * (1.0 + scale1)) + shift1).astype(BF16)


def _proj_tile(h, rows, weights, mu_ref, fb_ref, outs, shift_ref, carry_ref):
    wr_ref, wk_ref, wv_ref, ws_ref, wfox_ref, wff_ref, wgate_ref = weights
    rkv_ref, small_ref, qk_ref, vt_ref, cum_ref, cumt_ref, gate_ref = outs
    n = h.shape[0]

    n_tok = N_RKV + N_SMALL
    p = jnp.concatenate([jnp.dot(h, w[...], preferred_element_type=F32)
                         for w in (wr_ref, wk_ref, wv_ref, ws_ref)], axis=1)
    shift_ref[8:8 + n, :] = p
    prev = shift_ref[7:7 + n, :]
    shift_ref[7:8, :] = p[n - 1:n, :]
    mixed = p + mu_ref[...] * (prev - p)
    rkv_ref[0, rows, :] = mixed[:, 0:N_RKV].astype(BF16)
    small_ref[0, rows, :] = mixed[:, N_RKV:n_tok]

    pf = jnp.dot(h, wfox_ref[...], preferred_element_type=F32)
    qk_ref[0, rows, 0:WIDTH] = (pf[:, 0:WIDTH] * (LOG2E * HEAD_DIM ** -0.5)).astype(BF16)
    qk_ref[0, rows, WIDTH:2 * WIDTH] = pf[:, WIDTH:2 * WIDTH].astype(BF16)
    vt_ref[0, :, rows] = jnp.transpose(pf[:, 2 * WIDTH:N_FOX]).astype(BF16)

    ff = jnp.dot(h, wff_ref[...], preferred_element_type=F32)
    logf = -_softplus(-(ff + fb_ref[...]))
    cum = _cumsum_rows(_lower_tri(n, BF16), logf, 3) + carry_ref[...]
    cum_ref[0, rows, :] = cum
    cumt_ref[0, :, rows] = jnp.transpose(cum)[0:HEADS, :]
    carry_ref[...] = cum[n - 1:n, :]

    pg = jnp.dot(h, wgate_ref[...], preferred_element_type=F32)
    gate_ref[0, rows, :] = _sigmoid(pg).astype(BF16)


def _proj_kernel(x_ref, x_next_ref, mod_ref, g_ref, wr_ref, wk_ref, wv_ref, ws_ref, wfox_ref,
                 wff_ref, wgate_ref, mu_ref, fb_ref,
                 rkv_ref, small_ref, qk_ref, vt_ref, cum_ref, cumt_ref, gate_ref,
                 shift_ref, carry_ref, h_ref):
    tm = PROJ_ROWS
    weights = (wr_ref, wk_ref, wv_ref, ws_ref, wfox_ref, wff_ref, wgate_ref)
    outs = (rkv_ref, small_ref, qk_ref, vt_ref, cum_ref, cumt_ref, gate_ref)

    @pl.when(pl.program_id(1) == 0)
    def _():
        shift_ref[0:8, :] = jnp.zeros((8, N_RKV + N_SMALL), F32)
        carry_ref[...] = jnp.zeros_like(carry_ref)
        h_ref[0] = _proj_normalise(x_ref[0, 0:tm, :], mod_ref, g_ref)

    h_ref[1] = _proj_normalise(x_ref[0, tm:2 * tm, :], mod_ref, g_ref)
    _proj_tile(h_ref[0], slice(0, tm), weights, mu_ref, fb_ref, outs, shift_ref, carry_ref)
    h_ref[0] = _proj_normalise(x_next_ref[0], mod_ref, g_ref)
    _proj_tile(h_ref[1], slice(tm, 2 * tm), weights, mu_ref, fb_ref, outs, shift_ref, carry_ref)


def _projection(x, mod3, norm_g, weights, mu, f_bias):
    batch, seq, _ = x.shape
    tm = PROJ_ROWS
    n_tiles = seq // tm
    grid = (batch, n_tiles // 2)
    const = lambda b, s: (0, 0)
    tile = lambda b, s: (b, s, 0)
    lanes = lambda b, s: (b, 0, s)
    ahead = lambda b, s: (b, jnp.minimum(2 * s + 2, n_tiles - 1), 0)
    return pl.pallas_call(
        _proj_kernel,
        grid=grid,
        in_specs=[pl.BlockSpec((1, 2 * tm, D_MODEL), tile),
                  pl.BlockSpec((1, tm, D_MODEL), ahead),
                  pl.BlockSpec((1, N_MOD, D_MODEL), lambda b, s: (b, 0, 0)),
                  pl.BlockSpec((1, D_MODEL), const),
                  *[pl.BlockSpec(w.shape, const, pipeline_mode=pl.Buffered(1)) for w in weights],
                  pl.BlockSpec((1, N_RKV + N_SMALL), const),
                  pl.BlockSpec((1, LANES), const)],
        out_specs=[pl.BlockSpec((1, 2 * tm, N_RKV), tile),
                   pl.BlockSpec((1, 2 * tm, N_SMALL), tile),
                   pl.BlockSpec((1, 2 * tm, 2 * WIDTH), tile),
                   pl.BlockSpec((1, WIDTH, 2 * tm), lanes),
                   pl.BlockSpec((1, 2 * tm, LANES), tile),
                   pl.BlockSpec((1, HEADS, 2 * tm), lanes),
                   pl.BlockSpec((1, 2 * tm, N_GATE), tile)],
        out_shape=[jax.ShapeDtypeStruct((batch, seq, N_RKV), BF16),
                   jax.ShapeDtypeStruct((batch, seq, N_SMALL), F32),
                   jax.ShapeDtypeStruct((batch, seq, 2 * WIDTH), BF16),
                   jax.ShapeDtypeStruct((batch, WIDTH, seq), BF16),
                   jax.ShapeDtypeStruct((batch, seq, LANES), F32),
                   jax.ShapeDtypeStruct((batch, HEADS, seq), F32),
                   jax.ShapeDtypeStruct((batch, seq, N_GATE), BF16)],
        scratch_shapes=[pltpu.VMEM((tm + 8, N_RKV + N_SMALL), F32),
                        pltpu.VMEM((1, LANES), F32),
                        pltpu.VMEM((2, tm, D_MODEL), BF16)],
        compiler_params=pltpu.CompilerParams(dimension_semantics=("arbitrary", "arbitrary"),
                                             vmem_limit_bytes=VMEM_LIMIT_BYTES),
        name="proj",
    )(x, x, mod3, norm_g, *weights, mu, f_bias)


def _rwkv_prepare(rkv, small, w_lr_ref, w_gate_ref, vec_ref, seg_ref, slot, prep):
    am_ref, rm_ref, vb_ref, bt_ref, kt_ref, bh_ref, kh_ref, dend_ref, g_ref, bv_ref = prep
    c = CHUNK
    r = rkv[:, 0:WIDTH].astype(F32)
    k_raw = rkv[:, WIDTH:2 * WIDTH].astype(F32)
    v = rkv[:, 2 * WIDTH:3 * WIDTH].astype(F32)
    decay_base, iclr_base, kk_scale, k_mix, r_bonus = (vec_ref[i:i + 1, :] for i in range(5))

    lane = lax.broadcasted_iota(jnp.int32, (1, LANES), 1)
    lr_in = jnp.where(lane < DECAY_RANK, jnp.tanh(small[:, 0:LANES]), small[:, 0:LANES])
    lr = _dot(lr_in, w_lr_ref[...])
    w_log = -_softplus(-(decay_base + lr[:, 0:WIDTH])) - 0.5
    log_decay = -jnp.exp(w_log)
    a = _sigmoid(iclr_base + lr[:, WIDTH:2 * WIDTH])
    g_ref[slot] = _dot(_sigmoid(small[:, LANES:2 * LANES]), w_gate_ref[...])
    yield

    seg = seg_ref[...]
    kk = k_raw * kk_scale
    kk = kk * lax.rsqrt(jnp.maximum(_head_sums(kk * kk, seg), 1e-24))
    k = k_raw * (1.0 + (a - 1.0) * k_mix)
    b_vec = kk * a
    yield

    cs = _cumsum_rows(_lower_tri(c, BF16), log_decay, 2)
    cs_end = cs[c - 1:c, :]
    am_ref[slot] = (-kk * jnp.exp(cs - log_decay)).astype(BF16)
    rm_ref[slot] = (r * jnp.exp(cs)).astype(BF16)
    vb_ref[slot] = v.astype(BF16)
    yield
    w_inv = jnp.exp(-cs)
    bt_ref[slot] = jnp.transpose(b_vec * w_inv).astype(BF16)
    yield
    kt_ref[slot] = jnp.transpose(k * w_inv).astype(BF16)
    yield
    w_end = jnp.exp(cs_end - cs)
    bh_ref[slot] = jnp.transpose(b_vec * w_end).astype(BF16)
    yield
    kh_ref[slot] = jnp.transpose(k * w_end).astype(BF16)
    dend_ref[slot] = jnp.broadcast_to(jnp.exp(cs_end), (8, WIDTH))
    yield
    bv_ref[slot] = _head_sums(r * k * r_bonus, seg) * v


def _rwkv_chunk(slot, prep, vec_ref, seg_ref, z_ref, out_ref, rows):
    am_ref, rm_ref, vb_ref, bt_ref, kt_ref, bh_ref, kh_ref, dend_ref, g_ref, bv_ref = prep
    c = CHUNK
    ln_w, ln_b = vec_ref[5:6, :], vec_ref[6:7, :]
    lane = lax.broadcasted_iota(jnp.int32, (1, LANES), 1)
    row = lax.broadcasted_iota(jnp.int32, (c, c), 0)
    col = lax.broadcasted_iota(jnp.int32, (c, c), 1)
    strict = col < row
    incl = col <= row
    eye = (col == row).astype(F32)
    same_head = (row // HEAD_DIM) == (col // HEAD_DIM)

    pairs = range(N_PAIRS)
    sls = [slice(p * PAIR, (p + 1) * PAIR) for p in pairs]
    v_b = [vb_ref[slot, :, sl] for sl in sls]
    z_f = [z_ref[p] for p in pairs]
    z_b = [z.astype(BF16) for z in z_f]
    heads = [(p, h) for p in pairs for h in range(2)]
    a_m, r_m, a_ab, a_ak, m_rb, m_rk = {}, {}, {}, {}, {}, {}
    for (p, h) in heads:
        head = (lane // HEAD_DIM) == h
        a_m[p, h] = jnp.where(head, am_ref[slot, :, sls[p]].astype(F32), 0.0).astype(BF16)
        r_m[p, h] = jnp.where(head, rm_ref[slot, :, sls[p]].astype(F32), 0.0).astype(BF16)
        rhs = jnp.concatenate([bt_ref[slot, sls[p], :], kt_ref[slot, sls[p], :]], axis=1)
        big = jnp.dot(jnp.concatenate([a_m[p, h], r_m[p, h]], axis=0), rhs,
                      preferred_element_type=F32)
        a_ab[p, h] = jnp.where(strict, big[0:c, 0:c], 0.0)
        a_ak[p, h] = jnp.where(strict, big[0:c, c:2 * c], 0.0).astype(BF16)
        m_rb[p, h] = jnp.where(incl, big[c:2 * c, 0:c], 0.0).astype(BF16)
        m_rk[p, h] = jnp.where(incl, big[c:2 * c, c:2 * c], 0.0).astype(BF16)
    yield
    t_inv = {hd: eye + a_ab[hd] for hd in heads}
    m_pow = {hd: _dot(a_ab[hd], a_ab[hd]) for hd in heads}
    yield
    for _ in range(5):
        for hd in heads:
            both = _dot(m_pow[hd], jnp.concatenate([m_pow[hd], t_inv[hd]], axis=1))
            m_pow[hd] = both[:, 0:c]
            t_inv[hd] = t_inv[hd] + both[:, c:2 * c]
        yield
    for hd in heads:
        t_inv[hd] = t_inv[hd] + _dot(m_pow[hd], t_inv[hd])
    yield
    ak_v = {(p, h): _dot(a_ak[p, h], v_b[p]) for (p, h) in heads}
    pq = {hd: _dot(t_inv[hd], jnp.concatenate([a_m[hd].astype(F32), ak_v[hd]], axis=1))
          for hd in heads}
    u = {(p, h): _dot(pq[p, h][:, 0:PAIR], z_b[p]) + pq[p, h][:, PAIR:2 * PAIR]
         for (p, h) in heads}
    y_hd = {(p, h): jnp.dot(
        jnp.concatenate([r_m[p, h], m_rb[p, h], m_rk[p, h]], axis=1),
        jnp.concatenate([z_b[p], u[p, h].astype(BF16), v_b[p]], axis=0),
        preferred_element_type=F32) for (p, h) in heads}
    first = lane < HEAD_DIM
    y_pairs = []
    for p in pairs:
        u_p = jnp.where(first, u[p, 0], u[p, 1])
        y_pairs.append(jnp.where(first, y_hd[p, 0], y_hd[p, 1]))
        d_col = jnp.sum(eye * dend_ref[slot, 0:1, sls[p]], axis=1, keepdims=True)
        z_new = d_col * z_f[p] + jnp.dot(
            jnp.concatenate([bh_ref[slot, sls[p], :], kh_ref[slot, sls[p], :]], axis=1),
            jnp.concatenate([u_p.astype(BF16), v_b[p]], axis=0), preferred_element_type=F32)
        z_ref[p] = jnp.where(same_head, z_new, 0.0)

    seg = seg_ref[...]
    y = jnp.concatenate(y_pairs, axis=1)
    mean = _head_sums(y, seg) * (1.0 / HEAD_DIM)
    yc = y - mean
    var = _head_sums(yc * yc, seg) * (1.0 / HEAD_DIM)
    y = yc * lax.rsqrt(var + GN_EPS) * ln_w + ln_b
    out_ref[0, rows, :] = ((y + bv_ref[slot]) * g_ref[slot]).astype(BF16)


def _interleave(*stages):
    live = list(stages)
    while live:
        for gen in list(live):
            try:
                next(gen)
            except StopIteration:
                live.remove(gen)


def _rwkv_kernel(rkv_ref, small_ref, rkv_next_ref, small_next_ref,
                 w_lr_ref, w_gate_ref, vec_ref, seg_ref, out_ref, z_ref, *prep):
    c = CHUNK
    params = (w_lr_ref, w_gate_ref, vec_ref, seg_ref)

    @pl.when(pl.program_id(1) == 0)
    def _():
        z_ref[...] = jnp.zeros_like(z_ref)
        _interleave(_rwkv_prepare(rkv_ref[0, 0:c, :], small_ref[0, 0:c, :], *params, 0, prep))

    _interleave(_rwkv_chunk(0, prep, vec_ref, seg_ref, z_ref, out_ref, slice(0, c)),
                _rwkv_prepare(rkv_ref[0, c:2 * c, :], small_ref[0, c:2 * c, :], *params, 1, prep))
    _interleave(_rwkv_chunk(1, prep, vec_ref, seg_ref, z_ref, out_ref, slice(c, 2 * c)),
                _rwkv_prepare(rkv_next_ref[0], small_next_ref[0], *params, 0, prep))


def _rwkv_mix(rkv, small, w_lr, w_gate, vecs, seg):
    batch, seq, _ = rkv.shape
    c = CHUNK
    n_chunks = seq // c
    grid = (batch, n_chunks // 2)
    const = lambda b, s: (0, 0)
    tile = lambda b, s: (b, s, 0)
    ahead = lambda b, s: (b, jnp.minimum(2 * s + 2, n_chunks - 1), 0)
    slots = lambda shape, dtype: pltpu.VMEM((2,) + shape, dtype)
    return pl.pallas_call(
        _rwkv_kernel,
        grid=grid,
        in_specs=[pl.BlockSpec((1, 2 * c, N_RKV), tile),
                  pl.BlockSpec((1, 2 * c, N_SMALL), tile),
                  pl.BlockSpec((1, c, N_RKV), ahead),
                  pl.BlockSpec((1, c, N_SMALL), ahead),
                  pl.BlockSpec(w_lr.shape, const),
                  pl.BlockSpec(w_gate.shape, const),
                  pl.BlockSpec(vecs.shape, const),
                  pl.BlockSpec(seg.shape, const)],
        out_specs=pl.BlockSpec((1, 2 * c, WIDTH), tile),
        out_shape=jax.ShapeDtypeStruct((batch, seq, WIDTH), BF16),
        scratch_shapes=[pltpu.VMEM((N_PAIRS, PAIR, PAIR), F32),
                        slots((c, WIDTH), BF16),
                        slots((c, WIDTH), BF16),
                        slots((c, WIDTH), BF16),
                        slots((WIDTH, c), BF16),
                        slots((WIDTH, c), BF16),
                        slots((WIDTH, c), BF16),
                        slots((WIDTH, c), BF16),
                        slots((8, WIDTH), F32),
                        slots((c, WIDTH), F32),
                        slots((c, WIDTH), F32)],
        compiler_params=pltpu.CompilerParams(dimension_semantics=("arbitrary", "arbitrary"),
                                             vmem_limit_bytes=VMEM_LIMIT_BYTES),
        name="rwkv",
    )(rkv, small, rkv, small, w_lr, w_gate, vecs, seg)


def _fox_kernel(q_ref, k_ref, vt_ref, cum_ref, cumt_ref, mask_ref, route_ref, o_ref,
                qx_ref, kx_ref, vx_ref, t_ref, p_ref):
    hp = pl.program_id(1)
    blk = FOX_BLOCK
    seq = k_ref.shape[1]
    n_blk = seq // blk
    lane = lax.broadcasted_iota(jnp.int32, (1, LANES), 1)
    own = [(lane // HEAD_DIM) == h for h in range(2)]
    bias_lane = [HEAD_DIM * (1 - h) for h in range(2)]

    def fill(i, _):
        rows = pl.ds(pl.multiple_of(i * blk, blk), blk)
        parts = _bf16_parts(cum_ref[0, rows, :] * (-LOG2E), 3)
        feat = jnp.dot(jnp.concatenate(parts, axis=1), route_ref[0],
                       preferred_element_type=F32)
        k_rows = k_ref[0, rows, :].astype(F32)
        q_rows = q_ref[0, rows, :].astype(F32)
        for h in range(2):
            kx_ref[h, rows, :] = jnp.where(own[h], k_rows, feat).astype(BF16)
            is_bias = (lane >= bias_lane[h]) & (lane < bias_lane[h] + 3)
            qx_ref[h, rows, :] = jnp.where(own[h], q_rows, is_bias.astype(F32)).astype(BF16)
        return 0
    lax.fori_loop(0, n_blk, fill, 0)
    ones_row = lax.broadcasted_iota(jnp.int32, (FOX_VROWS - HEAD_DIM, seq), 0) == 0
    for h in range(2):
        vx_ref[h, 0:HEAD_DIM, :] = vt_ref[0, h * HEAD_DIM:(h + 1) * HEAD_DIM, :]
        vx_ref[h, HEAD_DIM:FOX_VROWS, :] = ones_row.astype(BF16)

    def rows_of(b):
        return pl.ds(pl.multiple_of(b * blk, blk), blk)

    def following(qi, kj):
        wrap = kj == qi
        return jnp.where(wrap, qi + 1, qi), jnp.where(wrap, 0, kj + 1)

    def scores(pair, slot, out):
        qi, kj = pair
        qi = jnp.minimum(qi, n_blk - 1)
        causal = mask_ref[(kj == qi).astype(jnp.int32)]
        mx = []
        for h in range(2):
            t = _dot_nt(kx_ref[h, rows_of(kj), :], qx_ref[h, rows_of(qi), :]) + causal
            t_ref[slot, h] = t
            c_q = cumt_ref[0, pl.ds(2 * hp + h, 1), rows_of(qi)] * LOG2E
            mx.append(jnp.max(t, axis=0, keepdims=True) + c_q)
            yield
        out["mx"] = tuple(mx)

    def softmax(pair, slot, m, mx, out):
        qi, kj = pair
        sub = FOX_SUB
        q_rows = rows_of(jnp.minimum(qi, n_blk - 1))
        m_out, alpha_out = [], []
        for h in range(2):
            m_old = jnp.where(kj == 0, -jnp.inf, m[h])
            c_q = cumt_ref[0, pl.ds(2 * hp + h, 1), q_rows] * LOG2E
            m_new = jnp.maximum(m_old, mx[h])
            shift = c_q - m_new
            for i in range(blk // sub):
                rows = slice(i * sub, (i + 1) * sub)
                p_ref[slot, h, rows, :] = jnp.exp2(t_ref[slot, h, rows, :] + shift).astype(BF16)
                if i % 2 == 1:
                    yield
            m_out.append(m_new)
            alpha_out.append(jnp.exp2(m_old - m_new))
        out["m"], out["alpha"] = tuple(m_out), tuple(alpha_out)

    def accumulate(pair, slot, alpha, acc, out):
        qi, kj = pair
        new = []
        for h in range(2):
            new.append(alpha[h] * acc[h] + jnp.dot(vx_ref[h, :, rows_of(kj)], p_ref[slot, h],
                                                   preferred_element_type=F32))
            yield
        o_t = jnp.concatenate([a[0:HEAD_DIM] / a[HEAD_DIM:HEAD_DIM + 1] for a in new], axis=0)
        o_ref[0, :, rows_of(qi)] = o_t.astype(BF16)
        out["acc"] = tuple(new)

    def step(pair_s, pair_p, pair_a, slot_s, m, mx, alpha, acc):
        out = {}
        _interleave(softmax(pair_p, 1 - slot_s, m, mx, out),
                    scores(pair_s, slot_s, out),
                    accumulate(pair_a, slot_s, alpha, acc, out))
        return out

    def two_pairs(_, carry):
        pair0, pair1, m, mx1, alpha0, acc = carry
        pair2 = following(*pair1)
        a = step(pair2, pair1, pair0, 0, m, mx1, alpha0, acc)
        pair3 = following(*pair2)
        b = step(pair3, pair2, pair1, 1, a["m"], a["mx"], a["alpha"], a["acc"])
        return pair2, pair3, b["m"], b["mx"], b["alpha"], b["acc"]

    n_pairs = n_blk * (n_blk + 1) // 2
    assert n_pairs % 2 == 0, "two pairs per trip"
    zero = jnp.int32(0)
    pair0 = (zero, zero)
    pair1 = following(*pair0)
    first, second = {}, {}
    _interleave(scores(pair0, 0, first))
    _interleave(scores(pair1, 1, second))
    m = tuple(jnp.full((1, blk), -jnp.inf, F32) for _ in range(2))
    _interleave(softmax(pair0, 0, m, first["mx"], first))
    acc = tuple(jnp.zeros((FOX_VROWS, blk), F32) for _ in range(2))
    lax.fori_loop(0, n_pairs // 2, two_pairs,
                  (pair0, pair1, first["m"], second["mx"], first["alpha"], acc))


def _forgetting_attention(qk, v_t, cum, cum_t):
    batch, seq, _ = qk.shape
    blk = FOX_BLOCK
    key = lax.broadcasted_iota(jnp.int32, (blk, blk), 0)
    qry = lax.broadcasted_iota(jnp.int32, (blk, blk), 1)
    mask = jnp.stack([jnp.zeros((blk, blk), F32), jnp.where(key <= qry, 0.0, FOX_MASKED)])
    pair = jnp.arange(N_PAIRS)[:, None, None]
    src = jnp.arange(3 * LANES)[None, :, None]
    dst = jnp.arange(LANES)[None, None, :]
    route = jnp.zeros((N_PAIRS, 3 * LANES, LANES), jnp.bool_)
    for h in range(2):
        for n in range(3):
            route |= (src == n * LANES + 2 * pair + h) & (dst == HEAD_DIM * (1 - h) + n)
    route = route.astype(BF16)
    whole = lambda b, p: (b, 0, 0)
    return pl.pallas_call(
        _fox_kernel,
        grid=(batch, N_PAIRS),
        in_specs=[pl.BlockSpec((1, seq, PAIR), lambda b, p: (b, 0, p)),
                  pl.BlockSpec((1, seq, PAIR), lambda b, p: (b, 0, N_PAIRS + p)),
                  pl.BlockSpec((1, PAIR, seq), lambda b, p: (b, p, 0)),
                  pl.BlockSpec((1, seq, LANES), whole),
                  pl.BlockSpec((1, HEADS, seq), whole),
                  pl.BlockSpec((2, blk, blk), lambda b, p: (0, 0, 0)),
                  pl.BlockSpec((1, 3 * LANES, LANES), lambda b, p: (p, 0, 0))],
        out_specs=pl.BlockSpec((1, PAIR, seq), lambda b, p: (b, p, 0)),
        out_shape=jax.ShapeDtypeStruct((batch, WIDTH, seq), BF16),
        scratch_shapes=[pltpu.VMEM((2, seq, LANES), BF16),
                        pltpu.VMEM((2, seq, LANES), BF16),
                        pltpu.VMEM((2, FOX_VROWS, seq), BF16),
                        pltpu.VMEM((2, 2, blk, blk), F32),
                        pltpu.VMEM((2, 2, blk, blk), BF16)],
        compiler_params=pltpu.CompilerParams(
            dimension_semantics=("arbitrary", "arbitrary"),
            vmem_limit_bytes=VMEM_LIMIT_BYTES),
        name="fox",
    )(qk, qk, v_t, cum, cum_t, mask, route)


def _tail_kernel(x_ref, ya_ref, ybt_ref, gate_ref, mod_ref, g2_ref, gf_ref,
                 woa_ref, wob_ref, wout_ref, w1_ref, w2_ref, o_ref):
    x = x_ref[0]
    gate1 = mod_ref[0, 2:3, :]
    shift2 = mod_ref[0, 3:4, :]
    scale2 = mod_ref[0, 4:5, :]
    gate2 = mod_ref[0, 5:6, :]

    merged = (gate_ref[0, :, 0:D_MODEL].astype(F32)
              * jnp.dot(ya_ref[0], woa_ref[...], preferred_element_type=F32)
              + gate_ref[0, :, D_MODEL:N_GATE].astype(F32)
              * lax.dot_general(ybt_ref[0], wob_ref[...], (((0,), (0,)), ((), ())),
                                preferred_element_type=F32))
    x = x + gate1 * _dot(merged, wout_ref[...])

    inv = lax.rsqrt(jnp.mean(x * x, axis=-1, keepdims=True) + NORM_EPS)
    h2 = ((x * inv) * g2_ref[...] * (1.0 + scale2) + shift2).astype(BF16)
    ff = jnp.zeros_like(x)
    for j in range(D_FF // FF_CHUNK):
        cols = slice(j * FF_CHUNK, (j + 1) * FF_CHUNK)
        hid = jnp.maximum(jnp.dot(h2, w1_ref[:, cols], preferred_element_type=F32), 0.0)
        ff = ff + _dot(hid * hid, w2_ref[cols, :])
    x = x + gate2 * ff

    inv = lax.rsqrt(jnp.mean(x * x, axis=-1, keepdims=True) + NORM_EPS)
    o_ref[0] = (x * inv) * gf_ref[...]


def _tail(x, y_a, y_b, gates, mod3, norm2_g, final_g, w_oa, w_ob, w_out, w_ff1, w_ff2):
    batch, seq, _ = x.shape
    tm = TAIL_ROWS
    grid = (batch, seq // tm)
    const = lambda b, s: (0, 0)
    tile = lambda b, s: (b, s, 0)
    resident = lambda a: pl.BlockSpec(a.shape, const, pipeline_mode=pl.Buffered(1))
    return pl.pallas_call(
        _tail_kernel,
        grid=grid,
        in_specs=[pl.BlockSpec((1, tm, D_MODEL), tile),
                  pl.BlockSpec((1, tm, WIDTH), tile),
                  pl.BlockSpec((1, WIDTH, tm), lambda b, s: (b, 0, s)),
                  pl.BlockSpec((1, tm, N_GATE), tile),
                  pl.BlockSpec((1, N_MOD, D_MODEL), lambda b, s: (b, 0, 0)),
                  pl.BlockSpec((1, D_MODEL), const),
                  pl.BlockSpec((1, D_MODEL), const),
                  resident(w_oa), resident(w_ob), resident(w_out),
                  resident(w_ff1), resident(w_ff2)],
        out_specs=pl.BlockSpec((1, tm, D_MODEL), tile),
        out_shape=jax.ShapeDtypeStruct((batch, seq, D_MODEL), F32),
        compiler_params=pltpu.CompilerParams(dimension_semantics=("arbitrary", "arbitrary"),
                                             vmem_limit_bytes=VMEM_LIMIT_BYTES),
        name="tail",
    )(x, y_a, y_b, gates, mod3, norm2_g, final_g, w_oa, w_ob, w_out, w_ff1, w_ff2)


def _reorder_rwkv_cols(t):
    o = 0
    r = t[..., o:o + WIDTH]; o += WIDTH
    wd = t[..., o:o + DECAY_RANK]; o += DECAY_RANK
    k = t[..., o:o + WIDTH]; o += WIDTH
    v = t[..., o:o + WIDTH]; o += WIDTH
    ad = t[..., o:o + ICLR_RANK]; o += ICLR_RANK
    gd = t[..., o:o + GATE_RANK]
    return jnp.concatenate([r, k, v, wd, ad, gd], axis=-1)


def kernel(x, c, w_ada, b_ada, norm1_g, w_in, mu_shift, w_decay_up, decay_base, w_iclr_up, iclr_base, w_gate_up, kk_scale, k_iclr_mix, r_bonus, lnx_w, lnx_b, fox_f_bias, w_o_rwkv, w_o_fox, w_out, norm2_g, w_ff1, w_ff2, final_g):
    assert w_ada.shape[0] == 1, "the tail kernel fuses the final norm: single layer only"
    l = 0
    n_rwkv = N_RKV + N_SMALL
    seg_id = jnp.arange(WIDTH // 2) // HEAD_DIM
    seg = (seg_id[:, None] == seg_id[None, :]).astype(BF16)

    mod3 = _modulation(c, w_ada[l], b_ada[l]).reshape(-1, N_MOD, D_MODEL)

    w = w_in[l]
    o_wd, o_k, o_v, o_ad = WIDTH, WIDTH + DECAY_RANK, 2 * WIDTH + DECAY_RANK, 3 * WIDTH + DECAY_RANK
    o_ff = n_rwkv + N_FOX
    weights = [w[:, 0:o_wd], w[:, o_k:o_v], w[:, o_v:o_ad],
               jnp.concatenate([w[:, o_wd:o_k], w[:, o_ad:n_rwkv]], axis=1),
               w[:, n_rwkv:o_ff],
               jnp.pad(w[:, o_ff:o_ff + HEADS], ((0, 0), (0, LANES - HEADS))),
               w[:, o_ff + HEADS:]]
    weights = [g.astype(BF16) for g in weights]
    mu = _reorder_rwkv_cols(mu_shift[l]).reshape(1, n_rwkv)
    f_bias = jnp.pad(fox_f_bias[l], (0, LANES - HEADS)).reshape(1, LANES)
    rkv, small, qk, v_t, cum, cum_t, gates = _projection(
        x, mod3, norm1_g[l].reshape(1, D_MODEL), weights, mu, f_bias)

    zeros = jnp.zeros((DECAY_RANK, WIDTH), F32)
    w_lr = jnp.concatenate(
        [jnp.concatenate([w_decay_up[l], zeros], axis=1),
         jnp.concatenate([zeros, w_iclr_up[l]], axis=1)], axis=0).astype(BF16)
    vecs = jnp.stack([decay_base[l], iclr_base[l], kk_scale[l], k_iclr_mix[l],
                      r_bonus[l].reshape(WIDTH), lnx_w[l], lnx_b[l],
                      jnp.zeros((WIDTH,), F32)], axis=0)
    y_a = _rwkv_mix(rkv, small, w_lr, w_gate_up[l].astype(BF16), vecs, seg)

    y_b = _forgetting_attention(qk, v_t, cum, cum_t)

    return _tail(x, y_a, y_b, gates, mod3, norm2_g[l].reshape(1, D_MODEL),
                 final_g.reshape(1, D_MODEL),
                 w_o_rwkv[l].astype(BF16), w_o_fox[l].astype(BF16), w_out[l].astype(BF16),
                 w_ff1[l].astype(BF16), w_ff2[l].astype(BF16))
```

```python
import jax
import jax.numpy as jnp
from jax import lax
from jax.experimental import pallas as pl
from jax.experimental.pallas import tpu as pltpu

F32 = jnp.float32
BF16 = jnp.bfloat16

D_MODEL = 1024
HEAD_DIM = 64
HEADS = 8
WIDTH = HEADS * HEAD_DIM
DECAY_RANK = 64
ICLR_RANK = 64
GATE_RANK = 128
D_FF = 4 * D_MODEL
N_MOD = 6
NORM_EPS = 1e-6
GN_EPS = 64e-5

LANES = 128
PAIR = 2 * HEAD_DIM
N_PAIRS = HEADS // 2

LOG2E = 1.4426950408889634

N_SMALL = DECAY_RANK + ICLR_RANK + GATE_RANK
N_RKV = 3 * WIDTH
N_FOX = 3 * WIDTH
N_GATE = 2 * D_MODEL

PROJ_ROWS = 256
CHUNK = 128
RWKV_STEP_CHUNKS = 4
FOX_BLOCK = 512
FOX_SUB = 64
FOX_VROWS = HEAD_DIM + 16
FOX_MASKED = -1e30
TAIL_ROWS = 512
FF_CHUNK = 1024

VMEM_LIMIT_BYTES = 56 * 1024 * 1024


def _dot(a, b):
    return jnp.dot(a.astype(BF16), b.astype(BF16), preferred_element_type=F32)


def _dot_nt(a, b):
    return lax.dot_general(a.astype(BF16), b.astype(BF16), (((1,), (1,)), ((), ())),
                           preferred_element_type=F32)


def _softplus(z):
    return jnp.maximum(z, 0.0) + jnp.log(1.0 + jnp.exp(-jnp.abs(z)))


def _sigmoid(z):
    return 1.0 / (1.0 + jnp.exp(-z))


def _bf16_parts(x, n):
    parts = []
    for _ in range(n):
        p = x.astype(BF16)
        parts.append(p)
        x = x - p.astype(F32)
    return parts


def _cumsum_rows(tri, x, n_parts):
    return sum(jnp.dot(tri, p, preferred_element_type=F32) for p in _bf16_parts(x, n_parts))


def _head_sums(x, seg):
    half = seg.shape[0]
    return jnp.concatenate([_dot(x[:, 0:half], seg), _dot(x[:, half:2 * half], seg)], axis=1)


def _lower_tri(n, dtype):
    row = lax.broadcasted_iota(jnp.int32, (n, n), 0)
    col = lax.broadcasted_iota(jnp.int32, (n, n), 1)
    return (col <= row).astype(dtype)


def _mod_kernel(c_ref, w_ref, b_ref, o_ref):
    c = c_ref[...]
    c_act = c * _sigmoid(c)
    o_ref[...] = jnp.dot(c_act, w_ref[...], preferred_element_type=F32,
                         precision=lax.Precision.HIGHEST) + b_ref[...]


def _modulation(c, w_ada, b_ada):
    batch = c.shape[0]
    n = w_ada.shape[1]
    tn = 1536
    return pl.pallas_call(
        _mod_kernel,
        grid=(n // tn,),
        in_specs=[pl.BlockSpec((batch, D_MODEL), lambda j: (0, 0)),
                  pl.BlockSpec((D_MODEL, tn), lambda j: (0, j)),
                  pl.BlockSpec((1, tn), lambda j: (0, j))],
        out_specs=pl.BlockSpec((batch, tn), lambda j: (0, j)),
        out_shape=jax.ShapeDtypeStruct((batch, n), F32),
        compiler_params=pltpu.CompilerParams(dimension_semantics=("arbitrary",),
                                             vmem_limit_bytes=VMEM_LIMIT_BYTES),
        name="mod",
    )(c, w_ada, b_ada.reshape(1, n))


def _proj_normalise(x, mod_ref, g_ref):
    shift1 = mod_ref[0, 0:1, :]
    scale1 = mod_ref[0, 1:2, :]
    inv = lax.rsqrt(jnp.mean(x * x, axis=-1, keepdims=True) + NORM_EPS)
    return ((x * inv) * (g_ref[...] * (1.0 + scale1)) + shift1).astype(BF16)


def _proj_tile(h, rows, weights, mu_ref, fb_ref, outs, shift_ref, carry_ref):
    wr_ref, wk_ref, wv_ref, ws_ref, wfox_ref, wff_ref, wgate_ref = weights
    rkv_ref, small_ref, qk_ref, vt_ref, cum_ref, cumt_ref, gate_ref = outs
    n = h.shape[0]

    n_tok = N_RKV + N_SMALL
    p = jnp.concatenate([jnp.dot(h, w[...], preferred_element_type=F32)
                         for w in (wr_ref, wk_ref, wv_ref, ws_ref)], axis=1)
    shift_ref[8:8 + n, :] = p
    prev = shift_ref[7:7 + n, :]
    shift_ref[7:8, :] = p[n - 1:n, :]
    mixed = p + mu_ref[...] * (prev - p)
    rkv_ref[0, rows, :] = mixed[:, 0:N_RKV].astype(BF16)
    small_ref[0, rows, :] = mixed[:, N_RKV:n_tok]

    pf = jnp.dot(h, wfox_ref[...], preferred_element_type=F32)
    qk_ref[0, rows, 0:WIDTH] = (pf[:, 0:WIDTH] * (LOG2E * HEAD_DIM ** -0.5)).astype(BF16)
    qk_ref[0, rows, WIDTH:2 * WIDTH] = pf[:, WIDTH:2 * WIDTH].astype(BF16)
    vt_ref[0, :, rows] = jnp.transpose(pf[:, 2 * WIDTH:N_FOX]).astype(BF16)

    ff = jnp.dot(h, wff_ref[...], preferred_element_type=F32)
    logf = -_softplus(-(ff + fb_ref[...]))
    cum = _cumsum_rows(_lower_tri(n, BF16), logf, 3) + carry_ref[...]
    cum_ref[0, rows, :] = cum
    cumt_ref[0, :, rows] = jnp.transpose(cum)[0:HEADS, :]
    carry_ref[...] = cum[n - 1:n, :]

    pg = jnp.dot(h, wgate_ref[...], preferred_element_type=F32)
    gate_ref[0, rows, :] = _sigmoid(pg).astype(BF16)


def _proj_kernel(x_ref, x_next_ref, mod_ref, g_ref, wr_ref, wk_ref, wv_ref, ws_ref, wfox_ref,
                 wff_ref, wgate_ref, mu_ref, fb_ref,
                 rkv_ref, small_ref, qk_ref, vt_ref, cum_ref, cumt_ref, gate_ref,
                 shift_ref, carry_ref, h_ref):
    tm = PROJ_ROWS
    weights = (wr_ref, wk_ref, wv_ref, ws_ref, wfox_ref, wff_ref, wgate_ref)
    outs = (rkv_ref, small_ref, qk_ref, vt_ref, cum_ref, cumt_ref, gate_ref)

    @pl.when(pl.program_id(1) == 0)
    def _():
        shift_ref[0:8, :] = jnp.zeros((8, N_RKV + N_SMALL), F32)
        carry_ref[...] = jnp.zeros_like(carry_ref)
        h_ref[0] = _proj_normalise(x_ref[0, 0:tm, :], mod_ref, g_ref)

    h_ref[1] = _proj_normalise(x_ref[0, tm:2 * tm, :], mod_ref, g_ref)
    _proj_tile(h_ref[0], slice(0, tm), weights, mu_ref, fb_ref, outs, shift_ref, carry_ref)
    h_ref[0] = _proj_normalise(x_next_ref[0], mod_ref, g_ref)
    _proj_tile(h_ref[1], slice(tm, 2 * tm), weights, mu_ref, fb_ref, outs, shift_ref, carry_ref)


def _projection(x, mod3, norm_g, weights, mu, f_bias):
    batch, seq, _ = x.shape
    tm = PROJ_ROWS
    n_tiles = seq // tm
    grid = (batch, n_tiles // 2)
    const = lambda b, s: (0, 0)
    tile = lambda b, s: (b, s, 0)
    lanes = lambda b, s: (b, 0, s)
    ahead = lambda b, s: (b, jnp.minimum(2 * s + 2, n_tiles - 1), 0)
    return pl.pallas_call(
        _proj_kernel,
        grid=grid,
        in_specs=[pl.BlockSpec((1, 2 * tm, D_MODEL), tile),
                  pl.BlockSpec((1, tm, D_MODEL), ahead),
                  pl.BlockSpec((1, N_MOD, D_MODEL), lambda b, s: (b, 0, 0)),
                  pl.BlockSpec((1, D_MODEL), const),
                  *[pl.BlockSpec(w.shape, const, pipeline_mode=pl.Buffered(1)) for w in weights],
                  pl.BlockSpec((1, N_RKV + N_SMALL), const),
                  pl.BlockSpec((1, LANES), const)],
        out_specs=[pl.BlockSpec((1, 2 * tm, N_RKV), tile),
                   pl.BlockSpec((1, 2 * tm, N_SMALL), tile),
                   pl.BlockSpec((1, 2 * tm, 2 * WIDTH), tile),
                   pl.BlockSpec((1, WIDTH, 2 * tm), lanes),
                   pl.BlockSpec((1, 2 * tm, LANES), tile),
                   pl.BlockSpec((1, HEADS, 2 * tm), lanes),
                   pl.BlockSpec((1, 2 * tm, N_GATE), tile)],
        out_shape=[jax.ShapeDtypeStruct((batch, seq, N_RKV), BF16),
                   jax.ShapeDtypeStruct((batch, seq, N_SMALL), F32),
                   jax.ShapeDtypeStruct((batch, seq, 2 * WIDTH), BF16),
                   jax.ShapeDtypeStruct((batch, WIDTH, seq), BF16),
                   jax.ShapeDtypeStruct((batch, seq, LANES), F32),
                   jax.ShapeDtypeStruct((batch, HEADS, seq), F32),
                   jax.ShapeDtypeStruct((batch, seq, N_GATE), BF16)],
        scratch_shapes=[pltpu.VMEM((tm + 8, N_RKV + N_SMALL), F32),
                        pltpu.VMEM((1, LANES), F32),
                        pltpu.VMEM((2, tm, D_MODEL), BF16)],
        compiler_params=pltpu.CompilerParams(dimension_semantics=("arbitrary", "arbitrary"),
                                             vmem_limit_bytes=VMEM_LIMIT_BYTES),
        name="proj",
    )(x, x, mod3, norm_g, *weights, mu, f_bias)


def _rwkv_prepare(rkv, small, w_lr_ref, w_gate_ref, vec_ref, seg_ref, slot, prep):
    am_ref, rm_ref, vb_ref, bt_ref, kt_ref, bh_ref, kh_ref, dend_ref, g_ref, bv_ref = prep
    c = CHUNK
    r = rkv[:, 0:WIDTH].astype(F32)
    k_raw = rkv[:, WIDTH:2 * WIDTH].astype(F32)
    v = rkv[:, 2 * WIDTH:3 * WIDTH].astype(F32)
    decay_base, iclr_base, kk_scale, k_mix, r_bonus = (vec_ref[i:i + 1, :] for i in range(5))

    lane = lax.broadcasted_iota(jnp.int32, (1, LANES), 1)
    lr_in = jnp.where(lane < DECAY_RANK, jnp.tanh(small[:, 0:LANES]), small[:, 0:LANES])
    lr = _dot(lr_in, w_lr_ref[...])
    w_log = -_softplus(-(decay_base + lr[:, 0:WIDTH])) - 0.5
    log_decay = -jnp.exp(w_log)
    a = _sigmoid(iclr_base + lr[:, WIDTH:2 * WIDTH])
    g_ref[slot] = _dot(_sigmoid(small[:, LANES:2 * LANES]), w_gate_ref[...])
    yield

    seg = seg_ref[...]
    kk = k_raw * kk_scale
    kk = kk * lax.rsqrt(jnp.maximum(_head_sums(kk * kk, seg), 1e-24))
    k = k_raw * (1.0 + (a - 1.0) * k_mix)
    b_vec = kk * a
    yield

    cs = _cumsum_rows(_lower_tri(c, BF16), log_decay, 2)
    cs_end = cs[c - 1:c, :]
    am_ref[slot] = (-kk * jnp.exp(cs - log_decay)).astype(BF16)
    rm_ref[slot] = (r * jnp.exp(cs)).astype(BF16)
    vb_ref[slot] = v.astype(BF16)
    yield
    w_inv = jnp.exp(-cs)
    bt_ref[slot] = jnp.transpose(b_vec * w_inv).astype(BF16)
    yield
    kt_ref[slot] = jnp.transpose(k * w_inv).astype(BF16)
    yield
    w_end = jnp.exp(cs_end - cs)
    bh_ref[slot] = jnp.transpose(b_vec * w_end).astype(BF16)
    yield
    kh_ref[slot] = jnp.transpose(k * w_end).astype(BF16)
    dend_ref[slot] = jnp.broadcast_to(jnp.exp(cs_end), (8, WIDTH))
    yield
    bv_ref[slot] = _head_sums(r * k * r_bonus, seg) * v


def _rwkv_chunk_matrices(slot, prep, mid):
    am_ref, rm_ref, vb_ref, bt_ref, kt_ref = prep[:5]
    pq_ref, lhs_ref = mid
    c = CHUNK
    lane = lax.broadcasted_iota(jnp.int32, (1, LANES), 1)
    row = lax.broadcasted_iota(jnp.int32, (c, c), 0)
    col = lax.broadcasted_iota(jnp.int32, (c, c), 1)
    strict = col < row
    incl = col <= row
    eye = (col == row).astype(F32)

    sls = [slice(p * PAIR, (p + 1) * PAIR) for p in range(N_PAIRS)]
    heads = [(p, h) for p in range(N_PAIRS) for h in range(2)]
    a_m, a_ab, a_ak = {}, {}, {}
    for i, (p, h) in enumerate(heads):
        head = (lane // HEAD_DIM) == h
        a_m[p, h] = jnp.where(head, am_ref[slot, :, sls[p]].astype(F32), 0.0).astype(BF16)
        r_m = jnp.where(head, rm_ref[slot, :, sls[p]].astype(F32), 0.0).astype(BF16)
        rhs = jnp.concatenate([bt_ref[slot, sls[p], :], kt_ref[slot, sls[p], :]], axis=1)
        big = jnp.dot(jnp.concatenate([a_m[p, h], r_m], axis=0), rhs,
                      preferred_element_type=F32)
        a_ab[p, h] = jnp.where(strict, big[0:c, 0:c], 0.0)
        a_ak[p, h] = jnp.where(strict, big[0:c, c:2 * c], 0.0).astype(BF16)
        lhs_ref[slot, i] = jnp.concatenate(
            [r_m, jnp.where(incl, big[c:2 * c, 0:c], 0.0).astype(BF16),
             jnp.where(incl, big[c:2 * c, c:2 * c], 0.0).astype(BF16)], axis=1)
    yield
    t_inv = {hd: eye + a_ab[hd] for hd in heads}
    m_pow = {hd: _dot(a_ab[hd], a_ab[hd]) for hd in heads}
    yield
    for k in range(1, 7):
        skip = (2 ** k) // 16 * 16
        last = k == 6
        for hd in heads:
            rhs = t_inv[hd] if last else jnp.concatenate([m_pow[hd], t_inv[hd]], axis=1)
            upd = _dot(m_pow[hd][skip:c, :], rhs)
            if skip:
                upd = jnp.concatenate([jnp.zeros((skip, upd.shape[1]), F32), upd], axis=0)
            if last:
                t_inv[hd] = t_inv[hd] + upd
            else:
                m_pow[hd] = upd[:, 0:c]
                t_inv[hd] = t_inv[hd] + upd[:, c:2 * c]
        yield
    ak_v = {(p, h): _dot(a_ak[p, h], vb_ref[slot, :, sls[p]]) for (p, h) in heads}
    yield
    for i, hd in enumerate(heads):
        pq_ref[slot, i] = _dot(t_inv[hd],
                               jnp.concatenate([a_m[hd].astype(F32), ak_v[hd]], axis=1))


def _rwkv_chunk_state(slot, prep, mid, vec_ref, seg_ref, z_ref, out_ref, rows):
    vb_ref, bh_ref, kh_ref, dend_ref, g_ref, bv_ref = (prep[i] for i in (2, 5, 6, 7, 8, 9))
    pq_ref, lhs_ref = mid
    c = CHUNK
    ln_w, ln_b = vec_ref[5:6, :], vec_ref[6:7, :]
    lane = lax.broadcasted_iota(jnp.int32, (1, LANES), 1)
    row = lax.broadcasted_iota(jnp.int32, (c, c), 0)
    col = lax.broadcasted_iota(jnp.int32, (c, c), 1)
    eye = (col == row).astype(F32)
    same_head = (row // HEAD_DIM) == (col // HEAD_DIM)
    pairs = range(N_PAIRS)
    sls = [slice(p * PAIR, (p + 1) * PAIR) for p in pairs]
    heads = [(p, h) for p in pairs for h in range(2)]
    v_b = [vb_ref[slot, :, sl] for sl in sls]
    z_f = [z_ref[p] for p in pairs]
    z_b = [z.astype(BF16) for z in z_f]

    u = {(p, h): _dot(pq_ref[slot, i, :, 0:PAIR], z_b[p]) + pq_ref[slot, i, :, PAIR:2 * PAIR]
         for i, (p, h) in enumerate(heads)}
    yield
    y_hd = {(p, h): jnp.dot(
        lhs_ref[slot, i], jnp.concatenate([z_b[p], u[p, h].astype(BF16), v_b[p]], axis=0),
        preferred_element_type=F32) for i, (p, h) in enumerate(heads)}
    first = lane < HEAD_DIM
    y_pairs = []
    for p in pairs:
        u_p = jnp.where(first, u[p, 0], u[p, 1])
        y_pairs.append(jnp.where(first, y_hd[p, 0], y_hd[p, 1]))
        d_col = jnp.sum(eye * dend_ref[slot, 0:1, sls[p]], axis=1, keepdims=True)
        z_new = d_col * z_f[p] + jnp.dot(
            jnp.concatenate([bh_ref[slot, sls[p], :], kh_ref[slot, sls[p], :]], axis=1),
            jnp.concatenate([u_p.astype(BF16), v_b[p]], axis=0), preferred_element_type=F32)
        z_ref[p] = jnp.where(same_head, z_new, 0.0)
    yield

    seg = seg_ref[...]
    y = jnp.concatenate(y_pairs, axis=1)
    mean = _head_sums(y, seg) * (1.0 / HEAD_DIM)
    yc = y - mean
    var = _head_sums(yc * yc, seg) * (1.0 / HEAD_DIM)
    y = yc * lax.rsqrt(var + GN_EPS) * ln_w + ln_b
    out_ref[0, rows, :] = ((y + bv_ref[slot]) * g_ref[slot]).astype(BF16)


def _interleave(*stages, late=()):
    live = list(stages)
    late = list(late)
    while live:
        for gen in list(live):
            try:
                next(gen)
            except StopIteration:
                live.remove(gen)
                if late and gen is stages[0]:
                    live.extend(late)
                    late = []


def _in_turn(*stages):
    for gen in stages:
        yield from gen


def _rwkv_kernel(rkv_ref, small_ref, rkv_next_ref, small_next_ref,
                 w_lr_ref, w_gate_ref, vec_ref, seg_ref, out_ref, z_ref, pq_ref, lhs_ref, *prep):
    c = CHUNK
    params = (w_lr_ref, w_gate_ref, vec_ref, seg_ref)
    mid = (pq_ref, lhs_ref)
    rows = [slice(i * c, (i + 1) * c) for i in range(RWKV_STEP_CHUNKS)]

    def prepare(r_ref, s_ref, i, slot):
        return _rwkv_prepare(r_ref[0, rows[i], :], s_ref[0, rows[i], :], *params, slot, prep)

    def matrices(slot):
        return _rwkv_chunk_matrices(slot, prep, mid)

    def state(slot):
        return _rwkv_chunk_state(slot, prep, mid, vec_ref, seg_ref, z_ref, out_ref, rows[slot])

    @pl.when(pl.program_id(1) == 0)
    def _():
        z_ref[...] = jnp.zeros_like(z_ref)
        _interleave(prepare(rkv_ref, small_ref, 0, 0), prepare(rkv_ref, small_ref, 1, 1))

    _interleave(matrices(0), matrices(1),
                prepare(rkv_ref, small_ref, 2, 2), prepare(rkv_ref, small_ref, 3, 3))
    _interleave(_in_turn(state(0), state(1)), matrices(2), matrices(3),
                late=(prepare(rkv_next_ref, small_next_ref, 0, 0),
                      prepare(rkv_next_ref, small_next_ref, 1, 1)))
    _interleave(_in_turn(state(2), state(3)))


def _rwkv_mix(rkv, small, w_lr, w_gate, vecs, seg):
    batch, seq, _ = rkv.shape
    c = CHUNK
    n = RWKV_STEP_CHUNKS
    n_half = seq // (n // 2 * c)
    grid = (batch, seq // (n * c))
    const = lambda b, s: (0, 0)
    tile = lambda b, s: (b, s, 0)
    ahead = lambda b, s: (b, jnp.minimum(2 * s + 2, n_half - 1), 0)
    slots = lambda shape, dtype: pltpu.VMEM((n,) + shape, dtype)
    return pl.pallas_call(
        _rwkv_kernel,
        grid=grid,
        in_specs=[pl.BlockSpec((1, n * c, N_RKV), tile),
                  pl.BlockSpec((1, n * c, N_SMALL), tile),
                  pl.BlockSpec((1, n // 2 * c, N_RKV), ahead),
                  pl.BlockSpec((1, n // 2 * c, N_SMALL), ahead),
                  pl.BlockSpec(w_lr.shape, const),
                  pl.BlockSpec(w_gate.shape, const),
                  pl.BlockSpec(vecs.shape, const),
                  pl.BlockSpec(seg.shape, const)],
        out_specs=pl.BlockSpec((1, n * c, WIDTH), tile),
        out_shape=jax.ShapeDtypeStruct((batch, seq, WIDTH), BF16),
        scratch_shapes=[pltpu.VMEM((N_PAIRS, PAIR, PAIR), F32),
                        slots((HEADS, c, 2 * PAIR), F32),
                        slots((HEADS, c, 3 * PAIR), BF16),
                        slots((c, WIDTH), BF16),
                        slots((c, WIDTH), BF16),
                        slots((c, WIDTH), BF16),
                        slots((WIDTH, c), BF16),
                        slots((WIDTH, c), BF16),
                        slots((WIDTH, c), BF16),
                        slots((WIDTH, c), BF16),
                        slots((8, WIDTH), F32),
                        slots((c, WIDTH), F32),
                        slots((c, WIDTH), F32)],
        compiler_params=pltpu.CompilerParams(dimension_semantics=("arbitrary", "arbitrary"),
                                             vmem_limit_bytes=VMEM_LIMIT_BYTES),
        name="rwkv",
    )(rkv, small, rkv, small, w_lr, w_gate, vecs, seg)


def _fox_kernel(q_ref, k_ref, vt_ref, cum_ref, cumt_ref, mask_ref, route_ref, o_ref,
                qx_ref, kx_ref, vx_ref, t_ref, p_ref):
    hp = pl.program_id(1)
    blk = FOX_BLOCK
    seq = k_ref.shape[1]
    n_blk = seq // blk
    lane = lax.broadcasted_iota(jnp.int32, (1, LANES), 1)
    own = [(lane // HEAD_DIM) == h for h in range(2)]
    bias_lane = [HEAD_DIM * (1 - h) for h in range(2)]

    def fill(i, _):
        rows = pl.ds(pl.multiple_of(i * blk, blk), blk)
        parts = _bf16_parts(cum_ref[0, rows, :] * (-LOG2E), 3)
        feat = jnp.dot(jnp.concatenate(parts, axis=1), route_ref[0],
                       preferred_element_type=F32)
        k_rows = k_ref[0, rows, :].astype(F32)
        q_rows = q_ref[0, rows, :].astype(F32)
        for h in range(2):
            kx_ref[h, rows, :] = jnp.where(own[h], k_rows, feat).astype(BF16)
            is_bias = (lane >= bias_lane[h]) & (lane < bias_lane[h] + 3)
            qx_ref[h, rows, :] = jnp.where(own[h], q_rows, is_bias.astype(F32)).astype(BF16)
        return 0
    lax.fori_loop(0, n_blk, fill, 0)
    ones_row = lax.broadcasted_iota(jnp.int32, (FOX_VROWS - HEAD_DIM, seq), 0) == 0
    for h in range(2):
        vx_ref[h, 0:HEAD_DIM, :] = vt_ref[0, h * HEAD_DIM:(h + 1) * HEAD_DIM, :]
        vx_ref[h, HEAD_DIM:FOX_VROWS, :] = ones_row.astype(BF16)

    def rows_of(b):
        return pl.ds(pl.multiple_of(b * blk, blk), blk)

    def following(qi, kj):
        wrap = kj == qi
        return jnp.where(wrap, qi + 1, qi), jnp.where(wrap, 0, kj + 1)

    def scores(pair, slot, out):
        qi, kj = pair
        qi = jnp.minimum(qi, n_blk - 1)
        causal = mask_ref[(kj == qi).astype(jnp.int32)]
        mx = []
        for h in range(2):
            t = _dot_nt(kx_ref[h, rows_of(kj), :], qx_ref[h, rows_of(qi), :]) + causal
            t_ref[slot, h] = t
            c_q = cumt_ref[0, pl.ds(2 * hp + h, 1), rows_of(qi)] * LOG2E
            mx.append(jnp.max(t, axis=0, keepdims=True) + c_q)
            yield
        out["mx"] = tuple(mx)

    def softmax(pair, slot, m, mx, out):
        qi, kj = pair
        sub = FOX_SUB
        q_rows = rows_of(jnp.minimum(qi, n_blk - 1))
        m_out, alpha_out = [], []
        for h in range(2):
            m_old = jnp.where(kj == 0, -jnp.inf, m[h])
            c_q = cumt_ref[0, pl.ds(2 * hp + h, 1), q_rows] * LOG2E
            m_new = jnp.maximum(m_old, mx[h])
            shift = c_q - m_new
            for i in range(blk // sub):
                rows = slice(i * sub, (i + 1) * sub)
                p_ref[slot, h, rows, :] = jnp.exp2(t_ref[slot, h, rows, :] + shift).astype(BF16)
                if i % 2 == 1:
                    yield
            m_out.append(m_new)
            alpha_out.append(jnp.exp2(m_old - m_new))
        out["m"], out["alpha"] = tuple(m_out), tuple(alpha_out)

    def accumulate(pair, slot, alpha, acc, out):
        qi, kj = pair
        new = []
        for h in range(2):
            new.append(alpha[h] * acc[h] + jnp.dot(vx_ref[h, :, rows_of(kj)], p_ref[slot, h],
                                                   preferred_element_type=F32))
            yield
        o_t = jnp.concatenate([a[0:HEAD_DIM] / a[HEAD_DIM:HEAD_DIM + 1] for a in new], axis=0)
        o_ref[0, :, rows_of(qi)] = o_t.astype(BF16)
        out["acc"] = tuple(new)

    def step(pair_s, pair_p, pair_a, slot_s, m, mx, alpha, acc):
        out = {}
        _interleave(softmax(pair_p, 1 - slot_s, m, mx, out),
                    scores(pair_s, slot_s, out),
                    accumulate(pair_a, slot_s, alpha, acc, out))
        return out

    def two_pairs(_, carry):
        pair0, pair1, m, mx1, alpha0, acc = carry
        pair2 = following(*pair1)
        a = step(pair2, pair1, pair0, 0, m, mx1, alpha0, acc)
        pair3 = following(*pair2)
        b = step(pair3, pair2, pair1, 1, a["m"], a["mx"], a["alpha"], a["acc"])
        return pair2, pair3, b["m"], b["mx"], b["alpha"], b["acc"]

    n_pairs = n_blk * (n_blk + 1) // 2
    assert n_pairs % 2 == 0, "two pairs per trip"
    zero = jnp.int32(0)
    pair0 = (zero, zero)
    pair1 = following(*pair0)
    first, second = {}, {}
    _interleave(scores(pair0, 0, first))
    _interleave(scores(pair1, 1, second))
    m = tuple(jnp.full((1, blk), -jnp.inf, F32) for _ in range(2))
    _interleave(softmax(pair0, 0, m, first["mx"], first))
    acc = tuple(jnp.zeros((FOX_VROWS, blk), F32) for _ in range(2))
    lax.fori_loop(0, n_pairs // 2, two_pairs,
                  (pair0, pair1, first["m"], second["mx"], first["alpha"], acc))


def _forgetting_attention(qk, v_t, cum, cum_t):
    batch, seq, _ = qk.shape
    blk = FOX_BLOCK
    key = lax.broadcasted_iota(jnp.int32, (blk, blk), 0)
    qry = lax.broadcasted_iota(jnp.int32, (blk, blk), 1)
    mask = jnp.stack([jnp.zeros((blk, blk), F32), jnp.where(key <= qry, 0.0, FOX_MASKED)])
    pair = jnp.arange(N_PAIRS)[:, None, None]
    src = jnp.arange(3 * LANES)[None, :, None]
    dst = jnp.arange(LANES)[None, None, :]
    route = jnp.zeros((N_PAIRS, 3 * LANES, LANES), jnp.bool_)
    for h in range(2):
        for n in range(3):
            route |= (src == n * LANES + 2 * pair + h) & (dst == HEAD_DIM * (1 - h) + n)
    route = route.astype(BF16)
    whole = lambda b, p: (b, 0, 0)
    return pl.pallas_call(
        _fox_kernel,
        grid=(batch, N_PAIRS),
        in_specs=[pl.BlockSpec((1, seq, PAIR), lambda b, p: (b, 0, p)),
                  pl.BlockSpec((1, seq, PAIR), lambda b, p: (b, 0, N_PAIRS + p)),
                  pl.BlockSpec((1, PAIR, seq), lambda b, p: (b, p, 0)),
                  pl.BlockSpec((1, seq, LANES), whole),
                  pl.BlockSpec((1, HEADS, seq), whole),
                  pl.BlockSpec((2, blk, blk), lambda b, p: (0, 0, 0)),
                  pl.BlockSpec((1, 3 * LANES, LANES), lambda b, p: (p, 0, 0))],
        out_specs=pl.BlockSpec((1, PAIR, seq), lambda b, p: (b, p, 0)),
        out_shape=jax.ShapeDtypeStruct((batch, WIDTH, seq), BF16),
        scratch_shapes=[pltpu.VMEM((2, seq, LANES), BF16),
                        pltpu.VMEM((2, seq, LANES), BF16),
                        pltpu.VMEM((2, FOX_VROWS, seq), BF16),
                        pltpu.VMEM((2, 2, blk, blk), F32),
                        pltpu.VMEM((2, 2, blk, blk), BF16)],
        compiler_params=pltpu.CompilerParams(
            dimension_semantics=("arbitrary", "arbitrary"),
            vmem_limit_bytes=VMEM_LIMIT_BYTES),
        name="fox",
    )(qk, qk, v_t, cum, cum_t, mask, route)


def _tail_kernel(x_ref, ya_ref, ybt_ref, gate_ref, mod_ref, g2_ref, gf_ref,
                 woa_ref, wob_ref, wout_ref, w1_ref, w2_ref, o_ref):
    x = x_ref[0]
    gate1 = mod_ref[0, 2:3, :]
    shift2 = mod_ref[0, 3:4, :]
    scale2 = mod_ref[0, 4:5, :]
    gate2 = mod_ref[0, 5:6, :]

    merged = (gate_ref[0, :, 0:D_MODEL].astype(F32)
              * jnp.dot(ya_ref[0], woa_ref[...], preferred_element_type=F32)
              + gate_ref[0, :, D_MODEL:N_GATE].astype(F32)
              * lax.dot_general(ybt_ref[0], wob_ref[...], (((0,), (0,)), ((), ())),
                                preferred_element_type=F32))
    x = x + gate1 * _dot(merged, wout_ref[...])

    inv = lax.rsqrt(jnp.mean(x * x, axis=-1, keepdims=True) + NORM_EPS)
    h2 = ((x * inv) * g2_ref[...] * (1.0 + scale2) + shift2).astype(BF16)
    ff = jnp.zeros_like(x)
    for j in range(D_FF // FF_CHUNK):
        cols = slice(j * FF_CHUNK, (j + 1) * FF_CHUNK)
        hid = jnp.maximum(jnp.dot(h2, w1_ref[:, cols], preferred_element_type=F32), 0.0)
        ff = ff + _dot(hid * hid, w2_ref[cols, :])
    x = x + gate2 * ff

    inv = lax.rsqrt(jnp.mean(x * x, axis=-1, keepdims=True) + NORM_EPS)
    o_ref[0] = (x * inv) * gf_ref[...]


def _tail(x, y_a, y_b, gates, mod3, norm2_g, final_g, w_oa, w_ob, w_out, w_ff1, w_ff2):
    batch, seq, _ = x.shape
    tm = TAIL_ROWS
    grid = (batch, seq // tm)
    const = lambda b, s: (0, 0)
    tile = lambda b, s: (b, s, 0)
    resident = lambda a: pl.BlockSpec(a.shape, const, pipeline_mode=pl.Buffered(1))
    return pl.pallas_call(
        _tail_kernel,
        grid=grid,
        in_specs=[pl.BlockSpec((1, tm, D_MODEL), tile),
                  pl.BlockSpec((1, tm, WIDTH), tile),
                  pl.BlockSpec((1, WIDTH, tm), lambda b, s: (b, 0, s)),
                  pl.BlockSpec((1, tm, N_GATE), tile),
                  pl.BlockSpec((1, N_MOD, D_MODEL), lambda b, s: (b, 0, 0)),
                  pl.BlockSpec((1, D_MODEL), const),
                  pl.BlockSpec((1, D_MODEL), const),
                  resident(w_oa), resident(w_ob), resident(w_out),
                  resident(w_ff1), resident(w_ff2)],
        out_specs=pl.BlockSpec((1, tm, D_MODEL), tile),
        out_shape=jax.ShapeDtypeStruct((batch, seq, D_MODEL), F32),
        compiler_params=pltpu.CompilerParams(dimension_semantics=("arbitrary", "arbitrary"),
                                             vmem_limit_bytes=VMEM_LIMIT_BYTES),
        name="tail",
    )(x, y_a, y_b, gates, mod3, norm2_g, final_g, w_oa, w_ob, w_out, w_ff1, w_ff2)


def _reorder_rwkv_cols(t):
    o = 0
    r = t[..., o:o + WIDTH]; o += WIDTH
    wd = t[..., o:o + DECAY_RANK]; o += DECAY_RANK
    k = t[..., o:o + WIDTH]; o += WIDTH
    v = t[..., o:o + WIDTH]; o += WIDTH
    ad = t[..., o:o + ICLR_RANK]; o += ICLR_RANK
    gd = t[..., o:o + GATE_RANK]
    return jnp.concatenate([r, k, v, wd, ad, gd], axis=-1)


def kernel(x, c, w_ada, b_ada, norm1_g, w_in, mu_shift, w_decay_up, decay_base, w_iclr_up, iclr_base, w_gate_up, kk_scale, k_iclr_mix, r_bonus, lnx_w, lnx_b, fox_f_bias, w_o_rwkv, w_o_fox, w_out, norm2_g, w_ff1, w_ff2, final_g):
    assert w_ada.shape[0] == 1, "the tail kernel fuses the final norm: single layer only"
    l = 0
    n_rwkv = N_RKV + N_SMALL
    seg_id = jnp.arange(WIDTH // 2) // HEAD_DIM
    seg = (seg_id[:, None] == seg_id[None, :]).astype(BF16)

    mod3 = _modulation(c, w_ada[l], b_ada[l]).reshape(-1, N_MOD, D_MODEL)

    w = w_in[l]
    o_wd, o_k, o_v, o_ad = WIDTH, WIDTH + DECAY_RANK, 2 * WIDTH + DECAY_RANK, 3 * WIDTH + DECAY_RANK
    o_ff = n_rwkv + N_FOX
    weights = [w[:, 0:o_wd], w[:, o_k:o_v], w[:, o_v:o_ad],
               jnp.concatenate([w[:, o_wd:o_k], w[:, o_ad:n_rwkv]], axis=1),
               w[:, n_rwkv:o_ff],
               jnp.pad(w[:, o_ff:o_ff + HEADS], ((0, 0), (0, LANES - HEADS))),
               w[:, o_ff + HEADS:]]
    weights = [g.astype(BF16) for g in weights]
    mu = _reorder_rwkv_cols(mu_shift[l]).reshape(1, n_rwkv)
    f_bias = jnp.pad(fox_f_bias[l], (0, LANES - HEADS)).reshape(1, LANES)
    rkv, small, qk, v_t, cum, cum_t, gates = _projection(
        x, mod3, norm1_g[l].reshape(1, D_MODEL), weights, mu, f_bias)

    zeros = jnp.zeros((DECAY_RANK, WIDTH), F32)
    w_lr = jnp.concatenate(
        [jnp.concatenate([w_decay_up[l], zeros], axis=1),
         jnp.concatenate([zeros, w_iclr_up[l]], axis=1)], axis=0).astype(BF16)
    vecs = jnp.stack([decay_base[l], iclr_base[l], kk_scale[l], k_iclr_mix[l],
                      r_bonus[l].reshape(WIDTH), lnx_w[l], lnx_b[l],
                      jnp.zeros((WIDTH,), F32)], axis=0)
    y_a = _rwkv_mix(rkv, small, w_lr, w_gate_up[l].astype(BF16), vecs, seg)

    y_b = _forgetting_attention(qk, v_t, cum, cum_t)

    return _tail(x, y_a, y_b, gates, mod3, norm2_g[l].reshape(1, D_MODEL),
                 final_g.reshape(1, D_MODEL),
                 w_o_rwkv[l].astype(BF16), w_o_fox[l].astype(BF16), w_out[l].astype(BF16),
                 w_ff1[l].astype(BF16), w_ff2[l].astype(BF16))
```

```python
import jax
import jax.numpy as jnp
from jax import lax
from jax.experimental import pallas as pl
from jax.experimental.pallas import tpu as pltpu

F32 = jnp.float32
BF16 = jnp.bfloat16

D_MODEL = 1024
HEAD_DIM = 64
HEADS = 8
WIDTH = HEADS * HEAD_DIM
DECAY_RANK = 64
ICLR_RANK = 64
GATE_RANK = 128
D_FF = 4 * D_MODEL
N_MOD = 6
NORM_EPS = 1e-6
GN_EPS = 64e-5

LANES = 128
PAIR = 2 * HEAD_DIM
N_PAIRS = HEADS // 2

LOG2E = 1.4426950408889634

N_SMALL = DECAY_RANK + ICLR_RANK + GATE_RANK
N_RKV = 3 * WIDTH
N_FOX = 3 * WIDTH
N_GATE = 2 * D_MODEL

PROJ_ROWS = 256
CHUNK = 128
RWKV_STEP_CHUNKS = 4
FOX_BLOCK = 512
FOX_SUB = 64
FOX_VROWS = HEAD_DIM + 16
FOX_STEP_HEADS = 4
FOX_MASKED = -1e30
TAIL_ROWS = 512
FF_CHUNK = 1024

VMEM_LIMIT_BYTES = 56 * 1024 * 1024


def _dot(a, b):
    return jnp.dot(a.astype(BF16), b.astype(BF16), preferred_element_type=F32)


def _dot_nt(a, b):
    return lax.dot_general(a.astype(BF16), b.astype(BF16), (((1,), (1,)), ((), ())),
                           preferred_element_type=F32)


def _softplus(z):
    return jnp.maximum(z, 0.0) + jnp.log(1.0 + jnp.exp(-jnp.abs(z)))


def _sigmoid(z):
    return 1.0 / (1.0 + jnp.exp(-z))


def _bf16_parts(x, n):
    parts = []
    for _ in range(n):
        p = x.astype(BF16)
        parts.append(p)
        x = x - p.astype(F32)
    return parts


def _cumsum_rows(tri, x, n_parts):
    return sum(jnp.dot(tri, p, preferred_element_type=F32) for p in _bf16_parts(x, n_parts))


def _head_sums(x, seg):
    half = seg.shape[0]
    return jnp.concatenate([_dot(x[:, 0:half], seg), _dot(x[:, half:2 * half], seg)], axis=1)


def _lower_tri(n, dtype):
    row = lax.broadcasted_iota(jnp.int32, (n, n), 0)
    col = lax.broadcasted_iota(jnp.int32, (n, n), 1)
    return (col <= row).astype(dtype)


def _mod_kernel(c_ref, w_ref, b_ref, o_ref):
    c = c_ref[...]
    c_act = c * _sigmoid(c)
    o_ref[...] = jnp.dot(c_act, w_ref[...], preferred_element_type=F32,
                         precision=lax.Precision.HIGHEST) + b_ref[...]


def _modulation(c, w_ada, b_ada):
    batch = c.shape[0]
    n = w_ada.shape[1]
    tn = 1536
    return pl.pallas_call(
        _mod_kernel,
        grid=(n // tn,),
        in_specs=[pl.BlockSpec((batch, D_MODEL), lambda j: (0, 0)),
                  pl.BlockSpec((D_MODEL, tn), lambda j: (0, j)),
                  pl.BlockSpec((1, tn), lambda j: (0, j))],
        out_specs=pl.BlockSpec((batch, tn), lambda j: (0, j)),
        out_shape=jax.ShapeDtypeStruct((batch, n), F32),
        compiler_params=pltpu.CompilerParams(dimension_semantics=("arbitrary",),
                                             vmem_limit_bytes=VMEM_LIMIT_BYTES),
        name="mod",
    )(c, w_ada, b_ada.reshape(1, n))


def _proj_normalise(x, mod_ref, g_ref):
    shift1 = mod_ref[0, 0:1, :]
    scale1 = mod_ref[0, 1:2, :]
    inv = lax.rsqrt(jnp.mean(x * x, axis=-1, keepdims=True) + NORM_EPS)
    return ((x * inv) * (g_ref[...] * (1.0 + scale1)) + shift1).astype(BF16)


def _proj_tile(h, rows, weights, mu_ref, fb_ref, outs, shift_ref, carry_ref):
    wr_ref, wk_ref, wv_ref, ws_ref, wfox_ref, wff_ref, wgate_ref = weights
    rkv_ref, small_ref, qk_ref, vt_ref, cum_ref, cumt_ref, gate_ref = outs
    n = h.shape[0]

    n_tok = N_RKV + N_SMALL
    p = jnp.concatenate([jnp.dot(h, w[...], preferred_element_type=F32)
                         for w in (wr_ref, wk_ref, wv_ref, ws_ref)], axis=1)
    shift_ref[8:8 + n, :] = p
    prev = shift_ref[7:7 + n, :]
    shift_ref[7:8, :] = p[n - 1:n, :]
    mixed = p + mu_ref[...] * (prev - p)
    rkv_ref[0, rows, :] = mixed[:, 0:N_RKV].astype(BF16)
    small_ref[0, rows, :] = mixed[:, N_RKV:n_tok]

    pf = jnp.dot(h, wfox_ref[...], preferred_element_type=F32)
    qk_ref[0, rows, 0:WIDTH] = (pf[:, 0:WIDTH] * (LOG2E * HEAD_DIM ** -0.5)).astype(BF16)
    qk_ref[0, rows, WIDTH:2 * WIDTH] = pf[:, WIDTH:2 * WIDTH].astype(BF16)
    vt_ref[0, :, rows] = jnp.transpose(pf[:, 2 * WIDTH:N_FOX]).astype(BF16)

    ff = jnp.dot(h, wff_ref[...], preferred_element_type=F32)
    logf = -_softplus(-(ff + fb_ref[...]))
    cum = _cumsum_rows(_lower_tri(n, BF16), logf, 3) + carry_ref[...]
    cum_ref[0, rows, :] = cum
    cumt_ref[0, :, rows] = jnp.transpose(cum)[0:HEADS, :]
    carry_ref[...] = cum[n - 1:n, :]

    pg = jnp.dot(h, wgate_ref[...], preferred_element_type=F32)
    gate_ref[0, rows, :] = _sigmoid(pg).astype(BF16)


def _proj_kernel(x_ref, x_next_ref, mod_ref, g_ref, wr_ref, wk_ref, wv_ref, ws_ref, wfox_ref,
                 wff_ref, wgate_ref, mu_ref, fb_ref,
                 rkv_ref, small_ref, qk_ref, vt_ref, cum_ref, cumt_ref, gate_ref,
                 shift_ref, carry_ref, h_ref):
    tm = PROJ_ROWS
    weights = (wr_ref, wk_ref, wv_ref, ws_ref, wfox_ref, wff_ref, wgate_ref)
    outs = (rkv_ref, small_ref, qk_ref, vt_ref, cum_ref, cumt_ref, gate_ref)

    @pl.when(pl.program_id(1) == 0)
    def _():
        shift_ref[0:8, :] = jnp.zeros((8, N_RKV + N_SMALL), F32)
        carry_ref[...] = jnp.zeros_like(carry_ref)
        h_ref[0] = _proj_normalise(x_ref[0, 0:tm, :], mod_ref, g_ref)

    h_ref[1] = _proj_normalise(x_ref[0, tm:2 * tm, :], mod_ref, g_ref)
    _proj_tile(h_ref[0], slice(0, tm), weights, mu_ref, fb_ref, outs, shift_ref, carry_ref)
    h_ref[0] = _proj_normalise(x_next_ref[0], mod_ref, g_ref)
    _proj_tile(h_ref[1], slice(tm, 2 * tm), weights, mu_ref, fb_ref, outs, shift_ref, carry_ref)


def _projection(x, mod3, norm_g, weights, mu, f_bias):
    batch, seq, _ = x.shape
    tm = PROJ_ROWS
    n_tiles = seq // tm
    grid = (batch, n_tiles // 2)
    const = lambda b, s: (0, 0)
    tile = lambda b, s: (b, s, 0)
    lanes = lambda b, s: (b, 0, s)
    ahead = lambda b, s: (b, jnp.minimum(2 * s + 2, n_tiles - 1), 0)
    return pl.pallas_call(
        _proj_kernel,
        grid=grid,
        in_specs=[pl.BlockSpec((1, 2 * tm, D_MODEL), tile),
                  pl.BlockSpec((1, tm, D_MODEL), ahead),
                  pl.BlockSpec((1, N_MOD, D_MODEL), lambda b, s: (b, 0, 0)),
                  pl.BlockSpec((1, D_MODEL), const),
                  *[pl.BlockSpec(w.shape, const, pipeline_mode=pl.Buffered(1)) for w in weights],
                  pl.BlockSpec((1, N_RKV + N_SMALL), const),
                  pl.BlockSpec((1, LANES), const)],
        out_specs=[pl.BlockSpec((1, 2 * tm, N_RKV), tile),
                   pl.BlockSpec((1, 2 * tm, N_SMALL), tile),
                   pl.BlockSpec((1, 2 * tm, 2 * WIDTH), tile),
                   pl.BlockSpec((1, WIDTH, 2 * tm), lanes),
                   pl.BlockSpec((1, 2 * tm, LANES), tile),
                   pl.BlockSpec((1, HEADS, 2 * tm), lanes),
                   pl.BlockSpec((1, 2 * tm, N_GATE), tile)],
        out_shape=[jax.ShapeDtypeStruct((batch, seq, N_RKV), BF16),
                   jax.ShapeDtypeStruct((batch, seq, N_SMALL), F32),
                   jax.ShapeDtypeStruct((batch, seq, 2 * WIDTH), BF16),
                   jax.ShapeDtypeStruct((batch, WIDTH, seq), BF16),
                   jax.ShapeDtypeStruct((batch, seq, LANES), F32),
                   jax.ShapeDtypeStruct((batch, HEADS, seq), F32),
                   jax.ShapeDtypeStruct((batch, seq, N_GATE), BF16)],
        scratch_shapes=[pltpu.VMEM((tm + 8, N_RKV + N_SMALL), F32),
                        pltpu.VMEM((1, LANES), F32),
                        pltpu.VMEM((2, tm, D_MODEL), BF16)],
        compiler_params=pltpu.CompilerParams(dimension_semantics=("arbitrary", "arbitrary"),
                                             vmem_limit_bytes=VMEM_LIMIT_BYTES),
        name="proj",
    )(x, x, mod3, norm_g, *weights, mu, f_bias)


def _rwkv_prepare(rkv, small, w_lr_ref, w_gate_ref, vec_ref, seg_ref, slot, prep):
    am_ref, rm_ref, vb_ref, bt_ref, kt_ref, bh_ref, kh_ref, dend_ref, g_ref, bv_ref = prep
    c = CHUNK
    r = rkv[:, 0:WIDTH].astype(F32)
    k_raw = rkv[:, WIDTH:2 * WIDTH].astype(F32)
    v = rkv[:, 2 * WIDTH:3 * WIDTH].astype(F32)
    decay_base, iclr_base, kk_scale, k_mix, r_bonus = (vec_ref[i:i + 1, :] for i in range(5))

    lane = lax.broadcasted_iota(jnp.int32, (1, LANES), 1)
    lr_in = jnp.where(lane < DECAY_RANK, jnp.tanh(small[:, 0:LANES]), small[:, 0:LANES])
    lr = _dot(lr_in, w_lr_ref[...])
    w_log = -_softplus(-(decay_base + lr[:, 0:WIDTH])) - 0.5
    log_decay = -jnp.exp(w_log)
    a = _sigmoid(iclr_base + lr[:, WIDTH:2 * WIDTH])
    g_ref[slot] = _dot(_sigmoid(small[:, LANES:2 * LANES]), w_gate_ref[...])
    yield

    seg = seg_ref[...]
    kk = k_raw * kk_scale
    kk = kk * lax.rsqrt(jnp.maximum(_head_sums(kk * kk, seg), 1e-24))
    k = k_raw * (1.0 + (a - 1.0) * k_mix)
    b_vec = kk * a
    yield

    cs = _cumsum_rows(_lower_tri(c, BF16), log_decay, 2)
    cs_end = cs[c - 1:c, :]
    am_ref[slot] = (-kk * jnp.exp(cs - log_decay)).astype(BF16)
    rm_ref[slot] = (r * jnp.exp(cs)).astype(BF16)
    vb_ref[slot] = v.astype(BF16)
    yield
    w_inv = jnp.exp(-cs)
    bt_ref[slot] = jnp.transpose(b_vec * w_inv).astype(BF16)
    yield
    kt_ref[slot] = jnp.transpose(k * w_inv).astype(BF16)
    yield
    w_end = jnp.exp(cs_end - cs)
    bh_ref[slot] = jnp.transpose(b_vec * w_end).astype(BF16)
    yield
    kh_ref[slot] = jnp.transpose(k * w_end).astype(BF16)
    dend_ref[slot] = jnp.broadcast_to(jnp.exp(cs_end), (8, WIDTH))
    yield
    bv_ref[slot] = _head_sums(r * k * r_bonus, seg) * v


def _rwkv_chunk_matrices(slot, prep, mid):
    am_ref, rm_ref, vb_ref, bt_ref, kt_ref = prep[:5]
    pq_ref, lhs_ref = mid
    c = CHUNK
    lane = lax.broadcasted_iota(jnp.int32, (1, LANES), 1)
    row = lax.broadcasted_iota(jnp.int32, (c, c), 0)
    col = lax.broadcasted_iota(jnp.int32, (c, c), 1)
    strict = col < row
    incl = col <= row
    eye = (col == row).astype(F32)

    sls = [slice(p * PAIR, (p + 1) * PAIR) for p in range(N_PAIRS)]
    heads = [(p, h) for p in range(N_PAIRS) for h in range(2)]
    a_m, a_ab, a_ak = {}, {}, {}
    for i, (p, h) in enumerate(heads):
        head = (lane // HEAD_DIM) == h
        a_m[p, h] = jnp.where(head, am_ref[slot, :, sls[p]].astype(F32), 0.0).astype(BF16)
        r_m = jnp.where(head, rm_ref[slot, :, sls[p]].astype(F32), 0.0).astype(BF16)
        rhs = jnp.concatenate([bt_ref[slot, sls[p], :], kt_ref[slot, sls[p], :]], axis=1)
        big = jnp.dot(jnp.concatenate([a_m[p, h], r_m], axis=0), rhs,
                      preferred_element_type=F32)
        a_ab[p, h] = jnp.where(strict, big[0:c, 0:c], 0.0)
        a_ak[p, h] = jnp.where(strict, big[0:c, c:2 * c], 0.0).astype(BF16)
        lhs_ref[slot, i] = jnp.concatenate(
            [r_m, jnp.where(incl, big[c:2 * c, 0:c], 0.0).astype(BF16),
             jnp.where(incl, big[c:2 * c, c:2 * c], 0.0).astype(BF16)], axis=1)
    yield
    t_inv = {hd: eye + a_ab[hd] for hd in heads}
    m_pow = {hd: _dot(a_ab[hd], a_ab[hd]) for hd in heads}
    yield
    for k in range(1, 7):
        skip = (2 ** k) // 16 * 16
        last = k == 6
        for hd in heads:
            rhs = t_inv[hd] if last else jnp.concatenate([m_pow[hd], t_inv[hd]], axis=1)
            upd = _dot(m_pow[hd][skip:c, :], rhs)
            if skip:
                upd = jnp.concatenate([jnp.zeros((skip, upd.shape[1]), F32), upd], axis=0)
            if last:
                t_inv[hd] = t_inv[hd] + upd
            else:
                m_pow[hd] = upd[:, 0:c]
                t_inv[hd] = t_inv[hd] + upd[:, c:2 * c]
        yield
    ak_v = {(p, h): _dot(a_ak[p, h], vb_ref[slot, :, sls[p]]) for (p, h) in heads}
    yield
    for i, hd in enumerate(heads):
        pq_ref[slot, i] = _dot(t_inv[hd],
                               jnp.concatenate([a_m[hd].astype(F32), ak_v[hd]], axis=1))


def _rwkv_chunk_state(slot, prep, mid, vec_ref, seg_ref, z_ref, out_ref, rows):
    vb_ref, bh_ref, kh_ref, dend_ref, g_ref, bv_ref = (prep[i] for i in (2, 5, 6, 7, 8, 9))
    pq_ref, lhs_ref = mid
    c = CHUNK
    ln_w, ln_b = vec_ref[5:6, :], vec_ref[6:7, :]
    lane = lax.broadcasted_iota(jnp.int32, (1, LANES), 1)
    row = lax.broadcasted_iota(jnp.int32, (c, c), 0)
    col = lax.broadcasted_iota(jnp.int32, (c, c), 1)
    eye = (col == row).astype(F32)
    same_head = (row // HEAD_DIM) == (col // HEAD_DIM)
    pairs = range(N_PAIRS)
    sls = [slice(p * PAIR, (p + 1) * PAIR) for p in pairs]
    heads = [(p, h) for p in pairs for h in range(2)]
    v_b = [vb_ref[slot, :, sl] for sl in sls]
    z_f = [z_ref[p] for p in pairs]
    z_b = [z.astype(BF16) for z in z_f]

    u = {(p, h): _dot(pq_ref[slot, i, :, 0:PAIR], z_b[p]) + pq_ref[slot, i, :, PAIR:2 * PAIR]
         for i, (p, h) in enumerate(heads)}
    yield
    y_hd = {(p, h): jnp.dot(
        lhs_ref[slot, i], jnp.concatenate([z_b[p], u[p, h].astype(BF16), v_b[p]], axis=0),
        preferred_element_type=F32) for i, (p, h) in enumerate(heads)}
    first = lane < HEAD_DIM
    y_pairs = []
    for p in pairs:
        u_p = jnp.where(first, u[p, 0], u[p, 1])
        y_pairs.append(jnp.where(first, y_hd[p, 0], y_hd[p, 1]))
        d_col = jnp.sum(eye * dend_ref[slot, 0:1, sls[p]], axis=1, keepdims=True)
        z_new = d_col * z_f[p] + jnp.dot(
            jnp.concatenate([bh_ref[slot, sls[p], :], kh_ref[slot, sls[p], :]], axis=1),
            jnp.concatenate([u_p.astype(BF16), v_b[p]], axis=0), preferred_element_type=F32)
        z_ref[p] = jnp.where(same_head, z_new, 0.0)
    yield

    seg = seg_ref[...]
    y = jnp.concatenate(y_pairs, axis=1)
    mean = _head_sums(y, seg) * (1.0 / HEAD_DIM)
    yc = y - mean
    var = _head_sums(yc * yc, seg) * (1.0 / HEAD_DIM)
    y = yc * lax.rsqrt(var + GN_EPS) * ln_w + ln_b
    out_ref[0, rows, :] = ((y + bv_ref[slot]) * g_ref[slot]).astype(BF16)


def _interleave(*stages, late=()):
    live = list(stages)
    late = list(late)
    while live:
        for gen in list(live):
            try:
                next(gen)
            except StopIteration:
                live.remove(gen)
                if late and gen is stages[0]:
                    live.extend(late)
                    late = []


def _in_turn(*stages):
    for gen in stages:
        yield from gen


def _rwkv_kernel(rkv_ref, small_ref, rkv_next_ref, small_next_ref,
                 w_lr_ref, w_gate_ref, vec_ref, seg_ref, out_ref, z_ref, pq_ref, lhs_ref, *prep):
    c = CHUNK
    params = (w_lr_ref, w_gate_ref, vec_ref, seg_ref)
    mid = (pq_ref, lhs_ref)
    rows = [slice(i * c, (i + 1) * c) for i in range(RWKV_STEP_CHUNKS)]

    def prepare(r_ref, s_ref, i, slot):
        return _rwkv_prepare(r_ref[0, rows[i], :], s_ref[0, rows[i], :], *params, slot, prep)

    def matrices(slot):
        return _rwkv_chunk_matrices(slot, prep, mid)

    def state(slot):
        return _rwkv_chunk_state(slot, prep, mid, vec_ref, seg_ref, z_ref, out_ref, rows[slot])

    @pl.when(pl.program_id(1) == 0)
    def _():
        z_ref[...] = jnp.zeros_like(z_ref)
        _interleave(prepare(rkv_ref, small_ref, 0, 0), prepare(rkv_ref, small_ref, 1, 1))

    _interleave(matrices(0), matrices(1),
                prepare(rkv_ref, small_ref, 2, 2), prepare(rkv_ref, small_ref, 3, 3))
    _interleave(_in_turn(state(0), state(1)), matrices(2), matrices(3),
                late=(prepare(rkv_next_ref, small_next_ref, 0, 0),
                      prepare(rkv_next_ref, small_next_ref, 1, 1)))
    _interleave(_in_turn(state(2), state(3)))


def _rwkv_mix(rkv, small, w_lr, w_gate, vecs, seg):
    batch, seq, _ = rkv.shape
    c = CHUNK
    n = RWKV_STEP_CHUNKS
    n_half = seq // (n // 2 * c)
    grid = (batch, seq // (n * c))
    const = lambda b, s: (0, 0)
    tile = lambda b, s: (b, s, 0)
    ahead = lambda b, s: (b, jnp.minimum(2 * s + 2, n_half - 1), 0)
    slots = lambda shape, dtype: pltpu.VMEM((n,) + shape, dtype)
    return pl.pallas_call(
        _rwkv_kernel,
        grid=grid,
        in_specs=[pl.BlockSpec((1, n * c, N_RKV), tile),
                  pl.BlockSpec((1, n * c, N_SMALL), tile),
                  pl.BlockSpec((1, n // 2 * c, N_RKV), ahead),
                  pl.BlockSpec((1, n // 2 * c, N_SMALL), ahead),
                  pl.BlockSpec(w_lr.shape, const),
                  pl.BlockSpec(w_gate.shape, const),
                  pl.BlockSpec(vecs.shape, const),
                  pl.BlockSpec(seg.shape, const)],
        out_specs=pl.BlockSpec((1, n * c, WIDTH), tile),
        out_shape=jax.ShapeDtypeStruct((batch, seq, WIDTH), BF16),
        scratch_shapes=[pltpu.VMEM((N_PAIRS, PAIR, PAIR), F32),
                        slots((HEADS, c, 2 * PAIR), F32),
                        slots((HEADS, c, 3 * PAIR), BF16),
                        slots((c, WIDTH), BF16),
                        slots((c, WIDTH), BF16),
                        slots((c, WIDTH), BF16),
                        slots((WIDTH, c), BF16),
                        slots((WIDTH, c), BF16),
                        slots((WIDTH, c), BF16),
                        slots((WIDTH, c), BF16),
                        slots((8, WIDTH), F32),
                        slots((c, WIDTH), F32),
                        slots((c, WIDTH), F32)],
        compiler_params=pltpu.CompilerParams(dimension_semantics=("arbitrary", "arbitrary"),
                                             vmem_limit_bytes=VMEM_LIMIT_BYTES),
        name="rwkv",
    )(rkv, small, rkv, small, w_lr, w_gate, vecs, seg)


def _fox_kernel(q_ref, k_ref, vt_ref, cum_ref, cumt_ref, mask_ref, route_ref, o_ref,
                qx_ref, kx_ref, vx_ref, t_ref, p_ref):
    heads = range(FOX_STEP_HEADS)
    head0 = FOX_STEP_HEADS * pl.program_id(1)
    blk = FOX_BLOCK
    seq = k_ref.shape[1]
    n_blk = seq // blk
    lane = lax.broadcasted_iota(jnp.int32, (1, LANES), 1)
    own = [(lane // HEAD_DIM) == h for h in range(2)]
    bias_lane = [HEAD_DIM * (1 - h) for h in range(2)]

    def fill(i, _):
        rows = pl.ds(pl.multiple_of(i * blk, blk), blk)
        parts = jnp.concatenate(_bf16_parts(cum_ref[0, rows, :] * (-LOG2E), 3), axis=1)
        for pp in range(FOX_STEP_HEADS // 2):
            group = slice(pp * PAIR, (pp + 1) * PAIR)
            feat = jnp.dot(parts, route_ref[pp], preferred_element_type=F32)
            k_rows = k_ref[0, rows, group].astype(F32)
            q_rows = q_ref[0, rows, group].astype(F32)
            for h in range(2):
                kx_ref[2 * pp + h, rows, :] = jnp.where(own[h], k_rows, feat).astype(BF16)
                is_bias = (lane >= bias_lane[h]) & (lane < bias_lane[h] + 3)
                qx_ref[2 * pp + h, rows, :] = jnp.where(
                    own[h], q_rows, is_bias.astype(F32)).astype(BF16)
        return 0
    lax.fori_loop(0, n_blk, fill, 0)
    ones_row = lax.broadcasted_iota(jnp.int32, (FOX_VROWS - HEAD_DIM, seq), 0) == 0
    for h in heads:
        vx_ref[h, 0:HEAD_DIM, :] = vt_ref[0, h * HEAD_DIM:(h + 1) * HEAD_DIM, :]
        vx_ref[h, HEAD_DIM:FOX_VROWS, :] = ones_row.astype(BF16)

    def rows_of(b):
        return pl.ds(pl.multiple_of(b * blk, blk), blk)

    def following(qi, kj):
        wrap = kj == qi
        return jnp.where(wrap, qi + 1, qi), jnp.where(wrap, 0, kj + 1)

    def scores(pair, slot, out):
        qi, kj = pair
        qi = jnp.minimum(qi, n_blk - 1)
        causal = mask_ref[(kj == qi).astype(jnp.int32)]
        mx = []
        for h in heads:
            t = _dot_nt(kx_ref[h, rows_of(kj), :], qx_ref[h, rows_of(qi), :]) + causal
            t_ref[slot, h] = t
            c_q = cumt_ref[0, pl.ds(head0 + h, 1), rows_of(qi)] * LOG2E
            mx.append(jnp.max(t, axis=0, keepdims=True) + c_q)
            yield
        out["mx"] = tuple(mx)

    def softmax(pair, slot, m, mx, out):
        qi, kj = pair
        sub = FOX_SUB
        q_rows = rows_of(jnp.minimum(qi, n_blk - 1))
        m_out, alpha_out = [], []
        for h in heads:
            m_old = jnp.where(kj == 0, -jnp.inf, m[h])
            c_q = cumt_ref[0, pl.ds(head0 + h, 1), q_rows] * LOG2E
            m_new = jnp.maximum(m_old, mx[h])
            shift = c_q - m_new
            for i in range(blk // sub):
                rows = slice(i * sub, (i + 1) * sub)
                p_ref[slot, h, rows, :] = jnp.exp2(t_ref[slot, h, rows, :] + shift).astype(BF16)
                if i % 2 == 1:
                    yield
            m_out.append(m_new)
            alpha_out.append(jnp.exp2(m_old - m_new))
        out["m"], out["alpha"] = tuple(m_out), tuple(alpha_out)

    def accumulate(pair, slot, alpha, acc, out):
        qi, kj = pair
        new = []
        for h in heads:
            new.append(alpha[h] * acc[h] + jnp.dot(vx_ref[h, :, rows_of(kj)], p_ref[slot, h],
                                                   preferred_element_type=F32))
            yield
        o_t = jnp.concatenate([a[0:HEAD_DIM] / a[HEAD_DIM:HEAD_DIM + 1] for a in new], axis=0)
        o_ref[0, :, rows_of(qi)] = o_t.astype(BF16)
        out["acc"] = tuple(new)

    def step(pair_s, pair_p, pair_a, slot_s, m, mx, alpha, acc):
        out = {}
        _interleave(softmax(pair_p, 1 - slot_s, m, mx, out),
                    scores(pair_s, slot_s, out),
                    accumulate(pair_a, slot_s, alpha, acc, out))
        return out

    def two_pairs(_, carry):
        pair0, pair1, m, mx1, alpha0, acc = carry
        pair2 = following(*pair1)
        a = step(pair2, pair1, pair0, 0, m, mx1, alpha0, acc)
        pair3 = following(*pair2)
        b = step(pair3, pair2, pair1, 1, a["m"], a["mx"], a["alpha"], a["acc"])
        return pair2, pair3, b["m"], b["mx"], b["alpha"], b["acc"]

    n_pairs = n_blk * (n_blk + 1) // 2
    assert n_pairs % 2 == 0, "two pairs per trip"
    zero = jnp.int32(0)
    pair0 = (zero, zero)
    pair1 = following(*pair0)
    first, second = {}, {}
    _interleave(scores(pair0, 0, first))
    _interleave(scores(pair1, 1, second))
    m = tuple(jnp.full((1, blk), -jnp.inf, F32) for _ in heads)
    _interleave(softmax(pair0, 0, m, first["mx"], first))
    acc = tuple(jnp.zeros((FOX_VROWS, blk), F32) for _ in heads)
    lax.fori_loop(0, n_pairs // 2, two_pairs,
                  (pair0, pair1, first["m"], second["mx"], first["alpha"], acc))


def _forgetting_attention(qk, v_t, cum, cum_t):
    batch, seq, _ = qk.shape
    blk = FOX_BLOCK
    key = lax.broadcasted_iota(jnp.int32, (blk, blk), 0)
    qry = lax.broadcasted_iota(jnp.int32, (blk, blk), 1)
    mask = jnp.stack([jnp.zeros((blk, blk), F32), jnp.where(key <= qry, 0.0, FOX_MASKED)])
    pair = jnp.arange(N_PAIRS)[:, None, None]
    src = jnp.arange(3 * LANES)[None, :, None]
    dst = jnp.arange(LANES)[None, None, :]
    route = jnp.zeros((N_PAIRS, 3 * LANES, LANES), jnp.bool_)
    for h in range(2):
        for n in range(3):
            route |= (src == n * LANES + 2 * pair + h) & (dst == HEAD_DIM * (1 - h) + n)
    route = route.astype(BF16)
    whole = lambda b, g: (b, 0, 0)
    nh = FOX_STEP_HEADS
    width = nh * HEAD_DIM
    n_groups = HEADS // nh
    return pl.pallas_call(
        _fox_kernel,
        grid=(batch, n_groups),
        in_specs=[pl.BlockSpec((1, seq, width), lambda b, g: (b, 0, g)),
                  pl.BlockSpec((1, seq, width), lambda b, g: (b, 0, n_groups + g)),
                  pl.BlockSpec((1, width, seq), lambda b, g: (b, g, 0)),
                  pl.BlockSpec((1, seq, LANES), whole),
                  pl.BlockSpec((1, HEADS, seq), whole),
                  pl.BlockSpec((2, blk, blk), lambda b, g: (0, 0, 0)),
                  pl.BlockSpec((nh // 2, 3 * LANES, LANES), lambda b, g: (g, 0, 0))],
        out_specs=pl.BlockSpec((1, width, seq), lambda b, g: (b, g, 0)),
        out_shape=jax.ShapeDtypeStruct((batch, WIDTH, seq), BF16),
        scratch_shapes=[pltpu.VMEM((nh, seq, LANES), BF16),
                        pltpu.VMEM((nh, seq, LANES), BF16),
                        pltpu.VMEM((nh, FOX_VROWS, seq), BF16),
                        pltpu.VMEM((2, nh, blk, blk), F32),
                        pltpu.VMEM((2, nh, blk, blk), BF16)],
        compiler_params=pltpu.CompilerParams(
            dimension_semantics=("arbitrary", "arbitrary"),
            vmem_limit_bytes=VMEM_LIMIT_BYTES),
        name="fox",
    )(qk, qk, v_t, cum, cum_t, mask, route)


def _tail_kernel(x_ref, ya_ref, ybt_ref, gate_ref, mod_ref, g2_ref, gf_ref,
                 woa_ref, wob_ref, wout_ref, w1_ref, w2_ref, o_ref):
    x = x_ref[0]
    gate1 = mod_ref[0, 2:3, :]
    shift2 = mod_ref[0, 3:4, :]
    scale2 = mod_ref[0, 4:5, :]
    gate2 = mod_ref[0, 5:6, :]

    merged = (gate_ref[0, :, 0:D_MODEL].astype(F32)
              * jnp.dot(ya_ref[0], woa_ref[...], preferred_element_type=F32)
              + gate_ref[0, :, D_MODEL:N_GATE].astype(F32)
              * lax.dot_general(ybt_ref[0], wob_ref[...], (((0,), (0,)), ((), ())),
                                preferred_element_type=F32))
    x = x + gate1 * _dot(merged, wout_ref[...])

    inv = lax.rsqrt(jnp.mean(x * x, axis=-1, keepdims=True) + NORM_EPS)
    h2 = ((x * inv) * g2_ref[...] * (1.0 + scale2) + shift2).astype(BF16)
    ff = jnp.zeros_like(x)
    for j in range(D_FF // FF_CHUNK):
        cols = slice(j * FF_CHUNK, (j + 1) * FF_CHUNK)
        hid = jnp.maximum(jnp.dot(h2, w1_ref[:, cols], preferred_element_type=F32), 0.0)
        ff = ff + _dot(hid * hid, w2_ref[cols, :])
    x = x + gate2 * ff

    inv = lax.rsqrt(jnp.mean(x * x, axis=-1, keepdims=True) + NORM_EPS)
    o_ref[0] = (x * inv) * gf_ref[...]


def _tail(x, y_a, y_b, gates, mod3, norm2_g, final_g, w_oa, w_ob, w_out, w_ff1, w_ff2):
    batch, seq, _ = x.shape
    tm = TAIL_ROWS
    grid = (batch, seq // tm)
    const = lambda b, s: (0, 0)
    tile = lambda b, s: (b, s, 0)
    resident = lambda a: pl.BlockSpec(a.shape, const, pipeline_mode=pl.Buffered(1))
    return pl.pallas_call(
        _tail_kernel,
        grid=grid,
        in_specs=[pl.BlockSpec((1, tm, D_MODEL), tile),
                  pl.BlockSpec((1, tm, WIDTH), tile),
                  pl.BlockSpec((1, WIDTH, tm), lambda b, s: (b, 0, s)),
                  pl.BlockSpec((1, tm, N_GATE), tile),
                  pl.BlockSpec((1, N_MOD, D_MODEL), lambda b, s: (b, 0, 0)),
                  pl.BlockSpec((1, D_MODEL), const),
                  pl.BlockSpec((1, D_MODEL), const),
                  resident(w_oa), resident(w_ob), resident(w_out),
                  resident(w_ff1), resident(w_ff2)],
        out_specs=pl.BlockSpec((1, tm, D_MODEL), tile),
        out_shape=jax.ShapeDtypeStruct((batch, seq, D_MODEL), F32),
        compiler_params=pltpu.CompilerParams(dimension_semantics=("arbitrary", "arbitrary"),
                                             vmem_limit_bytes=VMEM_LIMIT_BYTES),
        name="tail",
    )(x, y_a, y_b, gates, mod3, norm2_g, final_g, w_oa, w_ob, w_out, w_ff1, w_ff2)


def _reorder_rwkv_cols(t):
    o = 0
    r = t[..., o:o + WIDTH]; o += WIDTH
    wd = t[..., o:o + DECAY_RANK]; o += DECAY_RANK
    k = t[..., o:o + WIDTH]; o += WIDTH
    v = t[..., o:o + WIDTH]; o += WIDTH
    ad = t[..., o:o + ICLR_RANK]; o += ICLR_RANK
    gd = t[..., o:o + GATE_RANK]
    return jnp.concatenate([r, k, v, wd, ad, gd], axis=-1)


def kernel(x, c, w_ada, b_ada, norm1_g, w_in, mu_shift, w_decay_up, decay_base, w_iclr_up, iclr_base, w_gate_up, kk_scale, k_iclr_mix, r_bonus, lnx_w, lnx_b, fox_f_bias, w_o_rwkv, w_o_fox, w_out, norm2_g, w_ff1, w_ff2, final_g):
    assert w_ada.shape[0] == 1, "the tail kernel fuses the final norm: single layer only"
    l = 0
    n_rwkv = N_RKV + N_SMALL
    seg_id = jnp.arange(WIDTH // 2) // HEAD_DIM
    seg = (seg_id[:, None] == seg_id[None, :]).astype(BF16)

    mod3 = _modulation(c, w_ada[l], b_ada[l]).reshape(-1, N_MOD, D_MODEL)

    w = w_in[l]
    o_wd, o_k, o_v, o_ad = WIDTH, WIDTH + DECAY_RANK, 2 * WIDTH + DECAY_RANK, 3 * WIDTH + DECAY_RANK
    o_ff = n_rwkv + N_FOX
    weights = [w[:, 0:o_wd], w[:, o_k:o_v], w[:, o_v:o_ad],
               jnp.concatenate([w[:, o_wd:o_k], w[:, o_ad:n_rwkv]], axis=1),
               w[:, n_rwkv:o_ff],
               jnp.pad(w[:, o_ff:o_ff + HEADS], ((0, 0), (0, LANES - HEADS))),
               w[:, o_ff + HEADS:]]
    weights = [g.astype(BF16) for g in weights]
    mu = _reorder_rwkv_cols(mu_shift[l]).reshape(1, n_rwkv)
    f_bias = jnp.pad(fox_f_bias[l], (0, LANES - HEADS)).reshape(1, LANES)
    rkv, small, qk, v_t, cum, cum_t, gates = _projection(
        x, mod3, norm1_g[l].reshape(1, D_MODEL), weights, mu, f_bias)

    zeros = jnp.zeros((DECAY_RANK, WIDTH), F32)
    w_lr = jnp.concatenate(
        [jnp.concatenate([w_decay_up[l], zeros], axis=1),
         jnp.concatenate([zeros, w_iclr_up[l]], axis=1)], axis=0).astype(BF16)
    vecs = jnp.stack([decay_base[l], iclr_base[l], kk_scale[l], k_iclr_mix[l],
                      r_bonus[l].reshape(WIDTH), lnx_w[l], lnx_b[l],
                      jnp.zeros((WIDTH,), F32)], axis=0)
    y_a = _rwkv_mix(rkv, small, w_lr, w_gate_up[l].astype(BF16), vecs, seg)

    y_b = _forgetting_attention(qk, v_t, cum, cum_t)

    return _tail(x, y_a, y_b, gates, mod3, norm2_g[l].reshape(1, D_MODEL),
                 final_g.reshape(1, D_MODEL),
                 w_o_rwkv[l].astype(BF16), w_o_fox[l].astype(BF16), w_out[l].astype(BF16),
                 w_ff1[l].astype(BF16), w_ff2[l].astype(BF16))
```

```python
import jax
import jax.numpy as jnp
from jax import lax
from jax.experimental import pallas as pl
from jax.experimental.pallas import tpu as pltpu

F32 = jnp.float32
BF16 = jnp.bfloat16

D_MODEL = 1024
HEAD_DIM = 64
HEADS = 8
WIDTH = HEADS * HEAD_DIM
DECAY_RANK = 64
ICLR_RANK = 64
GATE_RANK = 128
D_FF = 4 * D_MODEL
N_MOD = 6
NORM_EPS = 1e-6
GN_EPS = 64e-5

LANES = 128
PAIR = 2 * HEAD_DIM
N_PAIRS = HEADS // 2

LOG2E = 1.4426950408889634

N_SMALL = DECAY_RANK + ICLR_RANK + GATE_RANK
N_RKV = 3 * WIDTH
N_FOX = 3 * WIDTH
N_GATE = 2 * D_MODEL

PROJ_ROWS = 256
CHUNK = 128
RWKV_STEP_CHUNKS = 8
RWKV_MID_SLOTS = 4
FOX_BLOCK = 512
FOX_SUB = 32
FOX_VROWS = HEAD_DIM + 16
FOX_STEP_HEADS = 4
FOX_MASKED = -1e30
TAIL_ROWS = 512
FF_CHUNK = 1024

VMEM_LIMIT_BYTES = 56 * 1024 * 1024


def _dot(a, b):
    return jnp.dot(a.astype(BF16), b.astype(BF16), preferred_element_type=F32)


def _dot_nt(a, b):
    return lax.dot_general(a.astype(BF16), b.astype(BF16), (((1,), (1,)), ((), ())),
                           preferred_element_type=F32)


def _softplus(z):
    return jnp.maximum(z, 0.0) + jnp.log(1.0 + jnp.exp(-jnp.abs(z)))


def _sigmoid(z):
    return 1.0 / (1.0 + jnp.exp(-z))


def _bf16_parts(x, n):
    parts = []
    for _ in range(n):
        p = x.astype(BF16)
        parts.append(p)
        x = x - p.astype(F32)
    return parts


def _cumsum_rows(tri, x, n_parts):
    return sum(jnp.dot(tri, p, preferred_element_type=F32) for p in _bf16_parts(x, n_parts))


def _head_sums(x, seg):
    half = seg.shape[0]
    return jnp.concatenate([_dot(x[:, 0:half], seg), _dot(x[:, half:2 * half], seg)], axis=1)


def _lower_tri(n, dtype):
    row = lax.broadcasted_iota(jnp.int32, (n, n), 0)
    col = lax.broadcasted_iota(jnp.int32, (n, n), 1)
    return (col <= row).astype(dtype)


def _mod_kernel(c_ref, w_ref, b_ref, o_ref):
    c = c_ref[...]
    c_act = c * _sigmoid(c)
    o_ref[...] = jnp.dot(c_act, w_ref[...], preferred_element_type=F32,
                         precision=lax.Precision.HIGHEST) + b_ref[...]


def _modulation(c, w_ada, b_ada):
    batch = c.shape[0]
    n = w_ada.shape[1]
    tn = 1536
    return pl.pallas_call(
        _mod_kernel,
        grid=(n // tn,),
        in_specs=[pl.BlockSpec((batch, D_MODEL), lambda j: (0, 0)),
                  pl.BlockSpec((D_MODEL, tn), lambda j: (0, j)),
                  pl.BlockSpec((1, tn), lambda j: (0, j))],
        out_specs=pl.BlockSpec((batch, tn), lambda j: (0, j)),
        out_shape=jax.ShapeDtypeStruct((batch, n), F32),
        compiler_params=pltpu.CompilerParams(dimension_semantics=("arbitrary",),
                                             vmem_limit_bytes=VMEM_LIMIT_BYTES),
        name="mod",
    )(c, w_ada, b_ada.reshape(1, n))


def _proj_normalise(x, mod_ref, g_ref):
    shift1 = mod_ref[0, 0:1, :]
    scale1 = mod_ref[0, 1:2, :]
    inv = lax.rsqrt(jnp.mean(x * x, axis=-1, keepdims=True) + NORM_EPS)
    return ((x * inv) * (g_ref[...] * (1.0 + scale1)) + shift1).astype(BF16)


def _proj_tile(h, rows, weights, mu_ref, fb_ref, outs, shift_ref, carry_ref):
    wr_ref, wk_ref, wv_ref, ws_ref, wfox_ref, wff_ref, wgate_ref = weights
    rkv_ref, small_ref, qk_ref, vt_ref, cum_ref, cumt_ref, gate_ref = outs
    n = h.shape[0]

    n_tok = N_RKV + N_SMALL
    p = jnp.concatenate([jnp.dot(h, w[...], preferred_element_type=F32)
                         for w in (wr_ref, wk_ref, wv_ref, ws_ref)], axis=1)
    shift_ref[8:8 + n, :] = p
    prev = shift_ref[7:7 + n, :]
    shift_ref[7:8, :] = p[n - 1:n, :]
    mixed = p + mu_ref[...] * (prev - p)
    rkv_ref[0, rows, :] = mixed[:, 0:N_RKV].astype(BF16)
    small_ref[0, rows, :] = mixed[:, N_RKV:n_tok]

    pf = jnp.dot(h, wfox_ref[...], preferred_element_type=F32)
    qk_ref[0, rows, 0:WIDTH] = (pf[:, 0:WIDTH] * (LOG2E * HEAD_DIM ** -0.5)).astype(BF16)
    qk_ref[0, rows, WIDTH:2 * WIDTH] = pf[:, WIDTH:2 * WIDTH].astype(BF16)
    vt_ref[0, :, rows] = jnp.transpose(pf[:, 2 * WIDTH:N_FOX]).astype(BF16)

    ff = jnp.dot(h, wff_ref[...], preferred_element_type=F32)
    logf = -_softplus(-(ff + fb_ref[...]))
    cum = _cumsum_rows(_lower_tri(n, BF16), logf, 3) + carry_ref[...]
    cum_ref[0, rows, :] = cum
    cumt_ref[0, :, rows] = jnp.transpose(cum)[0:HEADS, :]
    carry_ref[...] = cum[n - 1:n, :]

    pg = jnp.dot(h, wgate_ref[...], preferred_element_type=F32)
    gate_ref[0, rows, :] = _sigmoid(pg).astype(BF16)


def _proj_kernel(x_ref, x_next_ref, mod_ref, g_ref, wr_ref, wk_ref, wv_ref, ws_ref, wfox_ref,
                 wff_ref, wgate_ref, mu_ref, fb_ref,
                 rkv_ref, small_ref, qk_ref, vt_ref, cum_ref, cumt_ref, gate_ref,
                 shift_ref, carry_ref, h_ref):
    tm = PROJ_ROWS
    weights = (wr_ref, wk_ref, wv_ref, ws_ref, wfox_ref, wff_ref, wgate_ref)
    outs = (rkv_ref, small_ref, qk_ref, vt_ref, cum_ref, cumt_ref, gate_ref)

    @pl.when(pl.program_id(1) == 0)
    def _():
        shift_ref[0:8, :] = jnp.zeros((8, N_RKV + N_SMALL), F32)
        carry_ref[...] = jnp.zeros_like(carry_ref)
        h_ref[0] = _proj_normalise(x_ref[0, 0:tm, :], mod_ref, g_ref)

    h_ref[1] = _proj_normalise(x_ref[0, tm:2 * tm, :], mod_ref, g_ref)
    _proj_tile(h_ref[0], slice(0, tm), weights, mu_ref, fb_ref, outs, shift_ref, carry_ref)
    h_ref[0] = _proj_normalise(x_next_ref[0], mod_ref, g_ref)
    _proj_tile(h_ref[1], slice(tm, 2 * tm), weights, mu_ref, fb_ref, outs, shift_ref, carry_ref)


def _projection(x, mod3, norm_g, weights, mu, f_bias):
    batch, seq, _ = x.shape
    tm = PROJ_ROWS
    n_tiles = seq // tm
    grid = (batch, n_tiles // 2)
    const = lambda b, s: (0, 0)
    tile = lambda b, s: (b, s, 0)
    lanes = lambda b, s: (b, 0, s)
    ahead = lambda b, s: (b, jnp.minimum(2 * s + 2, n_tiles - 1), 0)
    return pl.pallas_call(
        _proj_kernel,
        grid=grid,
        in_specs=[pl.BlockSpec((1, 2 * tm, D_MODEL), tile),
                  pl.BlockSpec((1, tm, D_MODEL), ahead),
                  pl.BlockSpec((1, N_MOD, D_MODEL), lambda b, s: (b, 0, 0)),
                  pl.BlockSpec((1, D_MODEL), const),
                  *[pl.BlockSpec(w.shape, const, pipeline_mode=pl.Buffered(1)) for w in weights],
                  pl.BlockSpec((1, N_RKV + N_SMALL), const),
                  pl.BlockSpec((1, LANES), const)],
        out_specs=[pl.BlockSpec((1, 2 * tm, N_RKV), tile),
                   pl.BlockSpec((1, 2 * tm, N_SMALL), tile),
                   pl.BlockSpec((1, 2 * tm, 2 * WIDTH), tile),
                   pl.BlockSpec((1, WIDTH, 2 * tm), lanes),
                   pl.BlockSpec((1, 2 * tm, LANES), tile),
                   pl.BlockSpec((1, HEADS, 2 * tm), lanes),
                   pl.BlockSpec((1, 2 * tm, N_GATE), tile)],
        out_shape=[jax.ShapeDtypeStruct((batch, seq, N_RKV), BF16),
                   jax.ShapeDtypeStruct((batch, seq, N_SMALL), F32),
                   jax.ShapeDtypeStruct((batch, seq, 2 * WIDTH), BF16),
                   jax.ShapeDtypeStruct((batch, WIDTH, seq), BF16),
                   jax.ShapeDtypeStruct((batch, seq, LANES), F32),
                   jax.ShapeDtypeStruct((batch, HEADS, seq), F32),
                   jax.ShapeDtypeStruct((batch, seq, N_GATE), BF16)],
        scratch_shapes=[pltpu.VMEM((tm + 8, N_RKV + N_SMALL), F32),
                        pltpu.VMEM((1, LANES), F32),
                        pltpu.VMEM((2, tm, D_MODEL), BF16)],
        compiler_params=pltpu.CompilerParams(dimension_semantics=("arbitrary", "arbitrary"),
                                             vmem_limit_bytes=VMEM_LIMIT_BYTES),
        name="proj",
    )(x, x, mod3, norm_g, *weights, mu, f_bias)


def _rwkv_prepare(rkv, small, w_lr_ref, w_gate_ref, vec_ref, seg_ref, slot, prep):
    am_ref, rm_ref, vb_ref, bt_ref, kt_ref, bh_ref, kh_ref, dend_ref, g_ref, bv_ref = prep
    c = CHUNK
    r = rkv[:, 0:WIDTH].astype(F32)
    k_raw = rkv[:, WIDTH:2 * WIDTH].astype(F32)
    v = rkv[:, 2 * WIDTH:3 * WIDTH].astype(F32)
    decay_base, iclr_base, kk_scale, k_mix, r_bonus = (vec_ref[i:i + 1, :] for i in range(5))

    lane = lax.broadcasted_iota(jnp.int32, (1, LANES), 1)
    lr_in = jnp.where(lane < DECAY_RANK, jnp.tanh(small[:, 0:LANES]), small[:, 0:LANES])
    lr = _dot(lr_in, w_lr_ref[...])
    w_log = -_softplus(-(decay_base + lr[:, 0:WIDTH])) - 0.5
    log_decay = -jnp.exp(w_log)
    a = _sigmoid(iclr_base + lr[:, WIDTH:2 * WIDTH])
    g_ref[slot] = _dot(_sigmoid(small[:, LANES:2 * LANES]), w_gate_ref[...])
    yield

    seg = seg_ref[...]
    kk = k_raw * kk_scale
    kk = kk * lax.rsqrt(jnp.maximum(_head_sums(kk * kk, seg), 1e-24))
    k = k_raw * (1.0 + (a - 1.0) * k_mix)
    b_vec = kk * a
    yield

    cs = _cumsum_rows(_lower_tri(c, BF16), log_decay, 2)
    cs_end = cs[c - 1:c, :]
    am_ref[slot] = (-kk * jnp.exp(cs - log_decay)).astype(BF16)
    rm_ref[slot] = (r * jnp.exp(cs)).astype(BF16)
    vb_ref[slot] = v.astype(BF16)
    yield
    w_inv = jnp.exp(-cs)
    bt_ref[slot] = jnp.transpose(b_vec * w_inv).astype(BF16)
    yield
    kt_ref[slot] = jnp.transpose(k * w_inv).astype(BF16)
    yield
    w_end = jnp.exp(cs_end - cs)
    bh_ref[slot] = jnp.transpose(b_vec * w_end).astype(BF16)
    yield
    kh_ref[slot] = jnp.transpose(k * w_end).astype(BF16)
    dend_ref[slot] = jnp.broadcast_to(jnp.exp(cs_end), (8, WIDTH))
    yield
    bv_ref[slot] = _head_sums(r * k * r_bonus, seg) * v


def _rwkv_chunk_matrices(slot, mid_slot, prep, mid):
    am_ref, rm_ref, vb_ref, bt_ref, kt_ref = prep[:5]
    pq_ref, lhs_ref = mid
    c = CHUNK
    lane = lax.broadcasted_iota(jnp.int32, (1, LANES), 1)
    row = lax.broadcasted_iota(jnp.int32, (c, c), 0)
    col = lax.broadcasted_iota(jnp.int32, (c, c), 1)
    strict = col < row
    incl = col <= row
    eye = (col == row).astype(F32)

    sls = [slice(p * PAIR, (p + 1) * PAIR) for p in range(N_PAIRS)]
    heads = [(p, h) for p in range(N_PAIRS) for h in range(2)]
    a_m, a_ab, a_ak = {}, {}, {}
    for i, (p, h) in enumerate(heads):
        head = (lane // HEAD_DIM) == h
        a_m[p, h] = jnp.where(head, am_ref[slot, :, sls[p]].astype(F32), 0.0).astype(BF16)
        r_m = jnp.where(head, rm_ref[slot, :, sls[p]].astype(F32), 0.0).astype(BF16)
        rhs = jnp.concatenate([bt_ref[slot, sls[p], :], kt_ref[slot, sls[p], :]], axis=1)
        big = jnp.dot(jnp.concatenate([a_m[p, h], r_m], axis=0), rhs,
                      preferred_element_type=F32)
        a_ab[p, h] = jnp.where(strict, big[0:c, 0:c], 0.0)
        a_ak[p, h] = jnp.where(strict, big[0:c, c:2 * c], 0.0).astype(BF16)
        lhs_ref[mid_slot, i] = jnp.concatenate(
            [r_m, jnp.where(incl, big[c:2 * c, 0:c], 0.0).astype(BF16),
             jnp.where(incl, big[c:2 * c, c:2 * c], 0.0).astype(BF16)], axis=1)
    yield
    t_inv = {hd: eye + a_ab[hd] for hd in heads}
    m_pow = {hd: _dot(a_ab[hd], a_ab[hd]) for hd in heads}
    yield
    for k in range(1, 7):
        skip = (2 ** k) // 16 * 16
        last = k == 6
        for hd in heads:
            rhs = t_inv[hd] if last else jnp.concatenate([m_pow[hd], t_inv[hd]], axis=1)
            upd = _dot(m_pow[hd][skip:c, :], rhs)
            if skip:
                upd = jnp.concatenate([jnp.zeros((skip, upd.shape[1]), F32), upd], axis=0)
            if last:
                t_inv[hd] = t_inv[hd] + upd
            else:
                m_pow[hd] = upd[:, 0:c]
                t_inv[hd] = t_inv[hd] + upd[:, c:2 * c]
        yield
    ak_v = {(p, h): _dot(a_ak[p, h], vb_ref[slot, :, sls[p]]) for (p, h) in heads}
    yield
    for i, hd in enumerate(heads):
        pq_ref[mid_slot, i] = _dot(t_inv[hd],
                               jnp.concatenate([a_m[hd].astype(F32), ak_v[hd]], axis=1))


def _rwkv_chunk_state(slot, mid_slot, prep, mid, vec_ref, seg_ref, z_ref, out_ref, rows):
    vb_ref, bh_ref, kh_ref, dend_ref, g_ref, bv_ref = (prep[i] for i in (2, 5, 6, 7, 8, 9))
    pq_ref, lhs_ref = mid
    c = CHUNK
    ln_w, ln_b = vec_ref[5:6, :], vec_ref[6:7, :]
    lane = lax.broadcasted_iota(jnp.int32, (1, LANES), 1)
    row = lax.broadcasted_iota(jnp.int32, (c, c), 0)
    col = lax.broadcasted_iota(jnp.int32, (c, c), 1)
    eye = (col == row).astype(F32)
    same_head = (row // HEAD_DIM) == (col // HEAD_DIM)
    pairs = range(N_PAIRS)
    sls = [slice(p * PAIR, (p + 1) * PAIR) for p in pairs]
    heads = [(p, h) for p in pairs for h in range(2)]
    v_b = [vb_ref[slot, :, sl] for sl in sls]
    z_f = [z_ref[p] for p in pairs]
    z_b = [z.astype(BF16) for z in z_f]

    u = {(p, h): (_dot(pq_ref[mid_slot, i, :, 0:PAIR], z_b[p])
                  + pq_ref[mid_slot, i, :, PAIR:2 * PAIR])
         for i, (p, h) in enumerate(heads)}
    yield
    y_hd = {(p, h): jnp.dot(
        lhs_ref[mid_slot, i], jnp.concatenate([z_b[p], u[p, h].astype(BF16), v_b[p]], axis=0),
        preferred_element_type=F32) for i, (p, h) in enumerate(heads)}
    first = lane < HEAD_DIM
    y_pairs = []
    for p in pairs:
        u_p = jnp.where(first, u[p, 0], u[p, 1])
        y_pairs.append(jnp.where(first, y_hd[p, 0], y_hd[p, 1]))
        d_col = jnp.sum(eye * dend_ref[slot, 0:1, sls[p]], axis=1, keepdims=True)
        z_new = d_col * z_f[p] + jnp.dot(
            jnp.concatenate([bh_ref[slot, sls[p], :], kh_ref[slot, sls[p], :]], axis=1),
            jnp.concatenate([u_p.astype(BF16), v_b[p]], axis=0), preferred_element_type=F32)
        z_ref[p] = jnp.where(same_head, z_new, 0.0)
    yield

    seg = seg_ref[...]
    y = jnp.concatenate(y_pairs, axis=1)
    mean = _head_sums(y, seg) * (1.0 / HEAD_DIM)
    yc = y - mean
    var = _head_sums(yc * yc, seg) * (1.0 / HEAD_DIM)
    y = yc * lax.rsqrt(var + GN_EPS) * ln_w + ln_b
    out_ref[0, rows, :] = ((y + bv_ref[slot]) * g_ref[slot]).astype(BF16)


def _interleave(*stages, late=()):
    live = list(stages)
    late = list(late)
    while live:
        for gen in list(live):
            try:
                next(gen)
            except StopIteration:
                live.remove(gen)
                if late and gen is stages[0]:
                    live.extend(late)
                    late = []


def _in_turn(*stages):
    for gen in stages:
        yield from gen


def _rwkv_kernel(rkv_ref, small_ref, rkv_next_ref, small_next_ref,
                 w_lr_ref, w_gate_ref, vec_ref, seg_ref, out_ref, z_ref, pq_ref, lhs_ref, *prep):
    c = CHUNK
    params = (w_lr_ref, w_gate_ref, vec_ref, seg_ref)
    mid = (pq_ref, lhs_ref)
    rows = [slice(i * c, (i + 1) * c) for i in range(RWKV_STEP_CHUNKS)]

    def prepare(r_ref, s_ref, i, slot):
        return _rwkv_prepare(r_ref[0, rows[i], :], s_ref[0, rows[i], :], *params, slot, prep)

    def matrices(i):
        return _rwkv_chunk_matrices(i, i % RWKV_MID_SLOTS, prep, mid)

    def state(i):
        return _rwkv_chunk_state(i, i % RWKV_MID_SLOTS, prep, mid, vec_ref, seg_ref, z_ref,
                                 out_ref, rows[i])

    @pl.when(pl.program_id(1) == 0)
    def _():
        z_ref[...] = jnp.zeros_like(z_ref)
        _interleave(prepare(rkv_ref, small_ref, 0, 0), prepare(rkv_ref, small_ref, 1, 1))

    n_pairs = RWKV_STEP_CHUNKS // 2
    for j in range(n_pairs):
        a, b = 2 * j, 2 * j + 1
        streams = [matrices(a), matrices(b)]
        if j > 0:
            streams.insert(0, _in_turn(state(a - 2), state(b - 2)))
        if j + 1 < n_pairs:
            ahead = [prepare(rkv_ref, small_ref, a + 2, a + 2),
                     prepare(rkv_ref, small_ref, b + 2, b + 2)]
        else:
            ahead = [prepare(rkv_next_ref, small_next_ref, 0, 0),
                     prepare(rkv_next_ref, small_next_ref, 1, 1)]
        if ahead and j == 1 and j + 1 == n_pairs:
            _interleave(*streams, late=ahead)
        else:
            _interleave(*streams, *ahead)
    _interleave(_in_turn(state(RWKV_STEP_CHUNKS - 2), state(RWKV_STEP_CHUNKS - 1)))


def _rwkv_mix(rkv, small, w_lr, w_gate, vecs, seg):
    batch, seq, _ = rkv.shape
    c = CHUNK
    n = RWKV_STEP_CHUNKS
    n_pairs_total = seq // (2 * c)
    grid = (batch, seq // (n * c))
    const = lambda b, s: (0, 0)
    tile = lambda b, s: (b, s, 0)
    ahead = lambda b, s: (b, jnp.minimum(n // 2 * (s + 1), n_pairs_total - 1), 0)
    slots = lambda shape, dtype: pltpu.VMEM((n,) + shape, dtype)
    mid_slots = lambda shape, dtype: pltpu.VMEM((RWKV_MID_SLOTS,) + shape, dtype)
    return pl.pallas_call(
        _rwkv_kernel,
        grid=grid,
        in_specs=[pl.BlockSpec((1, n * c, N_RKV), tile),
                  pl.BlockSpec((1, n * c, N_SMALL), tile),
                  pl.BlockSpec((1, 2 * c, N_RKV), ahead),
                  pl.BlockSpec((1, 2 * c, N_SMALL), ahead),
                  pl.BlockSpec(w_lr.shape, const),
                  pl.BlockSpec(w_gate.shape, const),
                  pl.BlockSpec(vecs.shape, const),
                  pl.BlockSpec(seg.shape, const)],
        out_specs=pl.BlockSpec((1, n * c, WIDTH), tile),
        out_shape=jax.ShapeDtypeStruct((batch, seq, WIDTH), BF16),
        scratch_shapes=[pltpu.VMEM((N_PAIRS, PAIR, PAIR), F32),
                        mid_slots((HEADS, c, 2 * PAIR), F32),
                        mid_slots((HEADS, c, 3 * PAIR), BF16),
                        slots((c, WIDTH), BF16),
                        slots((c, WIDTH), BF16),
                        slots((c, WIDTH), BF16),
                        slots((WIDTH, c), BF16),
                        slots((WIDTH, c), BF16),
                        slots((WIDTH, c), BF16),
                        slots((WIDTH, c), BF16),
                        slots((8, WIDTH), F32),
                        slots((c, WIDTH), F32),
                        slots((c, WIDTH), F32)],
        compiler_params=pltpu.CompilerParams(dimension_semantics=("arbitrary", "arbitrary"),
                                             vmem_limit_bytes=VMEM_LIMIT_BYTES),
        name="rwkv",
    )(rkv, small, rkv, small, w_lr, w_gate, vecs, seg)


def _fox_kernel(q_ref, k_ref, vt_ref, cum_ref, cumt_ref, mask_ref, route_ref, o_ref,
                qx_ref, kx_ref, vx_ref, t_ref, p_ref):
    heads = range(FOX_STEP_HEADS)
    head0 = FOX_STEP_HEADS * pl.program_id(1)
    blk = FOX_BLOCK
    seq = k_ref.shape[1]
    n_blk = seq // blk
    lane = lax.broadcasted_iota(jnp.int32, (1, LANES), 1)
    own = [(lane // HEAD_DIM) == h for h in range(2)]
    bias_lane = [HEAD_DIM * (1 - h) for h in range(2)]

    def fill(i, _):
        rows = pl.ds(pl.multiple_of(i * blk, blk), blk)
        parts = jnp.concatenate(_bf16_parts(cum_ref[0, rows, :] * (-LOG2E), 3), axis=1)
        lane_b = lax.broadcasted_iota(jnp.int32, (blk, LANES), 1)
        feats = jnp.dot(parts, route_ref[0], preferred_element_type=F32).astype(BF16)
        for pp in range(FOX_STEP_HEADS // 2):
            group = slice(pp * PAIR, (pp + 1) * PAIR)
            feat = feats[:, group]
            k_rows = k_ref[0, rows, group]
            q_rows = q_ref[0, rows, group]
            for h in range(2):
                own_b = (lane_b // HEAD_DIM) == h
                is_bias = (lane_b >= bias_lane[h]) & (lane_b < bias_lane[h] + 3)
                kx_ref[2 * pp + h, rows, :] = jnp.where(own_b, k_rows, feat)
                qx_ref[2 * pp + h, rows, :] = jnp.where(
                    own_b, q_rows, jnp.where(is_bias, 1.0, 0.0).astype(BF16))
        return 0
    lax.fori_loop(0, n_blk, fill, 0)
    ones_row = lax.broadcasted_iota(jnp.int32, (FOX_VROWS - HEAD_DIM, seq), 0) == 0
    for h in heads:
        vx_ref[h, 0:HEAD_DIM, :] = vt_ref[0, h * HEAD_DIM:(h + 1) * HEAD_DIM, :]
        vx_ref[h, HEAD_DIM:FOX_VROWS, :] = ones_row.astype(BF16)

    def rows_of(b):
        return pl.ds(pl.multiple_of(b * blk, blk), blk)

    def following(qi, kj):
        wrap = kj == qi
        return jnp.where(wrap, qi + 1, qi), jnp.where(wrap, 0, kj + 1)

    def scores(pair, slot, out):
        qi, kj = pair
        qi = jnp.minimum(qi, n_blk - 1)
        causal = mask_ref[(kj == qi).astype(jnp.int32)]
        mx = []
        for h in heads:
            t = _dot_nt(kx_ref[h, rows_of(kj), :], qx_ref[h, rows_of(qi), :]) + causal
            t_ref[slot, h] = t
            c_q = cumt_ref[0, pl.ds(head0 + h, 1), rows_of(qi)] * LOG2E
            mx.append(jnp.max(t, axis=0, keepdims=True) + c_q)
            yield
        out["mx"] = tuple(mx)

    def softmax(pair, slot, m, mx, out):
        qi, kj = pair
        sub = FOX_SUB
        q_rows = rows_of(jnp.minimum(qi, n_blk - 1))
        m_out, alpha_out = [], []
        for h in heads:
            m_old = jnp.where(kj == 0, -jnp.inf, m[h])
            c_q = cumt_ref[0, pl.ds(head0 + h, 1), q_rows] * LOG2E
            m_new = jnp.maximum(m_old, mx[h])
            shift = c_q - m_new
            for i in range(blk // sub):
                rows = slice(i * sub, (i + 1) * sub)
                p_ref[slot, h, rows, :] = jnp.exp2(t_ref[slot, h, rows, :] + shift).astype(BF16)
                if i % 2 == 1:
                    yield
            m_out.append(m_new)
            alpha_out.append(jnp.exp2(m_old - m_new))
        out["m"], out["alpha"] = tuple(m_out), tuple(alpha_out)

    def accumulate(pair, slot, alpha, acc, out):
        qi, kj = pair
        new = []
        for h in heads:
            new.append(alpha[h] * acc[h] + jnp.dot(vx_ref[h, :, rows_of(kj)], p_ref[slot, h],
                                                   preferred_element_type=F32))
            yield
        o_t = jnp.concatenate([a[0:HEAD_DIM] / a[HEAD_DIM:HEAD_DIM + 1] for a in new], axis=0)
        o_ref[0, :, rows_of(qi)] = o_t.astype(BF16)
        out["acc"] = tuple(new)

    def step(pair_s, pair_p, pair_a, slot_s, m, mx, alpha, acc):
        out = {}
        _interleave(softmax(pair_p, 1 - slot_s, m, mx, out),
                    scores(pair_s, slot_s, out),
                    accumulate(pair_a, slot_s, alpha, acc, out))
        return out

    def two_pairs(_, carry):
        pair0, pair1, m, mx1, alpha0, acc = carry
        pair2 = following(*pair1)
        a = step(pair2, pair1, pair0, 0, m, mx1, alpha0, acc)
        pair3 = following(*pair2)
        b = step(pair3, pair2, pair1, 1, a["m"], a["mx"], a["alpha"], a["acc"])
        return pair2, pair3, b["m"], b["mx"], b["alpha"], b["acc"]

    n_pairs = n_blk * (n_blk + 1) // 2
    assert n_pairs % 2 == 0, "two pairs per trip"
    zero = jnp.int32(0)
    pair0 = (zero, zero)
    pair1 = following(*pair0)
    first, second = {}, {}
    _interleave(scores(pair0, 0, first))
    _interleave(scores(pair1, 1, second))
    m = tuple(jnp.full((1, blk), -jnp.inf, F32) for _ in heads)
    _interleave(softmax(pair0, 0, m, first["mx"], first))
    acc = tuple(jnp.zeros((FOX_VROWS, blk), F32) for _ in heads)
    lax.fori_loop(0, n_pairs // 2, two_pairs,
                  (pair0, pair1, first["m"], second["mx"], first["alpha"], acc))


def _forgetting_attention(qk, v_t, cum, cum_t):
    batch, seq, _ = qk.shape
    blk = FOX_BLOCK
    key = lax.broadcasted_iota(jnp.int32, (blk, blk), 0)
    qry = lax.broadcasted_iota(jnp.int32, (blk, blk), 1)
    mask = jnp.stack([jnp.zeros((blk, blk), F32), jnp.where(key <= qry, 0.0, FOX_MASKED)])
    pair = jnp.arange(N_PAIRS)[:, None, None]
    src = jnp.arange(3 * LANES)[None, :, None]
    dst = jnp.arange(LANES)[None, None, :]
    route = jnp.zeros((N_PAIRS, 3 * LANES, LANES), jnp.bool_)
    for h in range(2):
        for n in range(3):
            route |= (src == n * LANES + 2 * pair + h) & (dst == HEAD_DIM * (1 - h) + n)
    nh = FOX_STEP_HEADS
    width = nh * HEAD_DIM
    route = route.astype(BF16).reshape(N_PAIRS * 2 // nh, nh // 2, 3 * LANES, LANES)
    route = jnp.transpose(route, (0, 2, 1, 3)).reshape(N_PAIRS * 2 // nh, 3 * LANES, width)
    whole = lambda b, g: (b, 0, 0)
    n_groups = HEADS // nh
    return pl.pallas_call(
        _fox_kernel,
        grid=(batch, n_groups),
        in_specs=[pl.BlockSpec((1, seq, width), lambda b, g: (b, 0, g)),
                  pl.BlockSpec((1, seq, width), lambda b, g: (b, 0, n_groups + g)),
                  pl.BlockSpec((1, width, seq), lambda b, g: (b, g, 0)),
                  pl.BlockSpec((1, seq, LANES), whole),
                  pl.BlockSpec((1, HEADS, seq), whole),
                  pl.BlockSpec((2, blk, blk), lambda b, g: (0, 0, 0)),
                  pl.BlockSpec((1, 3 * LANES, width), lambda b, g: (g, 0, 0))],
        out_specs=pl.BlockSpec((1, width, seq), lambda b, g: (b, g, 0)),
        out_shape=jax.ShapeDtypeStruct((batch, WIDTH, seq), BF16),
        scratch_shapes=[pltpu.VMEM((nh, seq, LANES), BF16),
                        pltpu.VMEM((nh, seq, LANES), BF16),
                        pltpu.VMEM((nh, FOX_VROWS, seq), BF16),
                        pltpu.VMEM((2, nh, blk, blk), F32),
                        pltpu.VMEM((2, nh, blk, blk), BF16)],
        compiler_params=pltpu.CompilerParams(
            dimension_semantics=("arbitrary", "arbitrary"),
            vmem_limit_bytes=VMEM_LIMIT_BYTES),
        name="fox",
    )(qk, qk, v_t, cum, cum_t, mask, route)


def _tail_kernel(x_ref, ya_ref, ybt_ref, gate_ref, mod_ref, g2_ref, gf_ref,
                 woa_ref, wob_ref, wout_ref, w1_ref, w2_ref, o_ref):
    x = x_ref[0]
    gate1 = mod_ref[0, 2:3, :]
    shift2 = mod_ref[0, 3:4, :]
    scale2 = mod_ref[0, 4:5, :]
    gate2 = mod_ref[0, 5:6, :]

    merged = (gate_ref[0, :, 0:D_MODEL].astype(F32)
              * jnp.dot(ya_ref[0], woa_ref[...], preferred_element_type=F32)
              + gate_ref[0, :, D_MODEL:N_GATE].astype(F32)
              * lax.dot_general(ybt_ref[0], wob_ref[...], (((0,), (0,)), ((), ())),
                                preferred_element_type=F32))
    x = x + gate1 * _dot(merged, wout_ref[...])

    inv = lax.rsqrt(jnp.mean(x * x, axis=-1, keepdims=True) + NORM_EPS)
    h2 = ((x * inv) * g2_ref[...] * (1.0 + scale2) + shift2).astype(BF16)
    ff = jnp.zeros_like(x)
    for j in range(D_FF // FF_CHUNK):
        cols = slice(j * FF_CHUNK, (j + 1) * FF_CHUNK)
        hid = jnp.maximum(jnp.dot(h2, w1_ref[:, cols], preferred_element_type=F32), 0.0)
        ff = ff + _dot(hid * hid, w2_ref[cols, :])
    x = x + gate2 * ff

    inv = lax.rsqrt(jnp.mean(x * x, axis=-1, keepdims=True) + NORM_EPS)
    o_ref[0] = (x * inv) * gf_ref[...]


def _tail(x, y_a, y_b, gates, mod3, norm2_g, final_g, w_oa, w_ob, w_out, w_ff1, w_ff2):
    batch, seq, _ = x.shape
    tm = TAIL_ROWS
    grid = (batch, seq // tm)
    const = lambda b, s: (0, 0)
    tile = lambda b, s: (b, s, 0)
    resident = lambda a: pl.BlockSpec(a.shape, const, pipeline_mode=pl.Buffered(1))
    return pl.pallas_call(
        _tail_kernel,
        grid=grid,
        in_specs=[pl.BlockSpec((1, tm, D_MODEL), tile),
                  pl.BlockSpec((1, tm, WIDTH), tile),
                  pl.BlockSpec((1, WIDTH, tm), lambda b, s: (b, 0, s)),
                  pl.BlockSpec((1, tm, N_GATE), tile),
                  pl.BlockSpec((1, N_MOD, D_MODEL), lambda b, s: (b, 0, 0)),
                  pl.BlockSpec((1, D_MODEL), const),
                  pl.BlockSpec((1, D_MODEL), const),
                  resident(w_oa), resident(w_ob), resident(w_out),
                  resident(w_ff1), resident(w_ff2)],
        out_specs=pl.BlockSpec((1, tm, D_MODEL), tile),
        out_shape=jax.ShapeDtypeStruct((batch, seq, D_MODEL), F32),
        compiler_params=pltpu.CompilerParams(dimension_semantics=("arbitrary", "arbitrary"),
                                             vmem_limit_bytes=VMEM_LIMIT_BYTES),
        name="tail",
    )(x, y_a, y_b, gates, mod3, norm2_g, final_g, w_oa, w_ob, w_out, w_ff1, w_ff2)


def _reorder_rwkv_cols(t):
    o = 0
    r = t[..., o:o + WIDTH]; o += WIDTH
    wd = t[..., o:o + DECAY_RANK]; o += DECAY_RANK
    k = t[..., o:o + WIDTH]; o += WIDTH
    v = t[..., o:o + WIDTH]; o += WIDTH
    ad = t[..., o:o + ICLR_RANK]; o += ICLR_RANK
    gd = t[..., o:o + GATE_RANK]
    return jnp.concatenate([r, k, v, wd, ad, gd], axis=-1)


def kernel(x, c, w_ada, b_ada, norm1_g, w_in, mu_shift, w_decay_up, decay_base, w_iclr_up, iclr_base, w_gate_up, kk_scale, k_iclr_mix, r_bonus, lnx_w, lnx_b, fox_f_bias, w_o_rwkv, w_o_fox, w_out, norm2_g, w_ff1, w_ff2, final_g):
    assert w_ada.shape[0] == 1, "the tail kernel fuses the final norm: single layer only"
    l = 0
    n_rwkv = N_RKV + N_SMALL
    seg_id = jnp.arange(WIDTH // 2) // HEAD_DIM
    seg = (seg_id[:, None] == seg_id[None, :]).astype(BF16)

    mod3 = _modulation(c, w_ada[l], b_ada[l]).reshape(-1, N_MOD, D_MODEL)

    w = w_in[l].astype(BF16)
    o_wd, o_k, o_v, o_ad = WIDTH, WIDTH + DECAY_RANK, 2 * WIDTH + DECAY_RANK, 3 * WIDTH + DECAY_RANK
    o_ff = n_rwkv + N_FOX
    weights = [w[:, 0:o_wd], w[:, o_k:o_v], w[:, o_v:o_ad],
               jnp.concatenate([w[:, o_wd:o_k], w[:, o_ad:n_rwkv]], axis=1),
               w[:, n_rwkv:o_ff],
               jnp.pad(w[:, o_ff:o_ff + HEADS], ((0, 0), (0, LANES - HEADS))),
               w[:, o_ff + HEADS:]]
    mu = _reorder_rwkv_cols(mu_shift[l]).reshape(1, n_rwkv)
    f_bias = jnp.pad(fox_f_bias[l], (0, LANES - HEADS)).reshape(1, LANES)
    rkv, small, qk, v_t, cum, cum_t, gates = _projection(
        x, mod3, norm1_g[l].reshape(1, D_MODEL), weights, mu, f_bias)

    zeros = jnp.zeros((DECAY_RANK, WIDTH), F32)
    w_lr = jnp.concatenate(
        [jnp.concatenate([w_decay_up[l], zeros], axis=1),
         jnp.concatenate([zeros, w_iclr_up[l]], axis=1)], axis=0).astype(BF16)
    vecs = jnp.stack([decay_base[l], iclr_base[l], kk_scale[l], k_iclr_mix[l],
                      r_bonus[l].reshape(WIDTH), lnx_w[l], lnx_b[l],
                      jnp.zeros((WIDTH,), F32)], axis=0)
    y_a = _rwkv_mix(rkv, small, w_lr, w_gate_up[l].astype(BF16), vecs, seg)

    y_b = _forgetting_attention(qk, v_t, cum, cum_t)

    return _tail(x, y_a, y_b, gates, mod3, norm2_g[l].reshape(1, D_MODEL),
                 final_g.reshape(1, D_MODEL),
                 w_o_rwkv[l].astype(BF16), w_o_fox[l].astype(BF16), w_out[l].astype(BF16),
                 w_ff1[l].astype(BF16), w_ff2[l].astype(BF16))
```

```python
import jax
import jax.numpy as jnp
from jax import lax
from jax.experimental import pallas as pl
from jax.experimental.pallas import tpu as pltpu

F32 = jnp.float32
BF16 = jnp.bfloat16

D_MODEL = 1024
HEAD_DIM = 64
HEADS = 8
WIDTH = HEADS * HEAD_DIM
DECAY_RANK = 64
ICLR_RANK = 64
GATE_RANK = 128
D_FF = 4 * D_MODEL
N_MOD = 6
NORM_EPS = 1e-6
GN_EPS = 64e-5

LANES = 128
PAIR = 2 * HEAD_DIM
N_PAIRS = HEADS // 2

LOG2E = 1.4426950408889634

N_SMALL = DECAY_RANK + ICLR_RANK + GATE_RANK
N_RKV = 3 * WIDTH
N_FOX = 3 * WIDTH
N_GATE = 2 * D_MODEL

PROJ_ROWS = 256
CHUNK = 128
RWKV_STEP_CHUNKS = 4
RWKV_MID_SLOTS = 4
FOX_BLOCK = 512
FOX_SUB = 32
FOX_VROWS = HEAD_DIM + 16
FOX_STEP_HEADS = 4
FOX_MASKED = -1e30
TAIL_ROWS = 512
FF_CHUNK = 1024

VMEM_LIMIT_BYTES = 56 * 1024 * 1024


def _dot(a, b):
    return jnp.dot(a.astype(BF16), b.astype(BF16), preferred_element_type=F32)


def _dot_nt(a, b):
    return lax.dot_general(a.astype(BF16), b.astype(BF16), (((1,), (1,)), ((), ())),
                           preferred_element_type=F32)


def _softplus(z):
    return jnp.maximum(z, 0.0) + jnp.log(1.0 + jnp.exp(-jnp.abs(z)))


def _sigmoid(z):
    return 1.0 / (1.0 + jnp.exp(-z))


def _bf16_parts(x, n):
    parts = []
    for _ in range(n):
        p = x.astype(BF16)
        parts.append(p)
        x = x - p.astype(F32)
    return parts


def _cumsum_rows(tri, x, n_parts):
    return sum(jnp.dot(tri, p, preferred_element_type=F32) for p in _bf16_parts(x, n_parts))


def _head_sums(x, seg):
    half = seg.shape[0]
    return jnp.concatenate([_dot(x[:, 0:half], seg), _dot(x[:, half:2 * half], seg)], axis=1)


def _lower_tri(n, dtype):
    row = lax.broadcasted_iota(jnp.int32, (n, n), 0)
    col = lax.broadcasted_iota(jnp.int32, (n, n), 1)
    return (col <= row).astype(dtype)


def _mod_kernel(c_ref, w_ref, b_ref, o_ref):
    c = c_ref[...]
    c_act = c * _sigmoid(c)
    o_ref[...] = _dot(c_act, w_ref[...]) + b_ref[...]


def _modulation(c, w_ada, b_ada):
    batch = c.shape[0]
    n = w_ada.shape[1]
    tn = 1536
    return pl.pallas_call(
        _mod_kernel,
        grid=(n // tn,),
        in_specs=[pl.BlockSpec((batch, D_MODEL), lambda j: (0, 0)),
                  pl.BlockSpec((D_MODEL, tn), lambda j: (0, j)),
                  pl.BlockSpec((1, tn), lambda j: (0, j))],
        out_specs=pl.BlockSpec((batch, tn), lambda j: (0, j)),
        out_shape=jax.ShapeDtypeStruct((batch, n), F32),
        compiler_params=pltpu.CompilerParams(dimension_semantics=("arbitrary",),
                                             vmem_limit_bytes=VMEM_LIMIT_BYTES),
        name="mod",
    )(c, w_ada, b_ada.reshape(1, n))


def _proj_normalise(x, mod_ref, g_ref):
    shift1 = mod_ref[0, 0:1, :]
    scale1 = mod_ref[0, 1:2, :]
    inv = lax.rsqrt(jnp.mean(x * x, axis=-1, keepdims=True) + NORM_EPS)
    return ((x * inv) * (g_ref[...] * (1.0 + scale1)) + shift1).astype(BF16)


def _proj_tile(h, rows, weights, mu_ref, fb_ref, outs, shift_ref, carry_ref):
    wr_ref, wk_ref, wv_ref, ws_ref, wfox_ref, wff_ref, wgate_ref = weights
    rkv_ref, small_ref, qk_ref, vt_ref, cum_ref, cumt_ref, gate_ref = outs
    n = h.shape[0]

    n_tok = N_RKV + N_SMALL
    p = jnp.concatenate([jnp.dot(h, w[...], preferred_element_type=F32)
                         for w in (wr_ref, wk_ref, wv_ref, ws_ref)], axis=1)
    shift_ref[8:8 + n, :] = p
    prev = shift_ref[7:7 + n, :]
    shift_ref[7:8, :] = p[n - 1:n, :]
    mixed = p + mu_ref[...] * (prev - p)
    rkv_ref[0, rows, :] = mixed[:, 0:N_RKV].astype(BF16)
    small_ref[0, rows, :] = mixed[:, N_RKV:n_tok]

    pf = jnp.dot(h, wfox_ref[...], preferred_element_type=F32)
    qk_ref[0, rows, 0:WIDTH] = (pf[:, 0:WIDTH] * (LOG2E * HEAD_DIM ** -0.5)).astype(BF16)
    qk_ref[0, rows, WIDTH:2 * WIDTH] = pf[:, WIDTH:2 * WIDTH].astype(BF16)
    vt_ref[0, :, rows] = jnp.transpose(pf[:, 2 * WIDTH:N_FOX]).astype(BF16)

    ff = jnp.dot(h, wff_ref[...], preferred_element_type=F32)
    logf = -_softplus(-(ff + fb_ref[...]))
    cum = _cumsum_rows(_lower_tri(n, BF16), logf, 3) + carry_ref[...]
    cum_ref[0, rows, :] = cum
    cumt_ref[0, :, rows] = jnp.transpose(cum)[0:HEADS, :]
    carry_ref[...] = cum[n - 1:n, :]

    pg = jnp.dot(h, wgate_ref[...], preferred_element_type=F32)
    gate_ref[0, rows, :] = _sigmoid(pg).astype(BF16)


def _proj_kernel(x_ref, x_next_ref, mod_ref, g_ref, wr_ref, wk_ref, wv_ref, ws_ref, wfox_ref,
                 wff_ref, wgate_ref, mu_ref, fb_ref,
                 rkv_ref, small_ref, qk_ref, vt_ref, cum_ref, cumt_ref, gate_ref,
                 shift_ref, carry_ref, h_ref):
    tm = PROJ_ROWS
    weights = (wr_ref, wk_ref, wv_ref, ws_ref, wfox_ref, wff_ref, wgate_ref)
    outs = (rkv_ref, small_ref, qk_ref, vt_ref, cum_ref, cumt_ref, gate_ref)

    @pl.when(pl.program_id(1) == 0)
    def _():
        shift_ref[0:8, :] = jnp.zeros((8, N_RKV + N_SMALL), F32)
        carry_ref[...] = jnp.zeros_like(carry_ref)
        h_ref[0] = _proj_normalise(x_ref[0, 0:tm, :], mod_ref, g_ref)

    h_ref[1] = _proj_normalise(x_ref[0, tm:2 * tm, :], mod_ref, g_ref)
    _proj_tile(h_ref[0], slice(0, tm), weights, mu_ref, fb_ref, outs, shift_ref, carry_ref)
    h_ref[0] = _proj_normalise(x_next_ref[0], mod_ref, g_ref)
    _proj_tile(h_ref[1], slice(tm, 2 * tm), weights, mu_ref, fb_ref, outs, shift_ref, carry_ref)


def _projection(x, mod3, norm_g, weights, mu, f_bias):
    batch, seq, _ = x.shape
    tm = PROJ_ROWS
    n_tiles = seq // tm
    grid = (batch, n_tiles // 2)
    const = lambda b, s: (0, 0)
    tile = lambda b, s: (b, s, 0)
    lanes = lambda b, s: (b, 0, s)
    ahead = lambda b, s: (b, jnp.minimum(2 * s + 2, n_tiles - 1), 0)
    return pl.pallas_call(
        _proj_kernel,
        grid=grid,
        in_specs=[pl.BlockSpec((1, 2 * tm, D_MODEL), tile),
                  pl.BlockSpec((1, tm, D_MODEL), ahead),
                  pl.BlockSpec((1, N_MOD, D_MODEL), lambda b, s: (b, 0, 0)),
                  pl.BlockSpec((1, D_MODEL), const),
                  *[pl.BlockSpec(w.shape, const, pipeline_mode=pl.Buffered(1)) for w in weights],
                  pl.BlockSpec((1, N_RKV + N_SMALL), const),
                  pl.BlockSpec((1, LANES), const)],
        out_specs=[pl.BlockSpec((1, 2 * tm, N_RKV), tile),
                   pl.BlockSpec((1, 2 * tm, N_SMALL), tile),
                   pl.BlockSpec((1, 2 * tm, 2 * WIDTH), tile),
                   pl.BlockSpec((1, WIDTH, 2 * tm), lanes),
                   pl.BlockSpec((1, 2 * tm, LANES), tile),
                   pl.BlockSpec((1, HEADS, 2 * tm), lanes),
                   pl.BlockSpec((1, 2 * tm, N_GATE), tile)],
        out_shape=[jax.ShapeDtypeStruct((batch, seq, N_RKV), BF16),
                   jax.ShapeDtypeStruct((batch, seq, N_SMALL), F32),
                   jax.ShapeDtypeStruct((batch, seq, 2 * WIDTH), BF16),
                   jax.ShapeDtypeStruct((batch, WIDTH, seq), BF16),
                   jax.ShapeDtypeStruct((batch, seq, LANES), F32),
                   jax.ShapeDtypeStruct((batch, HEADS, seq), F32),
                   jax.ShapeDtypeStruct((batch, seq, N_GATE), BF16)],
        scratch_shapes=[pltpu.VMEM((tm + 8, N_RKV + N_SMALL), F32),
                        pltpu.VMEM((1, LANES), F32),
                        pltpu.VMEM((2, tm, D_MODEL), BF16)],
        compiler_params=pltpu.CompilerParams(dimension_semantics=("arbitrary", "arbitrary"),
                                             vmem_limit_bytes=VMEM_LIMIT_BYTES),
        name="proj",
    )(x, x, mod3, norm_g, *weights, mu, f_bias)


def _rwkv_prepare(rkv, small, w_lr_ref, w_gate_ref, vec_ref, seg_ref, slot, prep):
    am_ref, rm_ref, vb_ref, bt_ref, kt_ref, bh_ref, kh_ref, dend_ref, g_ref, bv_ref = prep
    c = CHUNK
    r = rkv[:, 0:WIDTH].astype(F32)
    k_raw = rkv[:, WIDTH:2 * WIDTH].astype(F32)
    v = rkv[:, 2 * WIDTH:3 * WIDTH].astype(F32)
    decay_base, iclr_base, kk_scale, k_mix, r_bonus = (vec_ref[i:i + 1, :] for i in range(5))

    lane = lax.broadcasted_iota(jnp.int32, (1, LANES), 1)
    lr_in = jnp.where(lane < DECAY_RANK, jnp.tanh(small[:, 0:LANES]), small[:, 0:LANES])
    lr = _dot(lr_in, w_lr_ref[...])
    w_log = -_softplus(-(decay_base + lr[:, 0:WIDTH])) - 0.5
    log_decay = -jnp.exp(w_log)
    a = _sigmoid(iclr_base + lr[:, WIDTH:2 * WIDTH])
    g_ref[slot] = _dot(_sigmoid(small[:, LANES:2 * LANES]), w_gate_ref[...])
    yield

    seg = seg_ref[...]
    kk = k_raw * kk_scale
    kk = kk * lax.rsqrt(jnp.maximum(_head_sums(kk * kk, seg), 1e-24))
    k = k_raw * (1.0 + (a - 1.0) * k_mix)
    b_vec = kk * a
    yield

    cs = _cumsum_rows(_lower_tri(c, BF16), log_decay, 2)
    cs_end = cs[c - 1:c, :]
    am_ref[slot] = (-kk * jnp.exp(cs - log_decay)).astype(BF16)
    rm_ref[slot] = (r * jnp.exp(cs)).astype(BF16)
    vb_ref[slot] = v.astype(BF16)
    yield
    w_inv = jnp.exp(-cs)
    bt_ref[slot] = jnp.transpose(b_vec * w_inv).astype(BF16)
    yield
    kt_ref[slot] = jnp.transpose(k * w_inv).astype(BF16)
    yield
    w_end = jnp.exp(cs_end - cs)
    bh_ref[slot] = jnp.transpose(b_vec * w_end).astype(BF16)
    yield
    kh_ref[slot] = jnp.transpose(k * w_end).astype(BF16)
    dend_ref[slot] = jnp.broadcast_to(jnp.exp(cs_end), (8, WIDTH))
    yield
    bv_ref[slot] = _head_sums(r * k * r_bonus, seg) * v


def _rwkv_chunk_matrices(slot, mid_slot, prep, mid):
    am_ref, rm_ref, vb_ref, bt_ref, kt_ref = prep[:5]
    pq_ref, lhs_ref = mid
    c = CHUNK
    lane = lax.broadcasted_iota(jnp.int32, (1, LANES), 1)
    row = lax.broadcasted_iota(jnp.int32, (c, c), 0)
    col = lax.broadcasted_iota(jnp.int32, (c, c), 1)
    strict = col < row
    incl = col <= row
    eye = (col == row).astype(F32)

    sls = [slice(p * PAIR, (p + 1) * PAIR) for p in range(N_PAIRS)]
    heads = [(p, h) for p in range(N_PAIRS) for h in range(2)]
    a_m, a_ab, a_ak = {}, {}, {}
    for i, (p, h) in enumerate(heads):
        head = (lane // HEAD_DIM) == h
        a_m[p, h] = jnp.where(head, am_ref[slot, :, sls[p]].astype(F32), 0.0).astype(BF16)
        r_m = jnp.where(head, rm_ref[slot, :, sls[p]].astype(F32), 0.0).astype(BF16)
        rhs = jnp.concatenate([bt_ref[slot, sls[p], :], kt_ref[slot, sls[p], :]], axis=1)
        big = jnp.dot(jnp.concatenate([a_m[p, h], r_m], axis=0), rhs,
                      preferred_element_type=F32)
        a_ab[p, h] = jnp.where(strict, big[0:c, 0:c], 0.0)
        a_ak[p, h] = jnp.where(strict, big[0:c, c:2 * c], 0.0).astype(BF16)
        lhs_ref[mid_slot, i] = jnp.concatenate(
            [r_m, jnp.where(incl, big[c:2 * c, 0:c], 0.0).astype(BF16),
             jnp.where(incl, big[c:2 * c, c:2 * c], 0.0).astype(BF16)], axis=1)
    yield
    t_inv = {hd: eye + a_ab[hd] for hd in heads}
    m_pow = {hd: _dot(a_ab[hd], a_ab[hd]) for hd in heads}
    yield
    for k in range(1, 7):
        skip = (2 ** k) // 16 * 16
        last = k == 6
        for hd in heads:
            rhs = t_inv[hd] if last else jnp.concatenate([m_pow[hd], t_inv[hd]], axis=1)
            upd = _dot(m_pow[hd][skip:c, :], rhs)
            if skip:
                upd = jnp.concatenate([jnp.zeros((skip, upd.shape[1]), F32), upd], axis=0)
            if last:
                t_inv[hd] = t_inv[hd] + upd
            else:
                m_pow[hd] = upd[:, 0:c]
                t_inv[hd] = t_inv[hd] + upd[:, c:2 * c]
        yield
    ak_v = {(p, h): _dot(a_ak[p, h], vb_ref[slot, :, sls[p]]) for (p, h) in heads}
    yield
    for i, hd in enumerate(heads):
        pq_ref[mid_slot, i] = _dot(t_inv[hd],
                               jnp.concatenate([a_m[hd].astype(F32), ak_v[hd]], axis=1))


def _rwkv_chunk_state(slot, mid_slot, prep, mid, vec_ref, seg_ref, z_ref, out_ref, rows):
    vb_ref, bh_ref, kh_ref, dend_ref, g_ref, bv_ref = (prep[i] for i in (2, 5, 6, 7, 8, 9))
    pq_ref, lhs_ref = mid
    c = CHUNK
    ln_w, ln_b = vec_ref[5:6, :], vec_ref[6:7, :]
    lane = lax.broadcasted_iota(jnp.int32, (1, LANES), 1)
    row = lax.broadcasted_iota(jnp.int32, (c, c), 0)
    col = lax.broadcasted_iota(jnp.int32, (c, c), 1)
    eye = (col == row).astype(F32)
    same_head = (row // HEAD_DIM) == (col // HEAD_DIM)
    pairs = range(N_PAIRS)
    sls = [slice(p * PAIR, (p + 1) * PAIR) for p in pairs]
    heads = [(p, h) for p in pairs for h in range(2)]
    v_b = [vb_ref[slot, :, sl] for sl in sls]
    z_f = [z_ref[p] for p in pairs]
    z_b = [z.astype(BF16) for z in z_f]

    u = {(p, h): (_dot(pq_ref[mid_slot, i, :, 0:PAIR], z_b[p])
                  + pq_ref[mid_slot, i, :, PAIR:2 * PAIR])
         for i, (p, h) in enumerate(heads)}
    yield
    y_hd = {(p, h): jnp.dot(
        lhs_ref[mid_slot, i], jnp.concatenate([z_b[p], u[p, h].astype(BF16), v_b[p]], axis=0),
        preferred_element_type=F32) for i, (p, h) in enumerate(heads)}
    first = lane < HEAD_DIM
    y_pairs = []
    for p in pairs:
        u_p = jnp.where(first, u[p, 0], u[p, 1])
        y_pairs.append(jnp.where(first, y_hd[p, 0], y_hd[p, 1]))
        d_col = jnp.sum(eye * dend_ref[slot, 0:1, sls[p]], axis=1, keepdims=True)
        z_new = d_col * z_f[p] + jnp.dot(
            jnp.concatenate([bh_ref[slot, sls[p], :], kh_ref[slot, sls[p], :]], axis=1),
            jnp.concatenate([u_p.astype(BF16), v_b[p]], axis=0), preferred_element_type=F32)
        z_ref[p] = jnp.where(same_head, z_new, 0.0)
    yield

    seg = seg_ref[...]
    y = jnp.concatenate(y_pairs, axis=1)
    mean = _head_sums(y, seg) * (1.0 / HEAD_DIM)
    yc = y - mean
    var = _head_sums(yc * yc, seg) * (1.0 / HEAD_DIM)
    y = yc * lax.rsqrt(var + GN_EPS) * ln_w + ln_b
    out_ref[0, rows, :] = ((y + bv_ref[slot]) * g_ref[slot]).astype(BF16)


def _interleave(*stages, late=()):
    live = list(stages)
    late = list(late)
    while live:
        for gen in list(live):
            try:
                next(gen)
            except StopIteration:
                live.remove(gen)
                if late and gen is stages[0]:
                    live.extend(late)
                    late = []


def _in_turn(*stages):
    for gen in stages:
        yield from gen


def _rwkv_kernel(rkv_ref, small_ref, rkv_next_ref, small_next_ref,
                 w_lr_ref, w_gate_ref, vec_ref, seg_ref, out_ref, z_ref, pq_ref, lhs_ref, *prep):
    c = CHUNK
    params = (w_lr_ref, w_gate_ref, vec_ref, seg_ref)
    mid = (pq_ref, lhs_ref)
    rows = [slice(i * c, (i + 1) * c) for i in range(RWKV_STEP_CHUNKS)]

    def prepare(r_ref, s_ref, i, slot):
        return _rwkv_prepare(r_ref[0, rows[i], :], s_ref[0, rows[i], :], *params, slot, prep)

    def matrices(i):
        return _rwkv_chunk_matrices(i, i % RWKV_MID_SLOTS, prep, mid)

    def state(i):
        return _rwkv_chunk_state(i, i % RWKV_MID_SLOTS, prep, mid, vec_ref, seg_ref, z_ref,
                                 out_ref, rows[i])

    @pl.when(pl.program_id(1) == 0)
    def _():
        z_ref[...] = jnp.zeros_like(z_ref)
        _interleave(prepare(rkv_ref, small_ref, 0, 0), prepare(rkv_ref, small_ref, 1, 1))

    n_pairs = RWKV_STEP_CHUNKS // 2
    for j in range(n_pairs):
        a, b = 2 * j, 2 * j + 1
        streams = [matrices(a), matrices(b)]
        if j > 0:
            streams.insert(0, _in_turn(state(a - 2), state(b - 2)))
        if j + 1 < n_pairs:
            ahead = [prepare(rkv_ref, small_ref, a + 2, a + 2),
                     prepare(rkv_ref, small_ref, b + 2, b + 2)]
        else:
            ahead = [prepare(rkv_next_ref, small_next_ref, 0, 0),
                     prepare(rkv_next_ref, small_next_ref, 1, 1)]
        if ahead and j == 1 and j + 1 == n_pairs:
            _interleave(*streams, late=ahead)
        else:
            _interleave(*streams, *ahead)
    _interleave(_in_turn(state(RWKV_STEP_CHUNKS - 2), state(RWKV_STEP_CHUNKS - 1)))


def _rwkv_mix(rkv, small, w_lr, w_gate, vecs, seg):
    batch, seq, _ = rkv.shape
    c = CHUNK
    n = RWKV_STEP_CHUNKS
    n_pairs_total = seq // (2 * c)
    grid = (batch, seq // (n * c))
    const = lambda b, s: (0, 0)
    tile = lambda b, s: (b, s, 0)
    ahead = lambda b, s: (b, jnp.minimum(n // 2 * (s + 1), n_pairs_total - 1), 0)
    slots = lambda shape, dtype: pltpu.VMEM((n,) + shape, dtype)
    mid_slots = lambda shape, dtype: pltpu.VMEM((RWKV_MID_SLOTS,) + shape, dtype)
    return pl.pallas_call(
        _rwkv_kernel,
        grid=grid,
        in_specs=[pl.BlockSpec((1, n * c, N_RKV), tile),
                  pl.BlockSpec((1, n * c, N_SMALL), tile),
                  pl.BlockSpec((1, 2 * c, N_RKV), ahead),
                  pl.BlockSpec((1, 2 * c, N_SMALL), ahead),
                  pl.BlockSpec(w_lr.shape, const),
                  pl.BlockSpec(w_gate.shape, const),
                  pl.BlockSpec(vecs.shape, const),
                  pl.BlockSpec(seg.shape, const)],
        out_specs=pl.BlockSpec((1, n * c, WIDTH), tile),
        out_shape=jax.ShapeDtypeStruct((batch, seq, WIDTH), BF16),
        scratch_shapes=[pltpu.VMEM((N_PAIRS, PAIR, PAIR), F32),
                        mid_slots((HEADS, c, 2 * PAIR), F32),
                        mid_slots((HEADS, c, 3 * PAIR), BF16),
                        slots((c, WIDTH), BF16),
                        slots((c, WIDTH), BF16),
                        slots((c, WIDTH), BF16),
                        slots((WIDTH, c), BF16),
                        slots((WIDTH, c), BF16),
                        slots((WIDTH, c), BF16),
                        slots((WIDTH, c), BF16),
                        slots((8, WIDTH), F32),
                        slots((c, WIDTH), F32),
                        slots((c, WIDTH), F32)],
        compiler_params=pltpu.CompilerParams(dimension_semantics=("arbitrary", "arbitrary"),
                                             vmem_limit_bytes=VMEM_LIMIT_BYTES),
        name="rwkv",
    )(rkv, small, rkv, small, w_lr, w_gate, vecs, seg)


def _fox_kernel(q_ref, k_ref, vt_ref, cum_ref, cumt_ref, mask_ref, route_ref, o_ref,
                qx_ref, kx_ref, vx_ref, t_ref, p_ref):
    heads = range(FOX_STEP_HEADS)
    head0 = FOX_STEP_HEADS * pl.program_id(1)
    blk = FOX_BLOCK
    seq = k_ref.shape[1]
    n_blk = seq // blk
    lane = lax.broadcasted_iota(jnp.int32, (1, LANES), 1)
    own = [(lane // HEAD_DIM) == h for h in range(2)]
    bias_lane = [HEAD_DIM * (1 - h) for h in range(2)]

    def fill(i, _):
        rows = pl.ds(pl.multiple_of(i * blk, blk), blk)
        parts = jnp.concatenate(_bf16_parts(cum_ref[0, rows, :] * (-LOG2E), 3), axis=1)
        lane_b = lax.broadcasted_iota(jnp.int32, (blk, LANES), 1)
        feats = jnp.dot(parts, route_ref[0], preferred_element_type=F32).astype(BF16)
        for pp in range(FOX_STEP_HEADS // 2):
            group = slice(pp * PAIR, (pp + 1) * PAIR)
            feat = feats[:, group]
            k_rows = k_ref[0, rows, group]
            q_rows = q_ref[0, rows, group]
            for h in range(2):
                own_b = (lane_b // HEAD_DIM) == h
                is_bias = (lane_b >= bias_lane[h]) & (lane_b < bias_lane[h] + 3)
                kx_ref[2 * pp + h, rows, :] = jnp.where(own_b, k_rows, feat)
                qx_ref[2 * pp + h, rows, :] = jnp.where(
                    own_b, q_rows, jnp.where(is_bias, 1.0, 0.0).astype(BF16))
        return 0
    lax.fori_loop(0, n_blk, fill, 0)
    ones_row = lax.broadcasted_iota(jnp.int32, (FOX_VROWS - HEAD_DIM, seq), 0) == 0
    for h in heads:
        vx_ref[h, 0:HEAD_DIM, :] = vt_ref[0, h * HEAD_DIM:(h + 1) * HEAD_DIM, :]
        vx_ref[h, HEAD_DIM:FOX_VROWS, :] = ones_row.astype(BF16)

    def rows_of(b):
        return pl.ds(pl.multiple_of(b * blk, blk), blk)

    def following(qi, kj):
        wrap = kj == qi
        return jnp.where(wrap, qi + 1, qi), jnp.where(wrap, 0, kj + 1)

    def scores(pair, slot, out):
        qi, kj = pair
        qi = jnp.minimum(qi, n_blk - 1)
        causal = mask_ref[(kj == qi).astype(jnp.int32)]
        mx = []
        for h in heads:
            t = _dot_nt(kx_ref[h, rows_of(kj), :], qx_ref[h, rows_of(qi), :]) + causal
            t_ref[slot, h] = t
            c_q = cumt_ref[0, pl.ds(head0 + h, 1), rows_of(qi)] * LOG2E
            mx.append(jnp.max(t, axis=0, keepdims=True) + c_q)
            yield
        out["mx"] = tuple(mx)

    def softmax(pair, slot, m, mx, out):
        qi, kj = pair
        sub = FOX_SUB
        q_rows = rows_of(jnp.minimum(qi, n_blk - 1))
        m_out, alpha_out = [], []
        for h in heads:
            m_old = jnp.where(kj == 0, -jnp.inf, m[h])
            c_q = cumt_ref[0, pl.ds(head0 + h, 1), q_rows] * LOG2E
            m_new = jnp.maximum(m_old, mx[h])
            shift = c_q - m_new
            for i in range(blk // sub):
                rows = slice(i * sub, (i + 1) * sub)
                p_ref[slot, h, rows, :] = jnp.exp2(t_ref[slot, h, rows, :] + shift).astype(BF16)
                if i % 2 == 1:
                    yield
            m_out.append(m_new)
            alpha_out.append(jnp.exp2(m_old - m_new))
        out["m"], out["alpha"] = tuple(m_out), tuple(alpha_out)

    def accumulate(pair, slot, alpha, acc, out):
        qi, kj = pair
        new = []
        for h in heads:
            new.append(alpha[h] * acc[h] + jnp.dot(vx_ref[h, :, rows_of(kj)], p_ref[slot, h],
                                                   preferred_element_type=F32))
            yield
        o_t = jnp.concatenate([a[0:HEAD_DIM] / a[HEAD_DIM:HEAD_DIM + 1] for a in new], axis=0)
        o_ref[0, :, rows_of(qi)] = o_t.astype(BF16)
        out["acc"] = tuple(new)

    def step(pair_s, pair_p, pair_a, slot_s, m, mx, alpha, acc):
        out = {}
        _interleave(softmax(pair_p, 1 - slot_s, m, mx, out),
                    scores(pair_s, slot_s, out),
                    accumulate(pair_a, slot_s, alpha, acc, out))
        return out

    def two_pairs(_, carry):
        pair0, pair1, m, mx1, alpha0, acc = carry
        pair2 = following(*pair1)
        a = step(pair2, pair1, pair0, 0, m, mx1, alpha0, acc)
        pair3 = following(*pair2)
        b = step(pair3, pair2, pair1, 1, a["m"], a["mx"], a["alpha"], a["acc"])
        return pair2, pair3, b["m"], b["mx"], b["alpha"], b["acc"]

    n_pairs = n_blk * (n_blk + 1) // 2
    assert n_pairs % 2 == 0, "two pairs per trip"
    zero = jnp.int32(0)
    pair0 = (zero, zero)
    pair1 = following(*pair0)
    first, second = {}, {}
    _interleave(scores(pair0, 0, first))
    _interleave(scores(pair1, 1, second))
    m = tuple(jnp.full((1, blk), -jnp.inf, F32) for _ in heads)
    _interleave(softmax(pair0, 0, m, first["mx"], first))
    acc = tuple(jnp.zeros((FOX_VROWS, blk), F32) for _ in heads)
    lax.fori_loop(0, n_pairs // 2, two_pairs,
                  (pair0, pair1, first["m"], second["mx"], first["alpha"], acc))


def _forgetting_attention(qk, v_t, cum, cum_t):
    batch, seq, _ = qk.shape
    blk = FOX_BLOCK
    key = lax.broadcasted_iota(jnp.int32, (blk, blk), 0)
    qry = lax.broadcasted_iota(jnp.int32, (blk, blk), 1)
    mask = jnp.stack([jnp.zeros((blk, blk), F32), jnp.where(key <= qry, 0.0, FOX_MASKED)])
    pair = jnp.arange(N_PAIRS)[:, None, None]
    src = jnp.arange(3 * LANES)[None, :, None]
    dst = jnp.arange(LANES)[None, None, :]
    route = jnp.zeros((N_PAIRS, 3 * LANES, LANES), jnp.bool_)
    for h in range(2):
        for n in range(3):
            route |= (src == n * LANES + 2 * pair + h) & (dst == HEAD_DIM * (1 - h) + n)
    nh = FOX_STEP_HEADS
    width = nh * HEAD_DIM
    route = route.astype(BF16).reshape(N_PAIRS * 2 // nh, nh // 2, 3 * LANES, LANES)
    route = jnp.transpose(route, (0, 2, 1, 3)).reshape(N_PAIRS * 2 // nh, 3 * LANES, width)
    whole = lambda b, g: (b, 0, 0)
    n_groups = HEADS // nh
    return pl.pallas_call(
        _fox_kernel,
        grid=(batch, n_groups),
        in_specs=[pl.BlockSpec((1, seq, width), lambda b, g: (b, 0, g)),
                  pl.BlockSpec((1, seq, width), lambda b, g: (b, 0, n_groups + g)),
                  pl.BlockSpec((1, width, seq), lambda b, g: (b, g, 0)),
                  pl.BlockSpec((1, seq, LANES), whole),
                  pl.BlockSpec((1, HEADS, seq), whole),
                  pl.BlockSpec((2, blk, blk), lambda b, g: (0, 0, 0)),
                  pl.BlockSpec((1, 3 * LANES, width), lambda b, g: (g, 0, 0))],
        out_specs=pl.BlockSpec((1, width, seq), lambda b, g: (b, g, 0)),
        out_shape=jax.ShapeDtypeStruct((batch, WIDTH, seq), BF16),
        scratch_shapes=[pltpu.VMEM((nh, seq, LANES), BF16),
                        pltpu.VMEM((nh, seq, LANES), BF16),
                        pltpu.VMEM((nh, FOX_VROWS, seq), BF16),
                        pltpu.VMEM((2, nh, blk, blk), F32),
                        pltpu.VMEM((2, nh, blk, blk), BF16)],
        compiler_params=pltpu.CompilerParams(
            dimension_semantics=("arbitrary", "arbitrary"),
            vmem_limit_bytes=VMEM_LIMIT_BYTES),
        name="fox",
    )(qk, qk, v_t, cum, cum_t, mask, route)


def _tail_kernel(x_ref, ya_ref, ybt_ref, gate_ref, mod_ref, g2_ref, gf_ref,
                 woa_ref, wob_ref, wout_ref, w1_ref, w2_ref, o_ref):
    x = x_ref[0]
    gate1 = mod_ref[0, 2:3, :]
    shift2 = mod_ref[0, 3:4, :]
    scale2 = mod_ref[0, 4:5, :]
    gate2 = mod_ref[0, 5:6, :]

    merged = (gate_ref[0, :, 0:D_MODEL].astype(F32)
              * jnp.dot(ya_ref[0], woa_ref[...], preferred_element_type=F32)
              + gate_ref[0, :, D_MODEL:N_GATE].astype(F32)
              * lax.dot_general(ybt_ref[0], wob_ref[...], (((0,), (0,)), ((), ())),
                                preferred_element_type=F32))
    x = x + gate1 * _dot(merged, wout_ref[...])

    inv = lax.rsqrt(jnp.mean(x * x, axis=-1, keepdims=True) + NORM_EPS)
    h2 = ((x * inv) * g2_ref[...] * (1.0 + scale2) + shift2).astype(BF16)
    ff = jnp.zeros_like(x)
    for j in range(D_FF // FF_CHUNK):
        cols = slice(j * FF_CHUNK, (j + 1) * FF_CHUNK)
        hid = jnp.maximum(jnp.dot(h2, w1_ref[:, cols], preferred_element_type=F32), 0.0)
        ff = ff + _dot(hid * hid, w2_ref[cols, :])
    x = x + gate2 * ff

    inv = lax.rsqrt(jnp.mean(x * x, axis=-1, keepdims=True) + NORM_EPS)
    o_ref[0] = (x * inv) * gf_ref[...]


def _tail(x, y_a, y_b, gates, mod3, norm2_g, final_g, w_oa, w_ob, w_out, w_ff1, w_ff2):
    batch, seq, _ = x.shape
    tm = TAIL_ROWS
    grid = (batch, seq // tm)
    const = lambda b, s: (0, 0)
    tile = lambda b, s: (b, s, 0)
    resident = lambda a: pl.BlockSpec(a.shape, const, pipeline_mode=pl.Buffered(1))
    return pl.pallas_call(
        _tail_kernel,
        grid=grid,
        in_specs=[pl.BlockSpec((1, tm, D_MODEL), tile),
                  pl.BlockSpec((1, tm, WIDTH), tile),
                  pl.BlockSpec((1, WIDTH, tm), lambda b, s: (b, 0, s)),
                  pl.BlockSpec((1, tm, N_GATE), tile),
                  pl.BlockSpec((1, N_MOD, D_MODEL), lambda b, s: (b, 0, 0)),
                  pl.BlockSpec((1, D_MODEL), const),
                  pl.BlockSpec((1, D_MODEL), const),
                  resident(w_oa), resident(w_ob), resident(w_out),
                  resident(w_ff1), resident(w_ff2)],
        out_specs=pl.BlockSpec((1, tm, D_MODEL), tile),
        out_shape=jax.ShapeDtypeStruct((batch, seq, D_MODEL), F32),
        compiler_params=pltpu.CompilerParams(dimension_semantics=("arbitrary", "arbitrary"),
                                             vmem_limit_bytes=VMEM_LIMIT_BYTES),
        name="tail",
    )(x, y_a, y_b, gates, mod3, norm2_g, final_g, w_oa, w_ob, w_out, w_ff1, w_ff2)


def _drop_depth_axis(t):
    return t.reshape(t.shape[1:])


def _reorder_rwkv_cols(t):
    o = 0
    r = t[..., o:o + WIDTH]; o += WIDTH
    wd = t[..., o:o + DECAY_RANK]; o += DECAY_RANK
    k = t[..., o:o + WIDTH]; o += WIDTH
    v = t[..., o:o + WIDTH]; o += WIDTH
    ad = t[..., o:o + ICLR_RANK]; o += ICLR_RANK
    gd = t[..., o:o + GATE_RANK]
    return jnp.concatenate([r, k, v, wd, ad, gd], axis=-1)


def kernel(x, c, w_ada, b_ada, norm1_g, w_in, mu_shift, w_decay_up, decay_base, w_iclr_up, iclr_base, w_gate_up, kk_scale, k_iclr_mix, r_bonus, lnx_w, lnx_b, fox_f_bias, w_o_rwkv, w_o_fox, w_out, norm2_g, w_ff1, w_ff2, final_g):
    assert w_ada.shape[0] == 1, "the tail kernel fuses the final norm: single layer only"
    layer = _drop_depth_axis
    n_rwkv = N_RKV + N_SMALL
    seg_id = jnp.arange(WIDTH // 2) // HEAD_DIM
    seg = (seg_id[:, None] == seg_id[None, :]).astype(BF16)

    mod3 = _modulation(c, layer(w_ada), layer(b_ada)).reshape(-1, N_MOD, D_MODEL)

    w = layer(w_in).astype(BF16)
    o_wd, o_k, o_v, o_ad = WIDTH, WIDTH + DECAY_RANK, 2 * WIDTH + DECAY_RANK, 3 * WIDTH + DECAY_RANK
    o_ff = n_rwkv + N_FOX
    weights = [w[:, 0:o_wd], w[:, o_k:o_v], w[:, o_v:o_ad],
               jnp.concatenate([w[:, o_wd:o_k], w[:, o_ad:n_rwkv]], axis=1),
               w[:, n_rwkv:o_ff],
               jnp.pad(w[:, o_ff:o_ff + HEADS], ((0, 0), (0, LANES - HEADS))),
               w[:, o_ff + HEADS:]]
    mu = _reorder_rwkv_cols(layer(mu_shift)).reshape(1, n_rwkv)
    f_bias = jnp.pad(layer(fox_f_bias), (0, LANES - HEADS)).reshape(1, LANES)
    rkv, small, qk, v_t, cum, cum_t, gates = _projection(
        x, mod3, layer(norm1_g).reshape(1, D_MODEL), weights, mu, f_bias)

    zeros = jnp.zeros((DECAY_RANK, WIDTH), F32)
    w_lr = jnp.concatenate(
        [jnp.concatenate([layer(w_decay_up), zeros], axis=1),
         jnp.concatenate([zeros, layer(w_iclr_up)], axis=1)], axis=0).astype(BF16)
    vecs = jnp.stack([layer(decay_base), layer(iclr_base), layer(kk_scale), layer(k_iclr_mix),
                      layer(r_bonus).reshape(WIDTH), layer(lnx_w), layer(lnx_b),
                      jnp.zeros((WIDTH,), F32)], axis=0)
    y_a = _rwkv_mix(rkv, small, w_lr, layer(w_gate_up).astype(BF16), vecs, seg)

    y_b = _forgetting_attention(qk, v_t, cum, cum_t)

    return _tail(x, y_a, y_b, gates, mod3, layer(norm2_g).reshape(1, D_MODEL),
                 final_g.reshape(1, D_MODEL),
                 layer(w_o_rwkv).astype(BF16), layer(w_o_fox).astype(BF16),
                 layer(w_out).astype(BF16), layer(w_ff1).astype(BF16), layer(w_ff2).astype(BF16))
```

```python
import jax
import jax.numpy as jnp
from jax import lax
from jax.experimental import pallas as pl
from jax.experimental.pallas import tpu as pltpu

F32 = jnp.float32
BF16 = jnp.bfloat16

D_MODEL = 1024
HEAD_DIM = 64
HEADS = 8
WIDTH = HEADS * HEAD_DIM
DECAY_RANK = 64
ICLR_RANK = 64
GATE_RANK = 128
D_FF = 4 * D_MODEL
N_MOD = 6
NORM_EPS = 1e-6
GN_EPS = 64e-5

LANES = 128
PAIR = 2 * HEAD_DIM
N_PAIRS = HEADS // 2

LOG2E = 1.4426950408889634

N_SMALL = DECAY_RANK + ICLR_RANK + GATE_RANK
N_RKV = 3 * WIDTH
N_FOX = 3 * WIDTH
N_GATE = 2 * D_MODEL

PROJ_ROWS = 256
CHUNK = 128
RWKV_STEP_CHUNKS = 4
RWKV_MID_SLOTS = 4
FOX_BLOCK = 512
FOX_SUB = 32
FOX_VROWS = HEAD_DIM + 16
FOX_STEP_HEADS = 4
FOX_MASKED = -1e30
TAIL_ROWS = 512
FF_CHUNK = 1024

VMEM_LIMIT_BYTES = 56 * 1024 * 1024


def _dot(a, b):
    return jnp.dot(a.astype(BF16), b.astype(BF16), preferred_element_type=F32)


def _dot_nt(a, b):
    return lax.dot_general(a.astype(BF16), b.astype(BF16), (((1,), (1,)), ((), ())),
                           preferred_element_type=F32)


def _softplus(z):
    return jnp.maximum(z, 0.0) + jnp.log(1.0 + jnp.exp(-jnp.abs(z)))


def _sigmoid(z):
    return 1.0 / (1.0 + jnp.exp(-z))


def _bf16_parts(x, n):
    parts = []
    for _ in range(n):
        p = x.astype(BF16)
        parts.append(p)
        x = x - p.astype(F32)
    return parts


def _cumsum_rows(tri, x, n_parts):
    return sum(jnp.dot(tri, p, preferred_element_type=F32) for p in _bf16_parts(x, n_parts))


def _head_sums(x, seg):
    half = seg.shape[0]
    return jnp.concatenate([_dot(x[:, 0:half], seg), _dot(x[:, half:2 * half], seg)], axis=1)


def _lower_tri(n, dtype):
    row = lax.broadcasted_iota(jnp.int32, (n, n), 0)
    col = lax.broadcasted_iota(jnp.int32, (n, n), 1)
    return (col <= row).astype(dtype)


def _mod_kernel(c_ref, w_ref, b_ref, o_ref):
    c = c_ref[...]
    c_act = c * _sigmoid(c)
    o_ref[...] = _dot(c_act, w_ref[...]) + b_ref[...]


def _modulation(c, w_ada, b_ada):
    batch = c.shape[0]
    n = w_ada.shape[1]
    tn = 1536
    return pl.pallas_call(
        _mod_kernel,
        grid=(n // tn,),
        in_specs=[pl.BlockSpec((batch, D_MODEL), lambda j: (0, 0)),
                  pl.BlockSpec((D_MODEL, tn), lambda j: (0, j)),
                  pl.BlockSpec((1, tn), lambda j: (0, j))],
        out_specs=pl.BlockSpec((batch, tn), lambda j: (0, j)),
        out_shape=jax.ShapeDtypeStruct((batch, n), F32),
        compiler_params=pltpu.CompilerParams(dimension_semantics=("arbitrary",),
                                             vmem_limit_bytes=VMEM_LIMIT_BYTES),
        name="mod",
    )(c, w_ada, b_ada.reshape(1, n))


def _proj_normalise(x, mod_ref, g_ref):
    shift1 = mod_ref[0, 0:1, :]
    scale1 = mod_ref[0, 1:2, :]
    inv = lax.rsqrt(jnp.mean(x * x, axis=-1, keepdims=True) + NORM_EPS)
    return ((x * inv) * (g_ref[...] * (1.0 + scale1)) + shift1).astype(BF16)


def _proj_tile(h, rows, weights, mu_ref, fb_ref, route_ref, outs, shift_ref, carry_ref):
    wr_ref, wk_ref, wv_ref, ws_ref, wfox_ref, wff_ref, wgate_ref = weights
    rkv_ref, small_ref, qx_ref, kx_ref, vt_ref, cumt_ref, gate_ref = outs
    n = h.shape[0]

    n_tok = N_RKV + N_SMALL
    p = jnp.concatenate([jnp.dot(h, w[...], preferred_element_type=F32)
                         for w in (wr_ref, wk_ref, wv_ref, ws_ref)], axis=1)
    shift_ref[8:8 + n, :] = p
    prev = shift_ref[7:7 + n, :]
    shift_ref[7:8, :] = p[n - 1:n, :]
    mixed = p + mu_ref[...] * (prev - p)
    rkv_ref[0, rows, :] = mixed[:, 0:N_RKV].astype(BF16)
    small_ref[0, rows, :] = mixed[:, N_RKV:n_tok]

    pf = jnp.dot(h, wfox_ref[...], preferred_element_type=F32)
    vt_ref[0, :, rows] = jnp.transpose(pf[:, 2 * WIDTH:N_FOX]).astype(BF16)

    ff = jnp.dot(h, wff_ref[...], preferred_element_type=F32)
    logf = -_softplus(-(ff + fb_ref[...]))
    cum = _cumsum_rows(_lower_tri(n, BF16), logf, 3) + carry_ref[...]
    cumt_ref[0, :, rows] = jnp.transpose(cum)[0:HEADS, :]
    carry_ref[...] = cum[n - 1:n, :]

    parts = jnp.concatenate(_bf16_parts(cum * (-LOG2E), 3), axis=1)
    feats = jnp.dot(parts, route_ref[...], preferred_element_type=F32).astype(BF16)
    lane_b = lax.broadcasted_iota(jnp.int32, (n, LANES), 1)
    for pair in range(N_PAIRS):
        group = slice(pair * PAIR, (pair + 1) * PAIR)
        q_pair = (pf[:, group] * (LOG2E * HEAD_DIM ** -0.5)).astype(BF16)
        k_pair = pf[:, WIDTH + pair * PAIR:WIDTH + (pair + 1) * PAIR].astype(BF16)
        for hh in range(2):
            own = (lane_b // HEAD_DIM) == hh
            bias0 = HEAD_DIM * (1 - hh)
            is_bias = (lane_b >= bias0) & (lane_b < bias0 + 3)
            kx_ref[0, 2 * pair + hh, rows, :] = jnp.where(own, k_pair, feats[:, group])
            qx_ref[0, 2 * pair + hh, rows, :] = jnp.where(
                own, q_pair, jnp.where(is_bias, 1.0, 0.0).astype(BF16))

    pg = jnp.dot(h, wgate_ref[...], preferred_element_type=F32)
    gate_ref[0, rows, :] = _sigmoid(pg).astype(BF16)


def _proj_kernel(x_ref, x_next_ref, mod_ref, g_ref, wr_ref, wk_ref, wv_ref, ws_ref, wfox_ref,
                 wff_ref, wgate_ref, mu_ref, fb_ref, route_ref,
                 rkv_ref, small_ref, qx_ref, kx_ref, vt_ref, cumt_ref, gate_ref,
                 shift_ref, carry_ref, h_ref):
    tm = PROJ_ROWS
    weights = (wr_ref, wk_ref, wv_ref, ws_ref, wfox_ref, wff_ref, wgate_ref)
    outs = (rkv_ref, small_ref, qx_ref, kx_ref, vt_ref, cumt_ref, gate_ref)

    @pl.when(pl.program_id(1) == 0)
    def _():
        shift_ref[0:8, :] = jnp.zeros((8, N_RKV + N_SMALL), F32)
        carry_ref[...] = jnp.zeros_like(carry_ref)
        h_ref[0] = _proj_normalise(x_ref[0, 0:tm, :], mod_ref, g_ref)

    h_ref[1] = _proj_normalise(x_ref[0, tm:2 * tm, :], mod_ref, g_ref)
    _proj_tile(h_ref[0], slice(0, tm), weights, mu_ref, fb_ref, route_ref, outs,
               shift_ref, carry_ref)
    h_ref[0] = _proj_normalise(x_next_ref[0], mod_ref, g_ref)
    _proj_tile(h_ref[1], slice(tm, 2 * tm), weights, mu_ref, fb_ref, route_ref, outs,
               shift_ref, carry_ref)


def _projection(x, mod3, norm_g, weights, mu, f_bias):
    src = jnp.arange(3 * LANES)[:, None]
    dst = jnp.arange(WIDTH)[None, :]
    route = jnp.zeros((3 * LANES, WIDTH), jnp.bool_)
    for p in range(N_PAIRS):
        for h in range(2):
            for n in range(3):
                route |= ((src == n * LANES + 2 * p + h)
                          & (dst == p * PAIR + HEAD_DIM * (1 - h) + n))
    route = route.astype(BF16)
    batch, seq, _ = x.shape
    tm = PROJ_ROWS
    n_tiles = seq // tm
    grid = (batch, n_tiles // 2)
    const = lambda b, s: (0, 0)
    tile = lambda b, s: (b, s, 0)
    lanes = lambda b, s: (b, 0, s)
    ahead = lambda b, s: (b, jnp.minimum(2 * s + 2, n_tiles - 1), 0)
    return pl.pallas_call(
        _proj_kernel,
        grid=grid,
        in_specs=[pl.BlockSpec((1, 2 * tm, D_MODEL), tile),
                  pl.BlockSpec((1, tm, D_MODEL), ahead),
                  pl.BlockSpec((1, N_MOD, D_MODEL), lambda b, s: (b, 0, 0)),
                  pl.BlockSpec((1, D_MODEL), const),
                  *[pl.BlockSpec(w.shape, const, pipeline_mode=pl.Buffered(1)) for w in weights],
                  pl.BlockSpec((1, N_RKV + N_SMALL), const),
                  pl.BlockSpec((1, LANES), const),
                  pl.BlockSpec(route.shape, const)],
        out_specs=[pl.BlockSpec((1, 2 * tm, N_RKV), tile),
                   pl.BlockSpec((1, 2 * tm, N_SMALL), tile),
                   pl.BlockSpec((1, HEADS, 2 * tm, LANES), lambda b, s: (b, 0, s, 0)),
                   pl.BlockSpec((1, HEADS, 2 * tm, LANES), lambda b, s: (b, 0, s, 0)),
                   pl.BlockSpec((1, WIDTH, 2 * tm), lanes),
                   pl.BlockSpec((1, HEADS, 2 * tm), lanes),
                   pl.BlockSpec((1, 2 * tm, N_GATE), tile)],
        out_shape=[jax.ShapeDtypeStruct((batch, seq, N_RKV), BF16),
                   jax.ShapeDtypeStruct((batch, seq, N_SMALL), F32),
                   jax.ShapeDtypeStruct((batch, HEADS, seq, LANES), BF16),
                   jax.ShapeDtypeStruct((batch, HEADS, seq, LANES), BF16),
                   jax.ShapeDtypeStruct((batch, WIDTH, seq), BF16),
                   jax.ShapeDtypeStruct((batch, HEADS, seq), F32),
                   jax.ShapeDtypeStruct((batch, seq, N_GATE), BF16)],
        scratch_shapes=[pltpu.VMEM((tm + 8, N_RKV + N_SMALL), F32),
                        pltpu.VMEM((1, LANES), F32),
                        pltpu.VMEM((2, tm, D_MODEL), BF16)],
        compiler_params=pltpu.CompilerParams(dimension_semantics=("arbitrary", "arbitrary"),
                                             vmem_limit_bytes=VMEM_LIMIT_BYTES),
        name="proj",
    )(x, x, mod3, norm_g, *weights, mu, f_bias, route)


def _rwkv_prepare(rkv, small, w_lr_ref, w_gate_ref, vec_ref, seg_ref, slot, prep):
    am_ref, rm_ref, vb_ref, bt_ref, kt_ref, bh_ref, kh_ref, dend_ref, g_ref, bv_ref = prep
    c = CHUNK
    r = rkv[:, 0:WIDTH].astype(F32)
    k_raw = rkv[:, WIDTH:2 * WIDTH].astype(F32)
    v = rkv[:, 2 * WIDTH:3 * WIDTH].astype(F32)
    decay_base, iclr_base, kk_scale, k_mix, r_bonus = (vec_ref[i:i + 1, :] for i in range(5))

    lane = lax.broadcasted_iota(jnp.int32, (1, LANES), 1)
    lr_in = jnp.where(lane < DECAY_RANK, jnp.tanh(small[:, 0:LANES]), small[:, 0:LANES])
    lr = _dot(lr_in, w_lr_ref[...])
    w_log = -_softplus(-(decay_base + lr[:, 0:WIDTH])) - 0.5
    log_decay = -jnp.exp(w_log)
    a = _sigmoid(iclr_base + lr[:, WIDTH:2 * WIDTH])
    g_ref[slot] = _dot(_sigmoid(small[:, LANES:2 * LANES]), w_gate_ref[...])
    yield

    seg = seg_ref[...]
    kk = k_raw * kk_scale
    kk = kk * lax.rsqrt(jnp.maximum(_head_sums(kk * kk, seg), 1e-24))
    k = k_raw * (1.0 + (a - 1.0) * k_mix)
    b_vec = kk * a
    yield

    cs = _cumsum_rows(_lower_tri(c, BF16), log_decay, 2)
    cs_end = cs[c - 1:c, :]
    am_ref[slot] = (-kk * jnp.exp(cs - log_decay)).astype(BF16)
    rm_ref[slot] = (r * jnp.exp(cs)).astype(BF16)
    vb_ref[slot] = v.astype(BF16)
    yield
    w_inv = jnp.exp(-cs)
    bt_ref[slot] = jnp.transpose(b_vec * w_inv).astype(BF16)
    yield
    kt_ref[slot] = jnp.transpose(k * w_inv).astype(BF16)
    yield
    w_end = jnp.exp(cs_end - cs)
    bh_ref[slot] = jnp.transpose(b_vec * w_end).astype(BF16)
    yield
    kh_ref[slot] = jnp.transpose(k * w_end).astype(BF16)
    dend_ref[slot] = jnp.broadcast_to(jnp.exp(cs_end), (8, WIDTH))
    yield
    bv_ref[slot] = _head_sums(r * k * r_bonus, seg) * v


def _rwkv_chunk_matrices(slot, mid_slot, prep, mid):
    am_ref, rm_ref, vb_ref, bt_ref, kt_ref = prep[:5]
    pq_ref, lhs_ref = mid
    c = CHUNK
    lane = lax.broadcasted_iota(jnp.int32, (1, LANES), 1)
    row = lax.broadcasted_iota(jnp.int32, (c, c), 0)
    col = lax.broadcasted_iota(jnp.int32, (c, c), 1)
    strict = col < row
    incl = col <= row
    eye = (col == row).astype(F32)

    sls = [slice(p * PAIR, (p + 1) * PAIR) for p in range(N_PAIRS)]
    heads = [(p, h) for p in range(N_PAIRS) for h in range(2)]
    a_m, a_ab, a_ak = {}, {}, {}
    for i, (p, h) in enumerate(heads):
        head = (lane // HEAD_DIM) == h
        a_m[p, h] = jnp.where(head, am_ref[slot, :, sls[p]].astype(F32), 0.0).astype(BF16)
        r_m = jnp.where(head, rm_ref[slot, :, sls[p]].astype(F32), 0.0).astype(BF16)
        rhs = jnp.concatenate([bt_ref[slot, sls[p], :], kt_ref[slot, sls[p], :]], axis=1)
        big = jnp.dot(jnp.concatenate([a_m[p, h], r_m], axis=0), rhs,
                      preferred_element_type=F32)
        a_ab[p, h] = jnp.where(strict, big[0:c, 0:c], 0.0)
        a_ak[p, h] = jnp.where(strict, big[0:c, c:2 * c], 0.0).astype(BF16)
        lhs_ref[mid_slot, i] = jnp.concatenate(
            [r_m, jnp.where(incl, big[c:2 * c, 0:c], 0.0).astype(BF16),
             jnp.where(incl, big[c:2 * c, c:2 * c], 0.0).astype(BF16)], axis=1)
    yield
    t_inv = {hd: eye + a_ab[hd] for hd in heads}
    m_pow = {hd: _dot(a_ab[hd], a_ab[hd]) for hd in heads}
    yield
    for k in range(1, 7):
        skip = (2 ** k) // 16 * 16
        last = k == 6
        for hd in heads:
            rhs = t_inv[hd] if last else jnp.concatenate([m_pow[hd], t_inv[hd]], axis=1)
            upd = _dot(m_pow[hd][skip:c, :], rhs)
            if skip:
                upd = jnp.concatenate([jnp.zeros((skip, upd.shape[1]), F32), upd], axis=0)
            if last:
                t_inv[hd] = t_inv[hd] + upd
            else:
                m_pow[hd] = upd[:, 0:c]
                t_inv[hd] = t_inv[hd] + upd[:, c:2 * c]
        yield
    ak_v = {(p, h): _dot(a_ak[p, h], vb_ref[slot, :, sls[p]]) for (p, h) in heads}
    yield
    for i, hd in enumerate(heads):
        pq_ref[mid_slot, i] = _dot(t_inv[hd],
                               jnp.concatenate([a_m[hd].astype(F32), ak_v[hd]], axis=1))


def _rwkv_chunk_state(slot, mid_slot, prep, mid, vec_ref, seg_ref, z_ref, out_ref, rows):
    vb_ref, bh_ref, kh_ref, dend_ref, g_ref, bv_ref = (prep[i] for i in (2, 5, 6, 7, 8, 9))
    pq_ref, lhs_ref = mid
    c = CHUNK
    ln_w, ln_b = vec_ref[5:6, :], vec_ref[6:7, :]
    lane = lax.broadcasted_iota(jnp.int32, (1, LANES), 1)
    row = lax.broadcasted_iota(jnp.int32, (c, c), 0)
    col = lax.broadcasted_iota(jnp.int32, (c, c), 1)
    eye = (col == row).astype(F32)
    same_head = (row // HEAD_DIM) == (col // HEAD_DIM)
    pairs = range(N_PAIRS)
    sls = [slice(p * PAIR, (p + 1) * PAIR) for p in pairs]
    heads = [(p, h) for p in pairs for h in range(2)]
    v_b = [vb_ref[slot, :, sl] for sl in sls]
    z_f = [z_ref[p] for p in pairs]
    z_b = [z.astype(BF16) for z in z_f]

    u = {(p, h): (_dot(pq_ref[mid_slot, i, :, 0:PAIR], z_b[p])
                  + pq_ref[mid_slot, i, :, PAIR:2 * PAIR])
         for i, (p, h) in enumerate(heads)}
    yield
    y_hd = {(p, h): jnp.dot(
        lhs_ref[mid_slot, i], jnp.concatenate([z_b[p], u[p, h].astype(BF16), v_b[p]], axis=0),
        preferred_element_type=F32) for i, (p, h) in enumerate(heads)}
    first = lane < HEAD_DIM
    y_pairs = []
    for p in pairs:
        u_p = jnp.where(first, u[p, 0], u[p, 1])
        y_pairs.append(jnp.where(first, y_hd[p, 0], y_hd[p, 1]))
        d_col = jnp.sum(eye * dend_ref[slot, 0:1, sls[p]], axis=1, keepdims=True)
        z_new = d_col * z_f[p] + jnp.dot(
            jnp.concatenate([bh_ref[slot, sls[p], :], kh_ref[slot, sls[p], :]], axis=1),
            jnp.concatenate([u_p.astype(BF16), v_b[p]], axis=0), preferred_element_type=F32)
        z_ref[p] = jnp.where(same_head, z_new, 0.0)
    yield

    seg = seg_ref[...]
    y = jnp.concatenate(y_pairs, axis=1)
    mean = _head_sums(y, seg) * (1.0 / HEAD_DIM)
    yc = y - mean
    var = _head_sums(yc * yc, seg) * (1.0 / HEAD_DIM)
    y = yc * lax.rsqrt(var + GN_EPS) * ln_w + ln_b
    out_ref[0, rows, :] = ((y + bv_ref[slot]) * g_ref[slot]).astype(BF16)


def _interleave(*stages, late=()):
    live = list(stages)
    late = list(late)
    while live:
        for gen in list(live):
            try:
                next(gen)
            except StopIteration:
                live.remove(gen)
                if late and gen is stages[0]:
                    live.extend(late)
                    late = []


def _in_turn(*stages):
    for gen in stages:
        yield from gen


def _rwkv_kernel(rkv_ref, small_ref, rkv_next_ref, small_next_ref,
                 w_lr_ref, w_gate_ref, vec_ref, seg_ref, out_ref, z_ref, pq_ref, lhs_ref, *prep):
    c = CHUNK
    params = (w_lr_ref, w_gate_ref, vec_ref, seg_ref)
    mid = (pq_ref, lhs_ref)
    rows = [slice(i * c, (i + 1) * c) for i in range(RWKV_STEP_CHUNKS)]

    def prepare(r_ref, s_ref, i, slot):
        return _rwkv_prepare(r_ref[0, rows[i], :], s_ref[0, rows[i], :], *params, slot, prep)

    def matrices(i):
        return _rwkv_chunk_matrices(i, i % RWKV_MID_SLOTS, prep, mid)

    def state(i):
        return _rwkv_chunk_state(i, i % RWKV_MID_SLOTS, prep, mid, vec_ref, seg_ref, z_ref,
                                 out_ref, rows[i])

    @pl.when(pl.program_id(1) == 0)
    def _():
        z_ref[...] = jnp.zeros_like(z_ref)
        _interleave(prepare(rkv_ref, small_ref, 0, 0), prepare(rkv_ref, small_ref, 1, 1))

    n_pairs = RWKV_STEP_CHUNKS // 2
    for j in range(n_pairs):
        a, b = 2 * j, 2 * j + 1
        streams = [matrices(a), matrices(b)]
        if j > 0:
            streams.insert(0, _in_turn(state(a - 2), state(b - 2)))
        if j + 1 < n_pairs:
            ahead = [prepare(rkv_ref, small_ref, a + 2, a + 2),
                     prepare(rkv_ref, small_ref, b + 2, b + 2)]
        else:
            ahead = [prepare(rkv_next_ref, small_next_ref, 0, 0),
                     prepare(rkv_next_ref, small_next_ref, 1, 1)]
        if ahead and j == 1 and j + 1 == n_pairs:
            _interleave(*streams, late=ahead)
        else:
            _interleave(*streams, *ahead)
    _interleave(_in_turn(state(RWKV_STEP_CHUNKS - 2), state(RWKV_STEP_CHUNKS - 1)))


def _rwkv_mix(rkv, small, w_lr, w_gate, vecs, seg):
    batch, seq, _ = rkv.shape
    c = CHUNK
    n = RWKV_STEP_CHUNKS
    n_pairs_total = seq // (2 * c)
    grid = (batch, seq // (n * c))
    const = lambda b, s: (0, 0)
    tile = lambda b, s: (b, s, 0)
    ahead = lambda b, s: (b, jnp.minimum(n // 2 * (s + 1), n_pairs_total - 1), 0)
    slots = lambda shape, dtype: pltpu.VMEM((n,) + shape, dtype)
    mid_slots = lambda shape, dtype: pltpu.VMEM((RWKV_MID_SLOTS,) + shape, dtype)
    return pl.pallas_call(
        _rwkv_kernel,
        grid=grid,
        in_specs=[pl.BlockSpec((1, n * c, N_RKV), tile),
                  pl.BlockSpec((1, n * c, N_SMALL), tile),
                  pl.BlockSpec((1, 2 * c, N_RKV), ahead),
                  pl.BlockSpec((1, 2 * c, N_SMALL), ahead),
                  pl.BlockSpec(w_lr.shape, const),
                  pl.BlockSpec(w_gate.shape, const),
                  pl.BlockSpec(vecs.shape, const),
                  pl.BlockSpec(seg.shape, const)],
        out_specs=pl.BlockSpec((1, n * c, WIDTH), tile),
        out_shape=jax.ShapeDtypeStruct((batch, seq, WIDTH), BF16),
        scratch_shapes=[pltpu.VMEM((N_PAIRS, PAIR, PAIR), F32),
                        mid_slots((HEADS, c, 2 * PAIR), F32),
                        mid_slots((HEADS, c, 3 * PAIR), BF16),
                        slots((c, WIDTH), BF16),
                        slots((c, WIDTH), BF16),
                        slots((c, WIDTH), BF16),
                        slots((WIDTH, c), BF16),
                        slots((WIDTH, c), BF16),
                        slots((WIDTH, c), BF16),
                        slots((WIDTH, c), BF16),
                        slots((8, WIDTH), F32),
                        slots((c, WIDTH), F32),
                        slots((c, WIDTH), F32)],
        compiler_params=pltpu.CompilerParams(dimension_semantics=("arbitrary", "arbitrary"),
                                             vmem_limit_bytes=VMEM_LIMIT_BYTES),
        name="rwkv",
    )(rkv, small, rkv, small, w_lr, w_gate, vecs, seg)


def _fox_kernel(qx_ref, kx_ref, vt_ref, cumt_ref, mask_ref, o_ref, vx_ref, t_ref, p_ref):
    heads = range(FOX_STEP_HEADS)
    head0 = FOX_STEP_HEADS * pl.program_id(1)
    blk = FOX_BLOCK
    seq = kx_ref.shape[2]
    n_blk = seq // blk
    ones_row = lax.broadcasted_iota(jnp.int32, (FOX_VROWS - HEAD_DIM, seq), 0) == 0
    for h in heads:
        vx_ref[h, 0:HEAD_DIM, :] = vt_ref[0, h * HEAD_DIM:(h + 1) * HEAD_DIM, :]
        vx_ref[h, HEAD_DIM:FOX_VROWS, :] = ones_row.astype(BF16)

    def rows_of(b):
        return pl.ds(pl.multiple_of(b * blk, blk), blk)

    def following(qi, kj):
        wrap = kj == qi
        return jnp.where(wrap, qi + 1, qi), jnp.where(wrap, 0, kj + 1)

    def scores(pair, slot, out):
        qi, kj = pair
        qi = jnp.minimum(qi, n_blk - 1)
        causal = mask_ref[(kj == qi).astype(jnp.int32)]
        mx = []
        for h in heads:
            t = _dot_nt(kx_ref[0, h, rows_of(kj), :], qx_ref[0, h, rows_of(qi), :]) + causal
            t_ref[slot, h] = t
            c_q = cumt_ref[0, pl.ds(head0 + h, 1), rows_of(qi)] * LOG2E
            mx.append(jnp.max(t, axis=0, keepdims=True) + c_q)
            yield
        out["mx"] = tuple(mx)

    def softmax(pair, slot, m, mx, out):
        qi, kj = pair
        sub = FOX_SUB
        q_rows = rows_of(jnp.minimum(qi, n_blk - 1))
        m_out, alpha_out = [], []
        for h in heads:
            m_old = jnp.where(kj == 0, -jnp.inf, m[h])
            c_q = cumt_ref[0, pl.ds(head0 + h, 1), q_rows] * LOG2E
            m_new = jnp.maximum(m_old, mx[h])
            shift = c_q - m_new
            for i in range(blk // sub):
                rows = slice(i * sub, (i + 1) * sub)
                p_ref[slot, h, rows, :] = jnp.exp2(t_ref[slot, h, rows, :] + shift).astype(BF16)
                if i % 2 == 1:
                    yield
            m_out.append(m_new)
            alpha_out.append(jnp.exp2(m_old - m_new))
        out["m"], out["alpha"] = tuple(m_out), tuple(alpha_out)

    def accumulate(pair, slot, alpha, acc, out):
        qi, kj = pair
        new = []
        for h in heads:
            new.append(alpha[h] * acc[h] + jnp.dot(vx_ref[h, :, rows_of(kj)], p_ref[slot, h],
                                                   preferred_element_type=F32))
            yield
        o_t = jnp.concatenate([a[0:HEAD_DIM] / a[HEAD_DIM:HEAD_DIM + 1] for a in new], axis=0)
        o_ref[0, :, rows_of(qi)] = o_t.astype(BF16)
        out["acc"] = tuple(new)

    def step(pair_s, pair_p, pair_a, slot_s, m, mx, alpha, acc):
        out = {}
        _interleave(softmax(pair_p, 1 - slot_s, m, mx, out),
                    scores(pair_s, slot_s, out),
                    accumulate(pair_a, slot_s, alpha, acc, out))
        return out

    def two_pairs(_, carry):
        pair0, pair1, m, mx1, alpha0, acc = carry
        pair2 = following(*pair1)
        a = step(pair2, pair1, pair0, 0, m, mx1, alpha0, acc)
        pair3 = following(*pair2)
        b = step(pair3, pair2, pair1, 1, a["m"], a["mx"], a["alpha"], a["acc"])
        return pair2, pair3, b["m"], b["mx"], b["alpha"], b["acc"]

    n_pairs = n_blk * (n_blk + 1) // 2
    assert n_pairs % 2 == 0, "two pairs per trip"
    zero = jnp.int32(0)
    pair0 = (zero, zero)
    pair1 = following(*pair0)
    first, second = {}, {}
    _interleave(scores(pair0, 0, first))
    _interleave(scores(pair1, 1, second))
    m = tuple(jnp.full((1, blk), -jnp.inf, F32) for _ in heads)
    _interleave(softmax(pair0, 0, m, first["mx"], first))
    acc = tuple(jnp.zeros((FOX_VROWS, blk), F32) for _ in heads)
    lax.fori_loop(0, n_pairs // 2, two_pairs,
                  (pair0, pair1, first["m"], second["mx"], first["alpha"], acc))


def _forgetting_attention(qx, kx, v_t, cum_t):
    batch, _, seq, _ = qx.shape
    blk = FOX_BLOCK
    key = lax.broadcasted_iota(jnp.int32, (blk, blk), 0)
    qry = lax.broadcasted_iota(jnp.int32, (blk, blk), 1)
    mask = jnp.stack([jnp.zeros((blk, blk), F32), jnp.where(key <= qry, 0.0, FOX_MASKED)])
    nh = FOX_STEP_HEADS
    width = nh * HEAD_DIM
    whole = lambda b, g: (b, 0, 0)
    n_groups = HEADS // nh
    return pl.pallas_call(
        _fox_kernel,
        grid=(batch, n_groups),
        in_specs=[pl.BlockSpec((1, nh, seq, LANES), lambda b, g: (b, g, 0, 0)),
                  pl.BlockSpec((1, nh, seq, LANES), lambda b, g: (b, g, 0, 0)),
                  pl.BlockSpec((1, width, seq), lambda b, g: (b, g, 0)),
                  pl.BlockSpec((1, HEADS, seq), whole),
                  pl.BlockSpec((2, blk, blk), lambda b, g: (0, 0, 0))],
        out_specs=pl.BlockSpec((1, width, seq), lambda b, g: (b, g, 0)),
        out_shape=jax.ShapeDtypeStruct((batch, WIDTH, seq), BF16),
        scratch_shapes=[pltpu.VMEM((nh, FOX_VROWS, seq), BF16),
                        pltpu.VMEM((2, nh, blk, blk), F32),
                        pltpu.VMEM((2, nh, blk, blk), BF16)],
        compiler_params=pltpu.CompilerParams(
            dimension_semantics=("arbitrary", "arbitrary"),
            vmem_limit_bytes=VMEM_LIMIT_BYTES),
        name="fox",
    )(qx, kx, v_t, cum_t, mask)


def _tail_kernel(x_ref, ya_ref, ybt_ref, gate_ref, mod_ref, g2_ref, gf_ref,
                 woa_ref, wob_ref, wout_ref, w1_ref, w2_ref, o_ref):
    x = x_ref[0]
    gate1 = mod_ref[0, 2:3, :]
    shift2 = mod_ref[0, 3:4, :]
    scale2 = mod_ref[0, 4:5, :]
    gate2 = mod_ref[0, 5:6, :]

    merged = (gate_ref[0, :, 0:D_MODEL].astype(F32)
              * jnp.dot(ya_ref[0], woa_ref[...], preferred_element_type=F32)
              + gate_ref[0, :, D_MODEL:N_GATE].astype(F32)
              * lax.dot_general(ybt_ref[0], wob_ref[...], (((0,), (0,)), ((), ())),
                                preferred_element_type=F32))
    x = x + gate1 * _dot(merged, wout_ref[...])

    inv = lax.rsqrt(jnp.mean(x * x, axis=-1, keepdims=True) + NORM_EPS)
    h2 = ((x * inv) * g2_ref[...] * (1.0 + scale2) + shift2).astype(BF16)
    ff = jnp.zeros_like(x)
    for j in range(D_FF // FF_CHUNK):
        cols = slice(j * FF_CHUNK, (j + 1) * FF_CHUNK)
        hid = jnp.maximum(jnp.dot(h2, w1_ref[:, cols], preferred_element_type=F32), 0.0)
        ff = ff + _dot(hid * hid, w2_ref[cols, :])
    x = x + gate2 * ff

    inv = lax.rsqrt(jnp.mean(x * x, axis=-1, keepdims=True) + NORM_EPS)
    o_ref[0] = (x * inv) * gf_ref[...]


def _tail(x, y_a, y_b, gates, mod3, norm2_g, final_g, w_oa, w_ob, w_out, w_ff1, w_ff2):
    batch, seq, _ = x.shape
    tm = TAIL_ROWS
    grid = (batch, seq // tm)
    const = lambda b, s: (0, 0)
    tile = lambda b, s: (b, s, 0)
    resident = lambda a: pl.BlockSpec(a.shape, const, pipeline_mode=pl.Buffered(1))
    return pl.pallas_call(
        _tail_kernel,
        grid=grid,
        in_specs=[pl.BlockSpec((1, tm, D_MODEL), tile),
                  pl.BlockSpec((1, tm, WIDTH), tile),
                  pl.BlockSpec((1, WIDTH, tm), lambda b, s: (b, 0, s)),
                  pl.BlockSpec((1, tm, N_GATE), tile),
                  pl.BlockSpec((1, N_MOD, D_MODEL), lambda b, s: (b, 0, 0)),
                  pl.BlockSpec((1, D_MODEL), const),
                  pl.BlockSpec((1, D_MODEL), const),
                  resident(w_oa), resident(w_ob), resident(w_out),
                  resident(w_ff1), resident(w_ff2)],
        out_specs=pl.BlockSpec((1, tm, D_MODEL), tile),
        out_shape=jax.ShapeDtypeStruct((batch, seq, D_MODEL), F32),
        compiler_params=pltpu.CompilerParams(dimension_semantics=("arbitrary", "arbitrary"),
                                             vmem_limit_bytes=VMEM_LIMIT_BYTES),
        name="tail",
    )(x, y_a, y_b, gates, mod3, norm2_g, final_g, w_oa, w_ob, w_out, w_ff1, w_ff2)


def _drop_depth_axis(t):
    return t.reshape(t.shape[1:])


def _reorder_rwkv_cols(t):
    o = 0
    r = t[..., o:o + WIDTH]; o += WIDTH
    wd = t[..., o:o + DECAY_RANK]; o += DECAY_RANK
    k = t[..., o:o + WIDTH]; o += WIDTH
    v = t[..., o:o + WIDTH]; o += WIDTH
    ad = t[..., o:o + ICLR_RANK]; o += ICLR_RANK
    gd = t[..., o:o + GATE_RANK]
    return jnp.concatenate([r, k, v, wd, ad, gd], axis=-1)


def kernel(x, c, w_ada, b_ada, norm1_g, w_in, mu_shift, w_decay_up, decay_base, w_iclr_up, iclr_base, w_gate_up, kk_scale, k_iclr_mix, r_bonus, lnx_w, lnx_b, fox_f_bias, w_o_rwkv, w_o_fox, w_out, norm2_g, w_ff1, w_ff2, final_g):
    assert w_ada.shape[0] == 1, "the tail kernel fuses the final norm: single layer only"
    layer = _drop_depth_axis
    n_rwkv = N_RKV + N_SMALL
    seg_id = jnp.arange(WIDTH // 2) // HEAD_DIM
    seg = (seg_id[:, None] == seg_id[None, :]).astype(BF16)

    mod3 = _modulation(c, layer(w_ada), layer(b_ada)).reshape(-1, N_MOD, D_MODEL)

    w = layer(w_in).astype(BF16)
    o_wd, o_k, o_v, o_ad = WIDTH, WIDTH + DECAY_RANK, 2 * WIDTH + DECAY_RANK, 3 * WIDTH + DECAY_RANK
    o_ff = n_rwkv + N_FOX
    weights = [w[:, 0:o_wd], w[:, o_k:o_v], w[:, o_v:o_ad],
               jnp.concatenate([w[:, o_wd:o_k], w[:, o_ad:n_rwkv]], axis=1),
               w[:, n_rwkv:o_ff],
               jnp.pad(w[:, o_ff:o_ff + HEADS], ((0, 0), (0, LANES - HEADS))),
               w[:, o_ff + HEADS:]]
    mu = _reorder_rwkv_cols(layer(mu_shift)).reshape(1, n_rwkv)
    f_bias = jnp.pad(layer(fox_f_bias), (0, LANES - HEADS)).reshape(1, LANES)
    rkv, small, qx, kx, v_t, cum_t, gates = _projection(
        x, mod3, layer(norm1_g).reshape(1, D_MODEL), weights, mu, f_bias)

    zeros = jnp.zeros((DECAY_RANK, WIDTH), F32)
    w_lr = jnp.concatenate(
        [jnp.concatenate([layer(w_decay_up), zeros], axis=1),
         jnp.concatenate([zeros, layer(w_iclr_up)], axis=1)], axis=0).astype(BF16)
    vecs = jnp.stack([layer(decay_base), layer(iclr_base), layer(kk_scale), layer(k_iclr_mix),
                      layer(r_bonus).reshape(WIDTH), layer(lnx_w), layer(lnx_b),
                      jnp.zeros((WIDTH,), F32)], axis=0)
    y_a = _rwkv_mix(rkv, small, w_lr, layer(w_gate_up).astype(BF16), vecs, seg)

    y_b = _forgetting_attention(qx, kx, v_t, cum_t)

    return _tail(x, y_a, y_b, gates, mod3, layer(norm2_g).reshape(1, D_MODEL),
                 final_g.reshape(1, D_MODEL),
                 layer(w_o_rwkv).astype(BF16), layer(w_o_fox).astype(BF16),
                 layer(w_out).astype(BF16), layer(w_ff1).astype(BF16), layer(w_ff2).astype(BF16))
```

```python
import jax
import jax.numpy as jnp
from jax import lax
from jax.experimental import pallas as pl
from jax.experimental.pallas import tpu as pltpu

F32 = jnp.float32
BF16 = jnp.bfloat16

D_MODEL = 1024
HEAD_DIM = 64
HEADS = 8
WIDTH = HEADS * HEAD_DIM
DECAY_RANK = 64
ICLR_RANK = 64
GATE_RANK = 128
D_FF = 4 * D_MODEL
N_MOD = 6
NORM_EPS = 1e-6
GN_EPS = 64e-5

LANES = 128
PAIR = 2 * HEAD_DIM
N_PAIRS = HEADS // 2

LOG2E = 1.4426950408889634

N_SMALL = DECAY_RANK + ICLR_RANK + GATE_RANK
N_RKV = 3 * WIDTH
N_FOX = 3 * WIDTH
N_GATE = 2 * D_MODEL

PROJ_ROWS = 256
CHUNK = 128
RWKV_STEP_CHUNKS = 4
RWKV_MID_SLOTS = 4
FOX_BLOCK = 512
FOX_SUB = 32
FOX_VROWS = HEAD_DIM + 16
FOX_STEP_HEADS = 4
FOX_MASKED = -1e30
TAIL_ROWS = 512
FF_CHUNK = 1024

VMEM_LIMIT_BYTES = 56 * 1024 * 1024


def _dot(a, b):
    return jnp.dot(a.astype(BF16), b.astype(BF16), preferred_element_type=F32)


def _dot_nt(a, b):
    return lax.dot_general(a.astype(BF16), b.astype(BF16), (((1,), (1,)), ((), ())),
                           preferred_element_type=F32)


def _softplus(z):
    return jnp.maximum(z, 0.0) + jnp.log(1.0 + jnp.exp(-jnp.abs(z)))


def _sigmoid(z):
    return 1.0 / (1.0 + jnp.exp(-z))


def _bf16_parts(x, n):
    parts = []
    for _ in range(n):
        p = x.astype(BF16)
        parts.append(p)
        x = x - p.astype(F32)
    return parts


def _cumsum_rows(tri, x, n_parts):
    return sum(jnp.dot(tri, p, preferred_element_type=F32) for p in _bf16_parts(x, n_parts))


def _head_sums(x, seg):
    half = seg.shape[0]
    return jnp.concatenate([_dot(x[:, 0:half], seg), _dot(x[:, half:2 * half], seg)], axis=1)


def _lower_tri(n, dtype):
    row = lax.broadcasted_iota(jnp.int32, (n, n), 0)
    col = lax.broadcasted_iota(jnp.int32, (n, n), 1)
    return (col <= row).astype(dtype)


def _mod_kernel(c_ref, w_ref, b_ref, o_ref):
    c = c_ref[...]
    c_act = c * _sigmoid(c)
    o_ref[...] = _dot(c_act, w_ref[...]) + b_ref[...]


def _modulation(c, w_ada, b_ada):
    batch = c.shape[0]
    n = w_ada.shape[1]
    tn = 1536
    return pl.pallas_call(
        _mod_kernel,
        grid=(n // tn,),
        in_specs=[pl.BlockSpec((batch, D_MODEL), lambda j: (0, 0)),
                  pl.BlockSpec((D_MODEL, tn), lambda j: (0, j)),
                  pl.BlockSpec((1, tn), lambda j: (0, j))],
        out_specs=pl.BlockSpec((batch, tn), lambda j: (0, j)),
        out_shape=jax.ShapeDtypeStruct((batch, n), F32),
        compiler_params=pltpu.CompilerParams(dimension_semantics=("arbitrary",),
                                             vmem_limit_bytes=VMEM_LIMIT_BYTES),
        name="mod",
    )(c, w_ada, b_ada.reshape(1, n))


def _proj_normalise(x, mod_ref, g_ref):
    shift1 = mod_ref[0, 0:1, :]
    scale1 = mod_ref[0, 1:2, :]
    inv = lax.rsqrt(jnp.mean(x * x, axis=-1, keepdims=True) + NORM_EPS)
    return ((x * inv) * (g_ref[...] * (1.0 + scale1)) + shift1).astype(BF16)


def _proj_tile(h, rows, weights, mu_ref, fb_ref, route_ref, outs, shift_ref, carry_ref):
    wr_ref, wk_ref, wv_ref, ws_ref, wfox_ref, wff_ref, wgate_ref = weights
    rkv_ref, small_ref, qx_ref, kx_ref, vt_ref, cumt_ref, gate_ref = outs
    n = h.shape[0]

    n_tok = N_RKV + N_SMALL
    p = jnp.concatenate([jnp.dot(h, w[...], preferred_element_type=F32)
                         for w in (wr_ref, wk_ref, wv_ref, ws_ref)], axis=1)
    shift_ref[8:8 + n, :] = p
    prev = shift_ref[7:7 + n, :]
    shift_ref[7:8, :] = p[n - 1:n, :]
    mixed = p + mu_ref[...] * (prev - p)
    rkv_ref[0, rows, :] = mixed[:, 0:N_RKV].astype(BF16)
    small_ref[0, rows, :] = mixed[:, N_RKV:n_tok]

    pf = jnp.dot(h, wfox_ref[...], preferred_element_type=F32)
    vt_ref[0, :, rows] = jnp.transpose(pf[:, 2 * WIDTH:N_FOX]).astype(BF16)

    ff = jnp.dot(h, wff_ref[...], preferred_element_type=F32)
    logf = -_softplus(-(ff + fb_ref[...]))
    cum = _cumsum_rows(_lower_tri(n, BF16), logf, 3) + carry_ref[...]
    cumt_ref[0, :, rows] = jnp.transpose(cum)[0:HEADS, :]
    carry_ref[...] = cum[n - 1:n, :]

    parts = jnp.concatenate(_bf16_parts(cum * (-LOG2E), 3), axis=1)
    feats = jnp.dot(parts, route_ref[...], preferred_element_type=F32).astype(BF16)
    lane_b = lax.broadcasted_iota(jnp.int32, (n, LANES), 1)
    for pair in range(N_PAIRS):
        group = slice(pair * PAIR, (pair + 1) * PAIR)
        q_pair = (pf[:, group] * (LOG2E * HEAD_DIM ** -0.5)).astype(BF16)
        k_pair = pf[:, WIDTH + pair * PAIR:WIDTH + (pair + 1) * PAIR].astype(BF16)
        for hh in range(2):
            own = (lane_b // HEAD_DIM) == hh
            bias0 = HEAD_DIM * (1 - hh)
            is_bias = (lane_b >= bias0) & (lane_b < bias0 + 3)
            kx_ref[0, 2 * pair + hh, rows, :] = jnp.where(own, k_pair, feats[:, group])
            qx_ref[0, 2 * pair + hh, rows, :] = jnp.where(
                own, q_pair, jnp.where(is_bias, 1.0, 0.0).astype(BF16))

    pg = jnp.dot(h, wgate_ref[...], preferred_element_type=F32)
    gate_ref[0, rows, :] = _sigmoid(pg).astype(BF16)


def _proj_kernel(x_ref, x_next_ref, mod_ref, g_ref, wr_ref, wk_ref, wv_ref, ws_ref, wfox_ref,
                 wff_ref, wgate_ref, mu_ref, fb_ref, route_ref,
                 rkv_ref, small_ref, qx_ref, kx_ref, vt_ref, cumt_ref, gate_ref,
                 shift_ref, carry_ref, h_ref):
    tm = PROJ_ROWS
    weights = (wr_ref, wk_ref, wv_ref, ws_ref, wfox_ref, wff_ref, wgate_ref)
    outs = (rkv_ref, small_ref, qx_ref, kx_ref, vt_ref, cumt_ref, gate_ref)

    @pl.when(pl.program_id(1) == 0)
    def _():
        shift_ref[0:8, :] = jnp.zeros((8, N_RKV + N_SMALL), F32)
        carry_ref[...] = jnp.zeros_like(carry_ref)
        h_ref[0] = _proj_normalise(x_ref[0, 0:tm, :], mod_ref, g_ref)

    h_ref[1] = _proj_normalise(x_ref[0, tm:2 * tm, :], mod_ref, g_ref)
    _proj_tile(h_ref[0], slice(0, tm), weights, mu_ref, fb_ref, route_ref, outs,
               shift_ref, carry_ref)
    h_ref[0] = _proj_normalise(x_next_ref[0], mod_ref, g_ref)
    _proj_tile(h_ref[1], slice(tm, 2 * tm), weights, mu_ref, fb_ref, route_ref, outs,
               shift_ref, carry_ref)


def _projection(x, mod3, norm_g, weights, mu, f_bias):
    src = jnp.arange(3 * LANES)[:, None]
    dst = jnp.arange(WIDTH)[None, :]
    route = jnp.zeros((3 * LANES, WIDTH), jnp.bool_)
    for p in range(N_PAIRS):
        for h in range(2):
            for n in range(3):
                route |= ((src == n * LANES + 2 * p + h)
                          & (dst == p * PAIR + HEAD_DIM * (1 - h) + n))
    route = route.astype(BF16)
    batch, seq, _ = x.shape
    tm = PROJ_ROWS
    n_tiles = seq // tm
    grid = (batch, n_tiles // 2)
    const = lambda b, s: (0, 0)
    tile = lambda b, s: (b, s, 0)
    lanes = lambda b, s: (b, 0, s)
    ahead = lambda b, s: (b, jnp.minimum(2 * s + 2, n_tiles - 1), 0)
    return pl.pallas_call(
        _proj_kernel,
        grid=grid,
        in_specs=[pl.BlockSpec((1, 2 * tm, D_MODEL), tile),
                  pl.BlockSpec((1, tm, D_MODEL), ahead),
                  pl.BlockSpec((1, N_MOD, D_MODEL), lambda b, s: (b, 0, 0)),
                  pl.BlockSpec((1, D_MODEL), const),
                  *[pl.BlockSpec(w.shape, const, pipeline_mode=pl.Buffered(1)) for w in weights],
                  pl.BlockSpec((1, N_RKV + N_SMALL), const),
                  pl.BlockSpec((1, LANES), const),
                  pl.BlockSpec(route.shape, const)],
        out_specs=[pl.BlockSpec((1, 2 * tm, N_RKV), tile),
                   pl.BlockSpec((1, 2 * tm, N_SMALL), tile),
                   pl.BlockSpec((1, HEADS, 2 * tm, LANES), lambda b, s: (b, 0, s, 0)),
                   pl.BlockSpec((1, HEADS, 2 * tm, LANES), lambda b, s: (b, 0, s, 0)),
                   pl.BlockSpec((1, WIDTH, 2 * tm), lanes),
                   pl.BlockSpec((1, HEADS, 2 * tm), lanes),
                   pl.BlockSpec((1, 2 * tm, N_GATE), tile)],
        out_shape=[jax.ShapeDtypeStruct((batch, seq, N_RKV), BF16),
                   jax.ShapeDtypeStruct((batch, seq, N_SMALL), F32),
                   jax.ShapeDtypeStruct((batch, HEADS, seq, LANES), BF16),
                   jax.ShapeDtypeStruct((batch, HEADS, seq, LANES), BF16),
                   jax.ShapeDtypeStruct((batch, WIDTH, seq), BF16),
                   jax.ShapeDtypeStruct((batch, HEADS, seq), F32),
                   jax.ShapeDtypeStruct((batch, seq, N_GATE), BF16)],
        scratch_shapes=[pltpu.VMEM((tm + 8, N_RKV + N_SMALL), F32),
                        pltpu.VMEM((1, LANES), F32),
                        pltpu.VMEM((2, tm, D_MODEL), BF16)],
        compiler_params=pltpu.CompilerParams(dimension_semantics=("arbitrary", "arbitrary"),
                                             vmem_limit_bytes=VMEM_LIMIT_BYTES),
        name="proj",
    )(x, x, mod3, norm_g, *weights, mu, f_bias, route)


def _rwkv_prepare(rkv, small, w_lr_ref, w_gate_ref, vec_ref, seg_ref, slot, prep):
    am_ref, rm_ref, vb_ref, bt_ref, kt_ref, bh_ref, kh_ref, dend_ref, g_ref, bv_ref = prep
    c = CHUNK
    r = rkv[:, 0:WIDTH].astype(F32)
    k_raw = rkv[:, WIDTH:2 * WIDTH].astype(F32)
    v = rkv[:, 2 * WIDTH:3 * WIDTH].astype(F32)
    decay_base, iclr_base, kk_scale, k_mix, r_bonus = (vec_ref[i:i + 1, :] for i in range(5))

    lane = lax.broadcasted_iota(jnp.int32, (1, LANES), 1)
    lr_in = jnp.where(lane < DECAY_RANK, jnp.tanh(small[:, 0:LANES]), small[:, 0:LANES])
    lr = _dot(lr_in, w_lr_ref[...])
    w_log = -_softplus(-(decay_base + lr[:, 0:WIDTH])) - 0.5
    log_decay = -jnp.exp(w_log)
    a = _sigmoid(iclr_base + lr[:, WIDTH:2 * WIDTH])
    g_ref[slot] = _dot(_sigmoid(small[:, LANES:2 * LANES]), w_gate_ref[...])
    yield

    seg = seg_ref[...]
    kk = k_raw * kk_scale
    kk = kk * lax.rsqrt(jnp.maximum(_head_sums(kk * kk, seg), 1e-24))
    k = k_raw * (1.0 + (a - 1.0) * k_mix)
    b_vec = kk * a
    yield

    cs = _cumsum_rows(_lower_tri(c, BF16), log_decay, 2)
    cs_end = cs[c - 1:c, :]
    am_ref[slot] = (-kk * jnp.exp(cs - log_decay)).astype(BF16)
    rm_ref[slot] = (r * jnp.exp(cs)).astype(BF16)
    vb_ref[slot] = v.astype(BF16)
    yield
    w_inv = jnp.exp(-cs)
    bt_ref[slot] = jnp.transpose(b_vec * w_inv).astype(BF16)
    yield
    kt_ref[slot] = jnp.transpose(k * w_inv).astype(BF16)
    yield
    w_end = jnp.exp(cs_end - cs)
    bh_ref[slot] = jnp.transpose(b_vec * w_end).astype(BF16)
    yield
    kh_ref[slot] = jnp.transpose(k * w_end).astype(BF16)
    dend_ref[slot] = jnp.broadcast_to(jnp.exp(cs_end), (8, WIDTH))
    yield
    bv_ref[slot] = _head_sums(r * k * r_bonus, seg) * v


def _rwkv_chunk_matrices(slot, mid_slot, prep, mid):
    am_ref, rm_ref, vb_ref, bt_ref, kt_ref = prep[:5]
    pq_ref, lhs_ref = mid
    c = CHUNK
    lane = lax.broadcasted_iota(jnp.int32, (1, LANES), 1)
    row = lax.broadcasted_iota(jnp.int32, (c, c), 0)
    col = lax.broadcasted_iota(jnp.int32, (c, c), 1)
    strict = col < row
    incl = col <= row
    eye = (col == row).astype(F32)

    sls = [slice(p * PAIR, (p + 1) * PAIR) for p in range(N_PAIRS)]
    heads = [(p, h) for p in range(N_PAIRS) for h in range(2)]
    a_m, a_ab, a_ak = {}, {}, {}
    for i, (p, h) in enumerate(heads):
        head = (lane // HEAD_DIM) == h
        a_m[p, h] = jnp.where(head, am_ref[slot, :, sls[p]].astype(F32), 0.0).astype(BF16)
        r_m = jnp.where(head, rm_ref[slot, :, sls[p]].astype(F32), 0.0).astype(BF16)
        rhs = jnp.concatenate([bt_ref[slot, sls[p], :], kt_ref[slot, sls[p], :]], axis=1)
        big = jnp.dot(jnp.concatenate([a_m[p, h], r_m], axis=0), rhs,
                      preferred_element_type=F32)
        a_ab[p, h] = jnp.where(strict, big[0:c, 0:c], 0.0)
        a_ak[p, h] = jnp.where(strict, big[0:c, c:2 * c], 0.0).astype(BF16)
        lhs_ref[mid_slot, i] = jnp.concatenate(
            [r_m, jnp.where(incl, big[c:2 * c, 0:c], 0.0).astype(BF16),
             jnp.where(incl, big[c:2 * c, c:2 * c], 0.0).astype(BF16)], axis=1)
    yield
    t_inv = {hd: eye + a_ab[hd] for hd in heads}
    m_pow = {hd: _dot(a_ab[hd], a_ab[hd]) for hd in heads}
    yield
    for k in range(1, 7):
        skip = (2 ** k) // 16 * 16
        last = k == 6
        for hd in heads:
            rhs = t_inv[hd] if last else jnp.concatenate([m_pow[hd], t_inv[hd]], axis=1)
            upd = _dot(m_pow[hd][skip:c, :], rhs)
            if skip:
                upd = jnp.concatenate([jnp.zeros((skip, upd.shape[1]), F32), upd], axis=0)
            if last:
                t_inv[hd] = t_inv[hd] + upd
            else:
                m_pow[hd] = upd[:, 0:c]
                t_inv[hd] = t_inv[hd] + upd[:, c:2 * c]
        yield
    ak_v = {(p, h): _dot(a_ak[p, h], vb_ref[slot, :, sls[p]]) for (p, h) in heads}
    yield
    for i, hd in enumerate(heads):
        pq_ref[mid_slot, i] = _dot(t_inv[hd],
                               jnp.concatenate([a_m[hd].astype(F32), ak_v[hd]], axis=1))


def _rwkv_chunk_state(slot, mid_slot, prep, mid, vec_ref, seg_ref, z_ref, out_ref, rows):
    vb_ref, bh_ref, kh_ref, dend_ref, g_ref, bv_ref = (prep[i] for i in (2, 5, 6, 7, 8, 9))
    pq_ref, lhs_ref = mid
    c = CHUNK
    ln_w, ln_b = vec_ref[5:6, :], vec_ref[6:7, :]
    lane = lax.broadcasted_iota(jnp.int32, (1, LANES), 1)
    row = lax.broadcasted_iota(jnp.int32, (c, c), 0)
    col = lax.broadcasted_iota(jnp.int32, (c, c), 1)
    eye = (col == row).astype(F32)
    same_head = (row // HEAD_DIM) == (col // HEAD_DIM)
    pairs = range(N_PAIRS)
    sls = [slice(p * PAIR, (p + 1) * PAIR) for p in pairs]
    heads = [(p, h) for p in pairs for h in range(2)]
    v_b = [vb_ref[slot, :, sl] for sl in sls]
    z_f = [z_ref[p] for p in pairs]
    z_b = [z.astype(BF16) for z in z_f]

    u = {(p, h): (_dot(pq_ref[mid_slot, i, :, 0:PAIR], z_b[p])
                  + pq_ref[mid_slot, i, :, PAIR:2 * PAIR])
         for i, (p, h) in enumerate(heads)}
    yield
    y_hd = {(p, h): jnp.dot(
        lhs_ref[mid_slot, i], jnp.concatenate([z_b[p], u[p, h].astype(BF16), v_b[p]], axis=0),
        preferred_element_type=F32) for i, (p, h) in enumerate(heads)}
    first = lane < HEAD_DIM
    y_pairs = []
    for p in pairs:
        u_p = jnp.where(first, u[p, 0], u[p, 1])
        y_pairs.append(jnp.where(first, y_hd[p, 0], y_hd[p, 1]))
        d_col = jnp.sum(eye * dend_ref[slot, 0:1, sls[p]], axis=1, keepdims=True)
        z_new = d_col * z_f[p] + jnp.dot(
            jnp.concatenate([bh_ref[slot, sls[p], :], kh_ref[slot, sls[p], :]], axis=1),
            jnp.concatenate([u_p.astype(BF16), v_b[p]], axis=0), preferred_element_type=F32)
        z_ref[p] = jnp.where(same_head, z_new, 0.0)
    yield

    seg = seg_ref[...]
    y = jnp.concatenate(y_pairs, axis=1)
    mean = _head_sums(y, seg) * (1.0 / HEAD_DIM)
    yc = y - mean
    var = _head_sums(yc * yc, seg) * (1.0 / HEAD_DIM)
    y = yc * lax.rsqrt(var + GN_EPS) * ln_w + ln_b
    out_ref[0, rows, :] = ((y + bv_ref[slot]) * g_ref[slot]).astype(BF16)


def _interleave(*stages, late=()):
    live = list(stages)
    late = list(late)
    while live:
        for gen in list(live):
            try:
                next(gen)
            except StopIteration:
                live.remove(gen)
                if late and gen is stages[0]:
                    live.extend(late)
                    late = []


def _in_turn(*stages):
    for gen in stages:
        yield from gen


def _rwkv_kernel(rkv_ref, small_ref, rkv_next_ref, small_next_ref,
                 w_lr_ref, w_gate_ref, vec_ref, seg_ref, out_ref, z_ref, pq_ref, lhs_ref, *prep):
    c = CHUNK
    params = (w_lr_ref, w_gate_ref, vec_ref, seg_ref)
    mid = (pq_ref, lhs_ref)
    rows = [slice(i * c, (i + 1) * c) for i in range(RWKV_STEP_CHUNKS)]

    def prepare(r_ref, s_ref, i, slot):
        return _rwkv_prepare(r_ref[0, rows[i], :], s_ref[0, rows[i], :], *params, slot, prep)

    def matrices(i):
        return _rwkv_chunk_matrices(i, i % RWKV_MID_SLOTS, prep, mid)

    def state(i):
        return _rwkv_chunk_state(i, i % RWKV_MID_SLOTS, prep, mid, vec_ref, seg_ref, z_ref,
                                 out_ref, rows[i])

    @pl.when(pl.program_id(1) == 0)
    def _():
        z_ref[...] = jnp.zeros_like(z_ref)
        _interleave(prepare(rkv_ref, small_ref, 0, 0), prepare(rkv_ref, small_ref, 1, 1))

    n_pairs = RWKV_STEP_CHUNKS // 2
    for j in range(n_pairs):
        a, b = 2 * j, 2 * j + 1
        streams = [matrices(a), matrices(b)]
        if j > 0:
            streams.insert(0, _in_turn(state(a - 2), state(b - 2)))
        if j + 1 < n_pairs:
            ahead = [prepare(rkv_ref, small_ref, a + 2, a + 2),
                     prepare(rkv_ref, small_ref, b + 2, b + 2)]
        else:
            ahead = [prepare(rkv_next_ref, small_next_ref, 0, 0),
                     prepare(rkv_next_ref, small_next_ref, 1, 1)]
        if ahead and j == 1 and j + 1 == n_pairs:
            _interleave(*streams, late=ahead)
        else:
            _interleave(*streams, *ahead)
    _interleave(_in_turn(state(RWKV_STEP_CHUNKS - 2), state(RWKV_STEP_CHUNKS - 1)))


def _rwkv_mix(rkv, small, w_lr, w_gate, vecs, seg):
    batch, seq, _ = rkv.shape
    c = CHUNK
    n = RWKV_STEP_CHUNKS
    n_pairs_total = seq // (2 * c)
    grid = (batch, seq // (n * c))
    const = lambda b, s: (0, 0)
    tile = lambda b, s: (b, s, 0)
    ahead = lambda b, s: (b, jnp.minimum(n // 2 * (s + 1), n_pairs_total - 1), 0)
    slots = lambda shape, dtype: pltpu.VMEM((n,) + shape, dtype)
    mid_slots = lambda shape, dtype: pltpu.VMEM((RWKV_MID_SLOTS,) + shape, dtype)
    return pl.pallas_call(
        _rwkv_kernel,
        grid=grid,
        in_specs=[pl.BlockSpec((1, n * c, N_RKV), tile),
                  pl.BlockSpec((1, n * c, N_SMALL), tile),
                  pl.BlockSpec((1, 2 * c, N_RKV), ahead),
                  pl.BlockSpec((1, 2 * c, N_SMALL), ahead),
                  pl.BlockSpec(w_lr.shape, const),
                  pl.BlockSpec(w_gate.shape, const),
                  pl.BlockSpec(vecs.shape, const),
                  pl.BlockSpec(seg.shape, const)],
        out_specs=pl.BlockSpec((1, n * c, WIDTH), tile),
        out_shape=jax.ShapeDtypeStruct((batch, seq, WIDTH), BF16),
        scratch_shapes=[pltpu.VMEM((N_PAIRS, PAIR, PAIR), F32),
                        mid_slots((HEADS, c, 2 * PAIR), F32),
                        mid_slots((HEADS, c, 3 * PAIR), BF16),
                        slots((c, WIDTH), BF16),
                        slots((c, WIDTH), BF16),
                        slots((c, WIDTH), BF16),
                        slots((WIDTH, c), BF16),
                        slots((WIDTH, c), BF16),
                        slots((WIDTH, c), BF16),
                        slots((WIDTH, c), BF16),
                        slots((8, WIDTH), F32),
                        slots((c, WIDTH), F32),
                        slots((c, WIDTH), F32)],
        compiler_params=pltpu.CompilerParams(dimension_semantics=("arbitrary", "arbitrary"),
                                             vmem_limit_bytes=VMEM_LIMIT_BYTES),
        name="rwkv",
    )(rkv, small, rkv, small, w_lr, w_gate, vecs, seg)


def _fox_kernel(qx_ref, kx_ref, vt_ref, cumt_ref, mask_ref, o_ref, vx_ref, t_ref, p_ref):
    heads = range(FOX_STEP_HEADS)
    head0 = FOX_STEP_HEADS * pl.program_id(1)
    blk = FOX_BLOCK
    seq = kx_ref.shape[2]
    n_blk = seq // blk
    ones_row = lax.broadcasted_iota(jnp.int32, (FOX_VROWS - HEAD_DIM, seq), 0) == 0
    for h in heads:
        vx_ref[h, 0:HEAD_DIM, :] = vt_ref[0, h * HEAD_DIM:(h + 1) * HEAD_DIM, :]
        vx_ref[h, HEAD_DIM:FOX_VROWS, :] = ones_row.astype(BF16)

    def rows_of(b):
        return pl.ds(pl.multiple_of(b * blk, blk), blk)

    def following(qi, kj):
        wrap = kj == qi
        return jnp.where(wrap, qi + 1, qi), jnp.where(wrap, 0, kj + 1)

    def scores(pair, slot, diagonal, out):
        qi, kj = pair
        qi = jnp.minimum(qi, n_blk - 1)
        mx = []
        for h in heads:
            t = _dot_nt(kx_ref[0, h, rows_of(kj), :], qx_ref[0, h, rows_of(qi), :])
            if diagonal:
                t = t + mask_ref[...]
            t_ref[slot, h] = t
            c_q = cumt_ref[0, pl.ds(head0 + h, 1), rows_of(qi)] * LOG2E
            mx.append(jnp.max(t, axis=0, keepdims=True) + c_q)
            yield
        out["mx"] = tuple(mx)

    def softmax(pair, slot, m, mx, out):
        qi, kj = pair
        sub = FOX_SUB
        q_rows = rows_of(jnp.minimum(qi, n_blk - 1))
        m_out, alpha_out = [], []
        for h in heads:
            m_old = jnp.where(kj == 0, -jnp.inf, m[h])
            c_q = cumt_ref[0, pl.ds(head0 + h, 1), q_rows] * LOG2E
            m_new = jnp.maximum(m_old, mx[h])
            shift = c_q - m_new
            for i in range(blk // sub):
                rows = slice(i * sub, (i + 1) * sub)
                p_ref[slot, h, rows, :] = jnp.exp2(t_ref[slot, h, rows, :] + shift).astype(BF16)
                if i % 2 == 1:
                    yield
            m_out.append(m_new)
            alpha_out.append(jnp.exp2(m_old - m_new))
        out["m"], out["alpha"] = tuple(m_out), tuple(alpha_out)

    def accumulate(pair, slot, alpha, acc, out):
        qi, kj = pair
        new = []
        for h in heads:
            new.append(alpha[h] * acc[h] + jnp.dot(vx_ref[h, :, rows_of(kj)], p_ref[slot, h],
                                                   preferred_element_type=F32))
            yield
        o_t = jnp.concatenate([a[0:HEAD_DIM] / a[HEAD_DIM:HEAD_DIM + 1] for a in new], axis=0)
        o_ref[0, :, rows_of(qi)] = o_t.astype(BF16)
        out["acc"] = tuple(new)

    def step(pair_s, diagonal, pair_p, pair_a, slot_s, m, mx, alpha, acc):
        out = {}
        _interleave(softmax(pair_p, 1 - slot_s, m, mx, out),
                    scores(pair_s, slot_s, diagonal, out),
                    accumulate(pair_a, slot_s, alpha, acc, out))
        return out

    def two_pairs(_, carry):
        pair0, pair1 = carry[0], carry[1]
        pair2 = following(*pair1)
        pair3 = following(*pair2)

        def trip(diag2, diag3):
            def run(carry):
                _, _, m, mx1, alpha0, acc = carry
                a = step(pair2, diag2, pair1, pair0, 0, m, mx1, alpha0, acc)
                b = step(pair3, diag3, pair2, pair1, 1, a["m"], a["mx"], a["alpha"], a["acc"])
                return pair2, pair3, b["m"], b["mx"], b["alpha"], b["acc"]
            return run

        which = (2 * (pair2[0] == pair2[1]).astype(jnp.int32)
                 + (pair3[0] == pair3[1]).astype(jnp.int32))
        return lax.switch(which, [trip(False, False), trip(False, True),
                                  trip(True, False), trip(True, True)], carry)

    n_pairs = n_blk * (n_blk + 1) // 2
    assert n_pairs % 2 == 0, "two pairs per trip"
    zero = jnp.int32(0)
    pair0 = (zero, zero)
    pair1 = following(*pair0)
    first, second = {}, {}
    _interleave(scores(pair0, 0, True, first))
    _interleave(scores(pair1, 1, False, second))
    m = tuple(jnp.full((1, blk), -jnp.inf, F32) for _ in heads)
    _interleave(softmax(pair0, 0, m, first["mx"], first))
    acc = tuple(jnp.zeros((FOX_VROWS, blk), F32) for _ in heads)
    lax.fori_loop(0, n_pairs // 2, two_pairs,
                  (pair0, pair1, first["m"], second["mx"], first["alpha"], acc))


def _forgetting_attention(qx, kx, v_t, cum_t):
    batch, _, seq, _ = qx.shape
    blk = FOX_BLOCK
    key = lax.broadcasted_iota(jnp.int32, (blk, blk), 0)
    qry = lax.broadcasted_iota(jnp.int32, (blk, blk), 1)
    mask = jnp.where(key <= qry, 0.0, FOX_MASKED).astype(F32)
    nh = FOX_STEP_HEADS
    width = nh * HEAD_DIM
    whole = lambda b, g: (b, 0, 0)
    n_groups = HEADS // nh
    return pl.pallas_call(
        _fox_kernel,
        grid=(batch, n_groups),
        in_specs=[pl.BlockSpec((1, nh, seq, LANES), lambda b, g: (b, g, 0, 0)),
                  pl.BlockSpec((1, nh, seq, LANES), lambda b, g: (b, g, 0, 0)),
                  pl.BlockSpec((1, width, seq), lambda b, g: (b, g, 0)),
                  pl.BlockSpec((1, HEADS, seq), whole),
                  pl.BlockSpec((blk, blk), lambda b, g: (0, 0))],
        out_specs=pl.BlockSpec((1, width, seq), lambda b, g: (b, g, 0)),
        out_shape=jax.ShapeDtypeStruct((batch, WIDTH, seq), BF16),
        scratch_shapes=[pltpu.VMEM((nh, FOX_VROWS, seq), BF16),
                        pltpu.VMEM((2, nh, blk, blk), F32),
                        pltpu.VMEM((2, nh, blk, blk), BF16)],
        compiler_params=pltpu.CompilerParams(
            dimension_semantics=("arbitrary", "arbitrary"),
            vmem_limit_bytes=VMEM_LIMIT_BYTES),
        name="fox",
    )(qx, kx, v_t, cum_t, mask)


def _tail_kernel(x_ref, ya_ref, ybt_ref, gate_ref, mod_ref, g2_ref, gf_ref,
                 woa_ref, wob_ref, wout_ref, w1_ref, w2_ref, o_ref):
    x = x_ref[0]
    gate1 = mod_ref[0, 2:3, :]
    shift2 = mod_ref[0, 3:4, :]
    scale2 = mod_ref[0, 4:5, :]
    gate2 = mod_ref[0, 5:6, :]

    merged = (gate_ref[0, :, 0:D_MODEL].astype(F32)
              * jnp.dot(ya_ref[0], woa_ref[...], preferred_element_type=F32)
              + gate_ref[0, :, D_MODEL:N_GATE].astype(F32)
              * lax.dot_general(ybt_ref[0], wob_ref[...], (((0,), (0,)), ((), ())),
                                preferred_element_type=F32))
    x = x + gate1 * _dot(merged, wout_ref[...])

    inv = lax.rsqrt(jnp.mean(x * x, axis=-1, keepdims=True) + NORM_EPS)
    h2 = ((x * inv) * g2_ref[...] * (1.0 + scale2) + shift2).astype(BF16)
    ff = jnp.zeros_like(x)
    for j in range(D_FF // FF_CHUNK):
        cols = slice(j * FF_CHUNK, (j + 1) * FF_CHUNK)
        hid = jnp.maximum(jnp.dot(h2, w1_ref[:, cols], preferred_element_type=F32), 0.0)
        ff = ff + _dot(hid * hid, w2_ref[cols, :])
    x = x + gate2 * ff

    inv = lax.rsqrt(jnp.mean(x * x, axis=-1, keepdims=True) + NORM_EPS)
    o_ref[0] = (x * inv) * gf_ref[...]


def _tail(x, y_a, y_b, gates, mod3, norm2_g, final_g, w_oa, w_ob, w_out, w_ff1, w_ff2):
    batch, seq, _ = x.shape
    tm = TAIL_ROWS
    grid = (batch, seq // tm)
    const = lambda b, s: (0, 0)
    tile = lambda b, s: (b, s, 0)
    resident = lambda a: pl.BlockSpec(a.shape, const, pipeline_mode=pl.Buffered(1))
    return pl.pallas_call(
        _tail_kernel,
        grid=grid,
        in_specs=[pl.BlockSpec((1, tm, D_MODEL), tile),
                  pl.BlockSpec((1, tm, WIDTH), tile),
                  pl.BlockSpec((1, WIDTH, tm), lambda b, s: (b, 0, s)),
                  pl.BlockSpec((1, tm, N_GATE), tile),
                  pl.BlockSpec((1, N_MOD, D_MODEL), lambda b, s: (b, 0, 0)),
                  pl.BlockSpec((1, D_MODEL), const),
                  pl.BlockSpec((1, D_MODEL), const),
                  resident(w_oa), resident(w_ob), resident(w_out),
                  resident(w_ff1), resident(w_ff2)],
        out_specs=pl.BlockSpec((1, tm, D_MODEL), tile),
        out_shape=jax.ShapeDtypeStruct((batch, seq, D_MODEL), F32),
        compiler_params=pltpu.CompilerParams(dimension_semantics=("arbitrary", "arbitrary"),
                                             vmem_limit_bytes=VMEM_LIMIT_BYTES),
        name="tail",
    )(x, y_a, y_b, gates, mod3, norm2_g, final_g, w_oa, w_ob, w_out, w_ff1, w_ff2)


def _drop_depth_axis(t):
    return t.reshape(t.shape[1:])


def _reorder_rwkv_cols(t):
    o = 0
    r = t[..., o:o + WIDTH]; o += WIDTH
    wd = t[..., o:o + DECAY_RANK]; o += DECAY_RANK
    k = t[..., o:o + WIDTH]; o += WIDTH
    v = t[..., o:o + WIDTH]; o += WIDTH
    ad = t[..., o:o + ICLR_RANK]; o += ICLR_RANK
    gd = t[..., o:o + GATE_RANK]
    return jnp.concatenate([r, k, v, wd, ad, gd], axis=-1)


def kernel(x, c, w_ada, b_ada, norm1_g, w_in, mu_shift, w_decay_up, decay_base, w_iclr_up, iclr_base, w_gate_up, kk_scale, k_iclr_mix, r_bonus, lnx_w, lnx_b, fox_f_bias, w_o_rwkv, w_o_fox, w_out, norm2_g, w_ff1, w_ff2, final_g):
    assert w_ada.shape[0] == 1, "the tail kernel fuses the final norm: single layer only"
    layer = _drop_depth_axis
    n_rwkv = N_RKV + N_SMALL
    seg_id = jnp.arange(WIDTH // 2) // HEAD_DIM
    seg = (seg_id[:, None] == seg_id[None, :]).astype(BF16)

    mod3 = _modulation(c, layer(w_ada), layer(b_ada)).reshape(-1, N_MOD, D_MODEL)

    w = layer(w_in).astype(BF16)
    o_wd, o_k, o_v, o_ad = WIDTH, WIDTH + DECAY_RANK, 2 * WIDTH + DECAY_RANK, 3 * WIDTH + DECAY_RANK
    o_ff = n_rwkv + N_FOX
    weights = [w[:, 0:o_wd], w[:, o_k:o_v], w[:, o_v:o_ad],
               jnp.concatenate([w[:, o_wd:o_k], w[:, o_ad:n_rwkv]], axis=1),
               w[:, n_rwkv:o_ff],
               jnp.pad(w[:, o_ff:o_ff + HEADS], ((0, 0), (0, LANES - HEADS))),
               w[:, o_ff + HEADS:]]
    mu = _reorder_rwkv_cols(layer(mu_shift)).reshape(1, n_rwkv)
    f_bias = jnp.pad(layer(fox_f_bias), (0, LANES - HEADS)).reshape(1, LANES)
    rkv, small, qx, kx, v_t, cum_t, gates = _projection(
        x, mod3, layer(norm1_g).reshape(1, D_MODEL), weights, mu, f_bias)

    zeros = jnp.zeros((DECAY_RANK, WIDTH), F32)
    w_lr = jnp.concatenate(
        [jnp.concatenate([layer(w_decay_up), zeros], axis=1),
         jnp.concatenate([zeros, layer(w_iclr_up)], axis=1)], axis=0).astype(BF16)
    vecs = jnp.stack([layer(decay_base), layer(iclr_base), layer(kk_scale), layer(k_iclr_mix),
                      layer(r_bonus).reshape(WIDTH), layer(lnx_w), layer(lnx_b),
                      jnp.zeros((WIDTH,), F32)], axis=0)
    y_a = _rwkv_mix(rkv, small, w_lr, layer(w_gate_up).astype(BF16), vecs, seg)

    y_b = _forgetting_attention(qx, kx, v_t, cum_t)

    return _tail(x, y_a, y_b, gates, mod3, layer(norm2_g).reshape(1, D_MODEL),
                 final_g.reshape(1, D_MODEL),
                 layer(w_o_rwkv).astype(BF16), layer(w_o_fox).astype(BF16),
                 layer(w_out).astype(BF16), layer(w_ff1).astype(BF16), layer(w_ff2).astype(BF16))
```

```python
import jax
import jax.numpy as jnp
from jax import lax
from jax.experimental import pallas as pl
from jax.experimental.pallas import tpu as pltpu

F32 = jnp.float32
BF16 = jnp.bfloat16

D_MODEL = 1024
HEAD_DIM = 64
HEADS = 8
WIDTH = HEADS * HEAD_DIM
DECAY_RANK = 64
ICLR_RANK = 64
GATE_RANK = 128
D_FF = 4 * D_MODEL
N_MOD = 6
NORM_EPS = 1e-6
GN_EPS = 64e-5

LANES = 128
PAIR = 2 * HEAD_DIM
N_PAIRS = HEADS // 2

LOG2E = 1.4426950408889634

N_SMALL = DECAY_RANK + ICLR_RANK + GATE_RANK
N_RKV = 3 * WIDTH
N_FOX = 3 * WIDTH
N_GATE = 2 * D_MODEL

PROJ_ROWS = 256
CHUNK = 128
RWKV_STEP_CHUNKS = 4
RWKV_MID_SLOTS = 4
FOX_BLOCK = 512
FOX_SUB = 32
FOX_VROWS = HEAD_DIM + 16
FOX_STEP_HEADS = 4
FOX_MASKED = -1e30
TAIL_ROWS = 512
FF_CHUNK = 1024

VMEM_LIMIT_BYTES = 56 * 1024 * 1024


def _dot(a, b):
    return jnp.dot(a.astype(BF16), b.astype(BF16), preferred_element_type=F32)


def _dot_nt(a, b):
    return lax.dot_general(a.astype(BF16), b.astype(BF16), (((1,), (1,)), ((), ())),
                           preferred_element_type=F32)


def _softplus(z):
    return jnp.maximum(z, 0.0) + jnp.log(1.0 + jnp.exp(-jnp.abs(z)))


def _sigmoid(z):
    return 1.0 / (1.0 + jnp.exp(-z))


def _bf16_parts(x, n):
    parts = []
    for _ in range(n):
        p = x.astype(BF16)
        parts.append(p)
        x = x - p.astype(F32)
    return parts


def _cumsum_rows(tri, x, n_parts):
    return sum(jnp.dot(tri, p, preferred_element_type=F32) for p in _bf16_parts(x, n_parts))


def _head_sums(x, seg):
    half = seg.shape[0]
    return jnp.concatenate([_dot(x[:, 0:half], seg), _dot(x[:, half:2 * half], seg)], axis=1)


def _lower_tri(n, dtype):
    row = lax.broadcasted_iota(jnp.int32, (n, n), 0)
    col = lax.broadcasted_iota(jnp.int32, (n, n), 1)
    return (col <= row).astype(dtype)


def _mod_kernel(c_ref, w_ref, b_ref, o_ref):
    c = c_ref[...]
    c_act = c * _sigmoid(c)
    o_ref[...] = _dot(c_act, w_ref[...]) + b_ref[...]


def _modulation(c, w_ada, b_ada):
    batch = c.shape[0]
    n = w_ada.shape[1]
    tn = 1536
    return pl.pallas_call(
        _mod_kernel,
        grid=(n // tn,),
        in_specs=[pl.BlockSpec((batch, D_MODEL), lambda j: (0, 0)),
                  pl.BlockSpec((D_MODEL, tn), lambda j: (0, j)),
                  pl.BlockSpec((1, tn), lambda j: (0, j))],
        out_specs=pl.BlockSpec((batch, tn), lambda j: (0, j)),
        out_shape=jax.ShapeDtypeStruct((batch, n), F32),
        compiler_params=pltpu.CompilerParams(dimension_semantics=("arbitrary",),
                                             vmem_limit_bytes=VMEM_LIMIT_BYTES),
        name="mod",
    )(c, w_ada, b_ada.reshape(1, n))


def _proj_normalise(x, mod_ref, g_ref):
    shift1 = mod_ref[0, 0:1, :]
    scale1 = mod_ref[0, 1:2, :]
    inv = lax.rsqrt(jnp.mean(x * x, axis=-1, keepdims=True) + NORM_EPS)
    return ((x * inv) * (g_ref[...] * (1.0 + scale1)) + shift1).astype(BF16)


def _proj_tile(h, rows, weights, mu_ref, fb_ref, route_ref, outs, shift_ref, carry_ref):
    wr_ref, wk_ref, wv_ref, ws_ref, wfox_ref, wff_ref, wgate_ref = weights
    rkv_ref, small_ref, qx_ref, kx_ref, vt_ref, cumt_ref, gate_ref = outs
    n = h.shape[0]

    ff = jnp.dot(h, wff_ref[...], preferred_element_type=F32)
    logf = -_softplus(-(ff + fb_ref[...]))
    cum = _cumsum_rows(_lower_tri(n, BF16), logf, 3) + carry_ref[...]
    cumt_ref[0, :, rows] = jnp.transpose(cum)[0:HEADS, :]
    carry_ref[...] = cum[n - 1:n, :]

    n_tok = N_RKV + N_SMALL
    p = jnp.concatenate([jnp.dot(h, w[...], preferred_element_type=F32)
                         for w in (wr_ref, wk_ref, wv_ref, ws_ref)], axis=1)
    shift_ref[8:8 + n, :] = p
    prev = shift_ref[7:7 + n, :]
    shift_ref[7:8, :] = p[n - 1:n, :]
    mixed = p + mu_ref[...] * (prev - p)
    rkv_ref[0, rows, :] = mixed[:, 0:N_RKV].astype(BF16)
    small_ref[0, rows, :] = mixed[:, N_RKV:n_tok]

    pf = jnp.dot(h, wfox_ref[...], preferred_element_type=F32)
    vt_ref[0, :, rows] = jnp.transpose(pf[:, 2 * WIDTH:N_FOX]).astype(BF16)

    parts = jnp.concatenate(_bf16_parts(cum * (-LOG2E), 3), axis=1)
    feats = jnp.dot(parts, route_ref[...], preferred_element_type=F32).astype(BF16)
    lane_b = lax.broadcasted_iota(jnp.int32, (n, LANES), 1)
    for pair in range(N_PAIRS):
        group = slice(pair * PAIR, (pair + 1) * PAIR)
        q_pair = (pf[:, group] * (LOG2E * HEAD_DIM ** -0.5)).astype(BF16)
        k_pair = pf[:, WIDTH + pair * PAIR:WIDTH + (pair + 1) * PAIR].astype(BF16)
        for hh in range(2):
            own = (lane_b // HEAD_DIM) == hh
            bias0 = HEAD_DIM * (1 - hh)
            is_bias = (lane_b >= bias0) & (lane_b < bias0 + 3)
            kx_ref[0, 2 * pair + hh, rows, :] = jnp.where(own, k_pair, feats[:, group])
            qx_ref[0, 2 * pair + hh, rows, :] = jnp.where(
                own, q_pair, jnp.where(is_bias, 1.0, 0.0).astype(BF16))

    pg = jnp.dot(h, wgate_ref[...], preferred_element_type=F32)
    gate_ref[0, rows, :] = _sigmoid(pg).astype(BF16)


def _proj_kernel(x_ref, x_next_ref, mod_ref, g_ref, wr_ref, wk_ref, wv_ref, ws_ref, wfox_ref,
                 wff_ref, wgate_ref, mu_ref, fb_ref, route_ref,
                 rkv_ref, small_ref, qx_ref, kx_ref, vt_ref, cumt_ref, gate_ref,
                 shift_ref, carry_ref, h_ref):
    tm = PROJ_ROWS
    weights = (wr_ref, wk_ref, wv_ref, ws_ref, wfox_ref, wff_ref, wgate_ref)
    outs = (rkv_ref, small_ref, qx_ref, kx_ref, vt_ref, cumt_ref, gate_ref)

    @pl.when(pl.program_id(1) == 0)
    def _():
        shift_ref[0:8, :] = jnp.zeros((8, N_RKV + N_SMALL), F32)
        carry_ref[...] = jnp.zeros_like(carry_ref)
        h_ref[0] = _proj_normalise(x_ref[0, 0:tm, :], mod_ref, g_ref)

    h_ref[1] = _proj_normalise(x_ref[0, tm:2 * tm, :], mod_ref, g_ref)
    _proj_tile(h_ref[0], slice(0, tm), weights, mu_ref, fb_ref, route_ref, outs,
               shift_ref, carry_ref)
    h_ref[0] = _proj_normalise(x_next_ref[0], mod_ref, g_ref)
    _proj_tile(h_ref[1], slice(tm, 2 * tm), weights, mu_ref, fb_ref, route_ref, outs,
               shift_ref, carry_ref)


def _projection(x, mod3, norm_g, weights, mu, f_bias):
    src = jnp.arange(3 * LANES)[:, None]
    dst = jnp.arange(WIDTH)[None, :]
    route = jnp.zeros((3 * LANES, WIDTH), jnp.bool_)
    for p in range(N_PAIRS):
        for h in range(2):
            for n in range(3):
                route |= ((src == n * LANES + 2 * p + h)
                          & (dst == p * PAIR + HEAD_DIM * (1 - h) + n))
    route = route.astype(BF16)
    batch, seq, _ = x.shape
    tm = PROJ_ROWS
    n_tiles = seq // tm
    grid = (batch, n_tiles // 2)
    const = lambda b, s: (0, 0)
    tile = lambda b, s: (b, s, 0)
    lanes = lambda b, s: (b, 0, s)
    ahead = lambda b, s: (b, jnp.minimum(2 * s + 2, n_tiles - 1), 0)
    return pl.pallas_call(
        _proj_kernel,
        grid=grid,
        in_specs=[pl.BlockSpec((1, 2 * tm, D_MODEL), tile),
                  pl.BlockSpec((1, tm, D_MODEL), ahead),
                  pl.BlockSpec((1, N_MOD, D_MODEL), lambda b, s: (b, 0, 0)),
                  pl.BlockSpec((1, D_MODEL), const),
                  *[pl.BlockSpec(w.shape, const, pipeline_mode=pl.Buffered(1)) for w in weights],
                  pl.BlockSpec((1, N_RKV + N_SMALL), const),
                  pl.BlockSpec((1, LANES), const),
                  pl.BlockSpec(route.shape, const)],
        out_specs=[pl.BlockSpec((1, 2 * tm, N_RKV), tile),
                   pl.BlockSpec((1, 2 * tm, N_SMALL), tile),
                   pl.BlockSpec((1, HEADS, 2 * tm, LANES), lambda b, s: (b, 0, s, 0)),
                   pl.BlockSpec((1, HEADS, 2 * tm, LANES), lambda b, s: (b, 0, s, 0)),
                   pl.BlockSpec((1, WIDTH, 2 * tm), lanes),
                   pl.BlockSpec((1, HEADS, 2 * tm), lanes),
                   pl.BlockSpec((1, 2 * tm, N_GATE), tile)],
        out_shape=[jax.ShapeDtypeStruct((batch, seq, N_RKV), BF16),
                   jax.ShapeDtypeStruct((batch, seq, N_SMALL), F32),
                   jax.ShapeDtypeStruct((batch, HEADS, seq, LANES), BF16),
                   jax.ShapeDtypeStruct((batch, HEADS, seq, LANES), BF16),
                   jax.ShapeDtypeStruct((batch, WIDTH, seq), BF16),
                   jax.ShapeDtypeStruct((batch, HEADS, seq), F32),
                   jax.ShapeDtypeStruct((batch, seq, N_GATE), BF16)],
        scratch_shapes=[pltpu.VMEM((tm + 8, N_RKV + N_SMALL), F32),
                        pltpu.VMEM((1, LANES), F32),
                        pltpu.VMEM((2, tm, D_MODEL), BF16)],
        compiler_params=pltpu.CompilerParams(dimension_semantics=("arbitrary", "arbitrary"),
                                             vmem_limit_bytes=VMEM_LIMIT_BYTES),
        name="proj",
    )(x, x, mod3, norm_g, *weights, mu, f_bias, route)


def _rwkv_prepare(rkv, small, w_lr_ref, w_gate_ref, vec_ref, seg_ref, slot, prep):
    am_ref, rm_ref, vb_ref, bt_ref, kt_ref, bh_ref, kh_ref, dend_ref, g_ref, bv_ref = prep
    c = CHUNK
    r = rkv[:, 0:WIDTH].astype(F32)
    k_raw = rkv[:, WIDTH:2 * WIDTH].astype(F32)
    v = rkv[:, 2 * WIDTH:3 * WIDTH].astype(F32)
    decay_base, iclr_base, kk_scale, k_mix, r_bonus = (vec_ref[i:i + 1, :] for i in range(5))

    lane = lax.broadcasted_iota(jnp.int32, (1, LANES), 1)
    lr_in = jnp.where(lane < DECAY_RANK, jnp.tanh(small[:, 0:LANES]), small[:, 0:LANES])
    lr = _dot(lr_in, w_lr_ref[...])
    w_log = -_softplus(-(decay_base + lr[:, 0:WIDTH])) - 0.5
    log_decay = -jnp.exp(w_log)
    a = _sigmoid(iclr_base + lr[:, WIDTH:2 * WIDTH])
    g_ref[slot] = _dot(_sigmoid(small[:, LANES:2 * LANES]), w_gate_ref[...])
    yield

    seg = seg_ref[...]
    kk = k_raw * kk_scale
    kk = kk * lax.rsqrt(jnp.maximum(_head_sums(kk * kk, seg), 1e-24))
    k = k_raw * (1.0 + (a - 1.0) * k_mix)
    b_vec = kk * a
    yield

    cs = _cumsum_rows(_lower_tri(c, BF16), log_decay, 2)
    cs_end = cs[c - 1:c, :]
    am_ref[slot] = (-kk * jnp.exp(cs - log_decay)).astype(BF16)
    rm_ref[slot] = (r * jnp.exp(cs)).astype(BF16)
    vb_ref[slot] = v.astype(BF16)
    yield
    w_inv = jnp.exp(-cs)
    bt_ref[slot] = jnp.transpose(b_vec * w_inv).astype(BF16)
    yield
    kt_ref[slot] = jnp.transpose(k * w_inv).astype(BF16)
    yield
    w_end = jnp.exp(cs_end - cs)
    bh_ref[slot] = jnp.transpose(b_vec * w_end).astype(BF16)
    yield
    kh_ref[slot] = jnp.transpose(k * w_end).astype(BF16)
    dend_ref[slot] = jnp.broadcast_to(jnp.exp(cs_end), (8, WIDTH))
    yield
    bv_ref[slot] = _head_sums(r * k * r_bonus, seg) * v


def _rwkv_chunk_matrices(slot, mid_slot, prep, mid):
    am_ref, rm_ref, vb_ref, bt_ref, kt_ref = prep[:5]
    pq_ref, lhs_ref = mid
    c = CHUNK
    lane = lax.broadcasted_iota(jnp.int32, (1, LANES), 1)
    row = lax.broadcasted_iota(jnp.int32, (c, c), 0)
    col = lax.broadcasted_iota(jnp.int32, (c, c), 1)
    strict = col < row
    incl = col <= row
    eye = (col == row).astype(F32)

    sls = [slice(p * PAIR, (p + 1) * PAIR) for p in range(N_PAIRS)]
    heads = [(p, h) for p in range(N_PAIRS) for h in range(2)]
    a_m, a_ab, a_ak = {}, {}, {}
    for i, (p, h) in enumerate(heads):
        head = (lane // HEAD_DIM) == h
        a_m[p, h] = jnp.where(head, am_ref[slot, :, sls[p]].astype(F32), 0.0).astype(BF16)
        r_m = jnp.where(head, rm_ref[slot, :, sls[p]].astype(F32), 0.0).astype(BF16)
        rhs = jnp.concatenate([bt_ref[slot, sls[p], :], kt_ref[slot, sls[p], :]], axis=1)
        big = jnp.dot(jnp.concatenate([a_m[p, h], r_m], axis=0), rhs,
                      preferred_element_type=F32)
        a_ab[p, h] = jnp.where(strict, big[0:c, 0:c], 0.0)
        a_ak[p, h] = jnp.where(strict, big[0:c, c:2 * c], 0.0).astype(BF16)
        lhs_ref[mid_slot, i] = jnp.concatenate(
            [r_m, jnp.where(incl, big[c:2 * c, 0:c], 0.0).astype(BF16),
             jnp.where(incl, big[c:2 * c, c:2 * c], 0.0).astype(BF16)], axis=1)
    yield
    t_inv = {hd: eye + a_ab[hd] for hd in heads}
    m_pow = {hd: _dot(a_ab[hd], a_ab[hd]) for hd in heads}
    yield
    for k in range(1, 7):
        skip = (2 ** k) // 16 * 16
        last = k == 6
        for hd in heads:
            rhs = t_inv[hd] if last else jnp.concatenate([m_pow[hd], t_inv[hd]], axis=1)
            upd = _dot(m_pow[hd][skip:c, :], rhs)
            if skip:
                upd = jnp.concatenate([jnp.zeros((skip, upd.shape[1]), F32), upd], axis=0)
            if last:
                t_inv[hd] = t_inv[hd] + upd
            else:
                m_pow[hd] = upd[:, 0:c]
                t_inv[hd] = t_inv[hd] + upd[:, c:2 * c]
        yield
    ak_v = {(p, h): _dot(a_ak[p, h], vb_ref[slot, :, sls[p]]) for (p, h) in heads}
    yield
    for i, hd in enumerate(heads):
        pq_ref[mid_slot, i] = _dot(t_inv[hd],
                               jnp.concatenate([a_m[hd].astype(F32), ak_v[hd]], axis=1))


def _rwkv_chunk_state(slot, mid_slot, prep, mid, vec_ref, seg_ref, z_ref, out_ref, rows):
    vb_ref, bh_ref, kh_ref, dend_ref, g_ref, bv_ref = (prep[i] for i in (2, 5, 6, 7, 8, 9))
    pq_ref, lhs_ref = mid
    c = CHUNK
    ln_w, ln_b = vec_ref[5:6, :], vec_ref[6:7, :]
    lane = lax.broadcasted_iota(jnp.int32, (1, LANES), 1)
    row = lax.broadcasted_iota(jnp.int32, (c, c), 0)
    col = lax.broadcasted_iota(jnp.int32, (c, c), 1)
    eye = (col == row).astype(F32)
    same_head = (row // HEAD_DIM) == (col // HEAD_DIM)
    pairs = range(N_PAIRS)
    sls = [slice(p * PAIR, (p + 1) * PAIR) for p in pairs]
    heads = [(p, h) for p in pairs for h in range(2)]
    v_b = [vb_ref[slot, :, sl] for sl in sls]
    z_f = [z_ref[p] for p in pairs]
    z_b = [z.astype(BF16) for z in z_f]

    u = {(p, h): (_dot(pq_ref[mid_slot, i, :, 0:PAIR], z_b[p])
                  + pq_ref[mid_slot, i, :, PAIR:2 * PAIR])
         for i, (p, h) in enumerate(heads)}
    yield
    y_hd = {(p, h): jnp.dot(
        lhs_ref[mid_slot, i], jnp.concatenate([z_b[p], u[p, h].astype(BF16), v_b[p]], axis=0),
        preferred_element_type=F32) for i, (p, h) in enumerate(heads)}
    first = lane < HEAD_DIM
    y_pairs = []
    for p in pairs:
        u_p = jnp.where(first, u[p, 0], u[p, 1])
        y_pairs.append(jnp.where(first, y_hd[p, 0], y_hd[p, 1]))
        d_col = jnp.sum(eye * dend_ref[slot, 0:1, sls[p]], axis=1, keepdims=True)
        z_new = d_col * z_f[p] + jnp.dot(
            jnp.concatenate([bh_ref[slot, sls[p], :], kh_ref[slot, sls[p], :]], axis=1),
            jnp.concatenate([u_p.astype(BF16), v_b[p]], axis=0), preferred_element_type=F32)
        z_ref[p] = jnp.where(same_head, z_new, 0.0)
    yield

    seg = seg_ref[...]
    y = jnp.concatenate(y_pairs, axis=1)
    mean = _head_sums(y, seg) * (1.0 / HEAD_DIM)
    yc = y - mean
    var = _head_sums(yc * yc, seg) * (1.0 / HEAD_DIM)
    y = yc * lax.rsqrt(var + GN_EPS) * ln_w + ln_b
    out_ref[0, rows, :] = ((y + bv_ref[slot]) * g_ref[slot]).astype(BF16)


def _interleave(*stages, late=()):
    live = list(stages)
    late = list(late)
    while live:
        for gen in list(live):
            try:
                next(gen)
            except StopIteration:
                live.remove(gen)
                if late and gen is stages[0]:
                    live.extend(late)
                    late = []


def _in_turn(*stages):
    for gen in stages:
        yield from gen


def _rwkv_kernel(rkv_ref, small_ref, rkv_next_ref, small_next_ref,
                 w_lr_ref, w_gate_ref, vec_ref, seg_ref, out_ref, z_ref, pq_ref, lhs_ref, *prep):
    c = CHUNK
    params = (w_lr_ref, w_gate_ref, vec_ref, seg_ref)
    mid = (pq_ref, lhs_ref)
    rows = [slice(i * c, (i + 1) * c) for i in range(RWKV_STEP_CHUNKS)]

    def prepare(r_ref, s_ref, i, slot):
        return _rwkv_prepare(r_ref[0, rows[i], :], s_ref[0, rows[i], :], *params, slot, prep)

    def matrices(i):
        return _rwkv_chunk_matrices(i, i % RWKV_MID_SLOTS, prep, mid)

    def state(i):
        return _rwkv_chunk_state(i, i % RWKV_MID_SLOTS, prep, mid, vec_ref, seg_ref, z_ref,
                                 out_ref, rows[i])

    @pl.when(pl.program_id(1) == 0)
    def _():
        z_ref[...] = jnp.zeros_like(z_ref)
        _interleave(prepare(rkv_ref, small_ref, 0, 0), prepare(rkv_ref, small_ref, 1, 1))

    n_pairs = RWKV_STEP_CHUNKS // 2
    for j in range(n_pairs):
        a, b = 2 * j, 2 * j + 1
        streams = [matrices(a), matrices(b)]
        if j > 0:
            streams.insert(0, _in_turn(state(a - 2), state(b - 2)))
        if j + 1 < n_pairs:
            ahead = [prepare(rkv_ref, small_ref, a + 2, a + 2),
                     prepare(rkv_ref, small_ref, b + 2, b + 2)]
        else:
            ahead = [prepare(rkv_next_ref, small_next_ref, 0, 0),
                     prepare(rkv_next_ref, small_next_ref, 1, 1)]
        if ahead and j == 1 and j + 1 == n_pairs:
            _interleave(*streams, late=ahead)
        else:
            _interleave(*streams, *ahead)
    _interleave(_in_turn(state(RWKV_STEP_CHUNKS - 2), state(RWKV_STEP_CHUNKS - 1)))


def _rwkv_mix(rkv, small, w_lr, w_gate, vecs, seg):
    batch, seq, _ = rkv.shape
    c = CHUNK
    n = RWKV_STEP_CHUNKS
    n_pairs_total = seq // (2 * c)
    grid = (batch, seq // (n * c))
    const = lambda b, s: (0, 0)
    tile = lambda b, s: (b, s, 0)
    ahead = lambda b, s: (b, jnp.minimum(n // 2 * (s + 1), n_pairs_total - 1), 0)
    slots = lambda shape, dtype: pltpu.VMEM((n,) + shape, dtype)
    mid_slots = lambda shape, dtype: pltpu.VMEM((RWKV_MID_SLOTS,) + shape, dtype)
    return pl.pallas_call(
        _rwkv_kernel,
        grid=grid,
        in_specs=[pl.BlockSpec((1, n * c, N_RKV), tile),
                  pl.BlockSpec((1, n * c, N_SMALL), tile),
                  pl.BlockSpec((1, 2 * c, N_RKV), ahead),
                  pl.BlockSpec((1, 2 * c, N_SMALL), ahead),
                  pl.BlockSpec(w_lr.shape, const),
                  pl.BlockSpec(w_gate.shape, const),
                  pl.BlockSpec(vecs.shape, const),
                  pl.BlockSpec(seg.shape, const)],
        out_specs=pl.BlockSpec((1, n * c, WIDTH), tile),
        out_shape=jax.ShapeDtypeStruct((batch, seq, WIDTH), BF16),
        scratch_shapes=[pltpu.VMEM((N_PAIRS, PAIR, PAIR), F32),
                        mid_slots((HEADS, c, 2 * PAIR), F32),
                        mid_slots((HEADS, c, 3 * PAIR), BF16),
                        slots((c, WIDTH), BF16),
                        slots((c, WIDTH), BF16),
                        slots((c, WIDTH), BF16),
                        slots((WIDTH, c), BF16),
                        slots((WIDTH, c), BF16),
                        slots((WIDTH, c), BF16),
                        slots((WIDTH, c), BF16),
                        slots((8, WIDTH), F32),
                        slots((c, WIDTH), F32),
                        slots((c, WIDTH), F32)],
        compiler_params=pltpu.CompilerParams(dimension_semantics=("arbitrary", "arbitrary"),
                                             vmem_limit_bytes=VMEM_LIMIT_BYTES),
        name="rwkv",
    )(rkv, small, rkv, small, w_lr, w_gate, vecs, seg)


def _fox_kernel(qx_ref, kx_ref, vt_ref, cumt_ref, mask_ref, o_ref, vx_ref, t_ref, p_ref):
    heads = range(FOX_STEP_HEADS)
    head0 = FOX_STEP_HEADS * pl.program_id(1)
    blk = FOX_BLOCK
    seq = kx_ref.shape[2]
    n_blk = seq // blk
    ones_row = lax.broadcasted_iota(jnp.int32, (FOX_VROWS - HEAD_DIM, seq), 0) == 0
    for h in heads:
        vx_ref[h, 0:HEAD_DIM, :] = vt_ref[0, h * HEAD_DIM:(h + 1) * HEAD_DIM, :]
        vx_ref[h, HEAD_DIM:FOX_VROWS, :] = ones_row.astype(BF16)

    def rows_of(b):
        return pl.ds(pl.multiple_of(b * blk, blk), blk)

    def following(qi, kj):
        wrap = kj == qi
        return jnp.where(wrap, qi + 1, qi), jnp.where(wrap, 0, kj + 1)

    def scores(pair, slot, out):
        qi, kj = pair
        qi = jnp.minimum(qi, n_blk - 1)
        causal = mask_ref[(kj == qi).astype(jnp.int32)]
        mx = []
        for h in heads:
            t = _dot_nt(kx_ref[0, h, rows_of(kj), :], qx_ref[0, h, rows_of(qi), :]) + causal
            t_ref[slot, h] = t
            c_q = cumt_ref[0, pl.ds(head0 + h, 1), rows_of(qi)] * LOG2E
            mx.append(jnp.max(t, axis=0, keepdims=True) + c_q)
            yield
        out["mx"] = tuple(mx)

    def softmax(pair, slot, m, mx, out):
        qi, kj = pair
        sub = FOX_SUB
        q_rows = rows_of(jnp.minimum(qi, n_blk - 1))
        m_out, alpha_out = [], []
        for h in heads:
            m_old = jnp.where(kj == 0, -jnp.inf, m[h])
            c_q = cumt_ref[0, pl.ds(head0 + h, 1), q_rows] * LOG2E
            m_new = jnp.maximum(m_old, mx[h])
            shift = c_q - m_new
            for i in range(blk // sub):
                rows = slice(i * sub, (i + 1) * sub)
                p_ref[slot, h, rows, :] = jnp.exp2(t_ref[slot, h, rows, :] + shift).astype(BF16)
                if i % 2 == 1:
                    yield
            m_out.append(m_new)
            alpha_out.append(jnp.exp2(m_old - m_new))
        out["m"], out["alpha"] = tuple(m_out), tuple(alpha_out)

    def accumulate(pair, slot, alpha, acc, out):
        qi, kj = pair
        new = []
        for h in heads:
            new.append(alpha[h] * acc[h] + jnp.dot(vx_ref[h, :, rows_of(kj)], p_ref[slot, h],
                                                   preferred_element_type=F32))
            yield
        o_t = jnp.concatenate([a[0:HEAD_DIM] / a[HEAD_DIM:HEAD_DIM + 1] for a in new], axis=0)
        o_ref[0, :, rows_of(qi)] = o_t.astype(BF16)
        out["acc"] = tuple(new)

    def step(pair_s, pair_p, pair_a, slot_s, m, mx, alpha, acc):
        out = {}
        _interleave(softmax(pair_p, 1 - slot_s, m, mx, out),
                    scores(pair_s, slot_s, out),
                    accumulate(pair_a, slot_s, alpha, acc, out))
        return out

    def two_pairs(_, carry):
        pair0, pair1, m, mx1, alpha0, acc = carry
        pair2 = following(*pair1)
        a = step(pair2, pair1, pair0, 0, m, mx1, alpha0, acc)
        pair3 = following(*pair2)
        b = step(pair3, pair2, pair1, 1, a["m"], a["mx"], a["alpha"], a["acc"])
        return pair2, pair3, b["m"], b["mx"], b["alpha"], b["acc"]

    n_pairs = n_blk * (n_blk + 1) // 2
    assert n_pairs % 2 == 0, "two pairs per trip"
    zero = jnp.int32(0)
    pair0 = (zero, zero)
    pair1 = following(*pair0)
    first, second = {}, {}
    _interleave(scores(pair0, 0, first))
    _interleave(scores(pair1, 1, second))
    m = tuple(jnp.full((1, blk), -jnp.inf, F32) for _ in heads)
    _interleave(softmax(pair0, 0, m, first["mx"], first))
    acc = tuple(jnp.zeros((FOX_VROWS, blk), F32) for _ in heads)
    lax.fori_loop(0, n_pairs // 2, two_pairs,
                  (pair0, pair1, first["m"], second["mx"], first["alpha"], acc))


def _forgetting_attention(qx, kx, v_t, cum_t):
    batch, _, seq, _ = qx.shape
    blk = FOX_BLOCK
    key = lax.broadcasted_iota(jnp.int32, (blk, blk), 0)
    qry = lax.broadcasted_iota(jnp.int32, (blk, blk), 1)
    mask = jnp.stack([jnp.zeros((blk, blk), F32), jnp.where(key <= qry, 0.0, FOX_MASKED)])
    nh = FOX_STEP_HEADS
    width = nh * HEAD_DIM
    whole = lambda b, g: (b, 0, 0)
    n_groups = HEADS // nh
    return pl.pallas_call(
        _fox_kernel,
        grid=(batch, n_groups),
        in_specs=[pl.BlockSpec((1, nh, seq, LANES), lambda b, g: (b, g, 0, 0)),
                  pl.BlockSpec((1, nh, seq, LANES), lambda b, g: (b, g, 0, 0)),
                  pl.BlockSpec((1, width, seq), lambda b, g: (b, g, 0)),
                  pl.BlockSpec((1, HEADS, seq), whole),
                  pl.BlockSpec((2, blk, blk), lambda b, g: (0, 0, 0))],
        out_specs=pl.BlockSpec((1, width, seq), lambda b, g: (b, g, 0)),
        out_shape=jax.ShapeDtypeStruct((batch, WIDTH, seq), BF16),
        scratch_shapes=[pltpu.VMEM((nh, FOX_VROWS, seq), BF16),
                        pltpu.VMEM((2, nh, blk, blk), F32),
                        pltpu.VMEM((2, nh, blk, blk), BF16)],
        compiler_params=pltpu.CompilerParams(
            dimension_semantics=("arbitrary", "arbitrary"),
            vmem_limit_bytes=VMEM_LIMIT_BYTES),
        name="fox",
    )(qx, kx, v_t, cum_t, mask)


def _tail_kernel(x_ref, ya_ref, ybt_ref, gate_ref, mod_ref, g2_ref, gf_ref,
                 woa_ref, wob_ref, wout_ref, w1_ref, w2_ref, o_ref):
    x = x_ref[0]
    gate1 = mod_ref[0, 2:3, :]
    shift2 = mod_ref[0, 3:4, :]
    scale2 = mod_ref[0, 4:5, :]
    gate2 = mod_ref[0, 5:6, :]

    merged = (gate_ref[0, :, 0:D_MODEL].astype(F32)
              * jnp.dot(ya_ref[0], woa_ref[...], preferred_element_type=F32)
              + gate_ref[0, :, D_MODEL:N_GATE].astype(F32)
              * lax.dot_general(ybt_ref[0], wob_ref[...], (((0,), (0,)), ((), ())),
                                preferred_element_type=F32))
    x = x + gate1 * _dot(merged, wout_ref[...])

    inv = lax.rsqrt(jnp.mean(x * x, axis=-1, keepdims=True) + NORM_EPS)
    h2 = ((x * inv) * g2_ref[...] * (1.0 + scale2) + shift2).astype(BF16)
    ff = jnp.zeros_like(x)
    for j in range(D_FF // FF_CHUNK):
        cols = slice(j * FF_CHUNK, (j + 1) * FF_CHUNK)
        hid = jnp.maximum(jnp.dot(h2, w1_ref[:, cols], preferred_element_type=F32), 0.0)
        ff = ff + _dot(hid * hid, w2_ref[cols, :])
    x = x + gate2 * ff

    inv = lax.rsqrt(jnp.mean(x * x, axis=-1, keepdims=True) + NORM_EPS)
    o_ref[0] = (x * inv) * gf_ref[...]


def _tail(x, y_a, y_b, gates, mod3, norm2_g, final_g, w_oa, w_ob, w_out, w_ff1, w_ff2):
    batch, seq, _ = x.shape
    tm = TAIL_ROWS
    grid = (batch, seq // tm)
    const = lambda b, s: (0, 0)
    tile = lambda b, s: (b, s, 0)
    resident = lambda a: pl.BlockSpec(a.shape, const, pipeline_mode=pl.Buffered(1))
    return pl.pallas_call(
        _tail_kernel,
        grid=grid,
        in_specs=[pl.BlockSpec((1, tm, D_MODEL), tile),
                  pl.BlockSpec((1, tm, WIDTH), tile),
                  pl.BlockSpec((1, WIDTH, tm), lambda b, s: (b, 0, s)),
                  pl.BlockSpec((1, tm, N_GATE), tile),
                  pl.BlockSpec((1, N_MOD, D_MODEL), lambda b, s: (b, 0, 0)),
                  pl.BlockSpec((1, D_MODEL), const),
                  pl.BlockSpec((1, D_MODEL), const),
                  resident(w_oa), resident(w_ob), resident(w_out),
                  resident(w_ff1), resident(w_ff2)],
        out_specs=pl.BlockSpec((1, tm, D_MODEL), tile),
        out_shape=jax.ShapeDtypeStruct((batch, seq, D_MODEL), F32),
        compiler_params=pltpu.CompilerParams(dimension_semantics=("arbitrary", "arbitrary"),
                                             vmem_limit_bytes=VMEM_LIMIT_BYTES),
        name="tail",
    )(x, y_a, y_b, gates, mod3, norm2_g, final_g, w_oa, w_ob, w_out, w_ff1, w_ff2)


def _drop_depth_axis(t):
    return t.reshape(t.shape[1:])


def _reorder_rwkv_cols(t):
    o = 0
    r = t[..., o:o + WIDTH]; o += WIDTH
    wd = t[..., o:o + DECAY_RANK]; o += DECAY_RANK
    k = t[..., o:o + WIDTH]; o += WIDTH
    v = t[..., o:o + WIDTH]; o += WIDTH
    ad = t[..., o:o + ICLR_RANK]; o += ICLR_RANK
    gd = t[..., o:o + GATE_RANK]
    return jnp.concatenate([r, k, v, wd, ad, gd], axis=-1)


def kernel(x, c, w_ada, b_ada, norm1_g, w_in, mu_shift, w_decay_up, decay_base, w_iclr_up, iclr_base, w_gate_up, kk_scale, k_iclr_mix, r_bonus, lnx_w, lnx_b, fox_f_bias, w_o_rwkv, w_o_fox, w_out, norm2_g, w_ff1, w_ff2, final_g):
    assert w_ada.shape[0] == 1, "the tail kernel fuses the final norm: single layer only"
    layer = _drop_depth_axis
    n_rwkv = N_RKV + N_SMALL
    seg_id = jnp.arange(WIDTH // 2) // HEAD_DIM
    seg = (seg_id[:, None] == seg_id[None, :]).astype(BF16)

    mod3 = _modulation(c, layer(w_ada), layer(b_ada)).reshape(-1, N_MOD, D_MODEL)

    w = layer(w_in).astype(BF16)
    o_wd, o_k, o_v, o_ad = WIDTH, WIDTH + DECAY_RANK, 2 * WIDTH + DECAY_RANK, 3 * WIDTH + DECAY_RANK
    o_ff = n_rwkv + N_FOX
    weights = [w[:, 0:o_wd], w[:, o_k:o_v], w[:, o_v:o_ad],
               jnp.concatenate([w[:, o_wd:o_k], w[:, o_ad:n_rwkv]], axis=1),
               w[:, n_rwkv:o_ff],
               jnp.pad(w[:, o_ff:o_ff + HEADS], ((0, 0), (0, LANES - HEADS))),
               w[:, o_ff + HEADS:]]
    mu = _reorder_rwkv_cols(layer(mu_shift)).reshape(1, n_rwkv)
    f_bias = jnp.pad(layer(fox_f_bias), (0, LANES - HEADS)).reshape(1, LANES)
    rkv, small, qx, kx, v_t, cum_t, gates = _projection(
        x, mod3, layer(norm1_g).reshape(1, D_MODEL), weights, mu, f_bias)

    zeros = jnp.zeros((DECAY_RANK, WIDTH), F32)
    w_lr = jnp.concatenate(
        [jnp.concatenate([layer(w_decay_up), zeros], axis=1),
         jnp.concatenate([zeros, layer(w_iclr_up)], axis=1)], axis=0).astype(BF16)
    vecs = jnp.stack([layer(decay_base), layer(iclr_base), layer(kk_scale), layer(k_iclr_mix),
                      layer(r_bonus).reshape(WIDTH), layer(lnx_w), layer(lnx_b),
                      jnp.zeros((WIDTH,), F32)], axis=0)
    y_a = _rwkv_mix(rkv, small, w_lr, layer(w_gate_up).astype(BF16), vecs, seg)

    y_b = _forgetting_attention(qx, kx, v_t, cum_t)

    return _tail(x, y_a, y_b, gates, mod3, layer(norm2_g).reshape(1, D_MODEL),
                 final_g.reshape(1, D_MODEL),
                 layer(w_o_rwkv).astype(BF16), layer(w_o_fox).astype(BF16),
                 layer(w_out).astype(BF16), layer(w_ff1).astype(BF16), layer(w_ff2).astype(BF16))
```

```python
import jax
import jax.numpy as jnp
from jax import lax
from jax.experimental import pallas as pl
from jax.experimental.pallas import tpu as pltpu

F32 = jnp.float32
BF16 = jnp.bfloat16

D_MODEL = 1024
HEAD_DIM = 64
HEADS = 8
WIDTH = HEADS * HEAD_DIM
DECAY_RANK = 64
ICLR_RANK = 64
GATE_RANK = 128
D_FF = 4 * D_MODEL
N_MOD = 6
NORM_EPS = 1e-6
GN_EPS = 64e-5

LANES = 128
PAIR = 2 * HEAD_DIM
N_PAIRS = HEADS // 2

LOG2E = 1.4426950408889634

N_SMALL = DECAY_RANK + ICLR_RANK + GATE_RANK
N_RKV = 3 * WIDTH
N_FOX = 3 * WIDTH
N_GATE = 2 * D_MODEL

PROJ_ROWS = 256
CHUNK = 128
RWKV_STEP_CHUNKS = 4
RWKV_MID_SLOTS = 4
FOX_BLOCK = 512
FOX_SUB = 32
FOX_VROWS = HEAD_DIM + 16
FOX_STEP_HEADS = 4
FOX_MASKED = -1e30
TAIL_ROWS = 512
FF_CHUNK = 1024

VMEM_LIMIT_BYTES = 56 * 1024 * 1024


def _dot(a, b):
    return jnp.dot(a.astype(BF16), b.astype(BF16), preferred_element_type=F32)


def _dot_nt(a, b):
    return lax.dot_general(a.astype(BF16), b.astype(BF16), (((1,), (1,)), ((), ())),
                           preferred_element_type=F32)


def _softplus(z):
    return jnp.maximum(z, 0.0) + jnp.log(1.0 + jnp.exp(-jnp.abs(z)))


def _sigmoid(z):
    return 1.0 / (1.0 + jnp.exp(-z))


def _bf16_parts(x, n):
    parts = []
    for _ in range(n):
        p = x.astype(BF16)
        parts.append(p)
        x = x - p.astype(F32)
    return parts


def _cumsum_rows(tri, x, n_parts):
    return sum(jnp.dot(tri, p, preferred_element_type=F32) for p in _bf16_parts(x, n_parts))


def _head_sums(x, seg):
    half = seg.shape[0]
    return jnp.concatenate([_dot(x[:, 0:half], seg), _dot(x[:, half:2 * half], seg)], axis=1)


def _lower_tri(n, dtype):
    row = lax.broadcasted_iota(jnp.int32, (n, n), 0)
    col = lax.broadcasted_iota(jnp.int32, (n, n), 1)
    return (col <= row).astype(dtype)


def _mod_kernel(c_ref, w_ref, b_ref, o_ref):
    c = c_ref[...]
    c_act = c * _sigmoid(c)
    o_ref[...] = _dot(c_act, w_ref[...]) + b_ref[...]


def _modulation(c, w_ada, b_ada):
    batch = c.shape[0]
    n = w_ada.shape[1]
    tn = 1536
    return pl.pallas_call(
        _mod_kernel,
        grid=(n // tn,),
        in_specs=[pl.BlockSpec((batch, D_MODEL), lambda j: (0, 0)),
                  pl.BlockSpec((D_MODEL, tn), lambda j: (0, j)),
                  pl.BlockSpec((1, tn), lambda j: (0, j))],
        out_specs=pl.BlockSpec((batch, tn), lambda j: (0, j)),
        out_shape=jax.ShapeDtypeStruct((batch, n), F32),
        compiler_params=pltpu.CompilerParams(dimension_semantics=("arbitrary",),
                                             vmem_limit_bytes=VMEM_LIMIT_BYTES),
        name="mod",
    )(c, w_ada, b_ada.reshape(1, n))


def _proj_normalise(x, mod_ref, g_ref):
    shift1 = mod_ref[0, 0:1, :]
    scale1 = mod_ref[0, 1:2, :]
    inv = lax.rsqrt(jnp.mean(x * x, axis=-1, keepdims=True) + NORM_EPS)
    return ((x * inv) * (g_ref[...] * (1.0 + scale1)) + shift1).astype(BF16)


def _proj_tile(h, rows, weights, mu_ref, fb_ref, route_ref, outs, shift_ref, carry_ref):
    wr_ref, wk_ref, wv_ref, ws_ref, wfox_ref, wff_ref, wgate_ref = weights
    rkv_ref, small_ref, qx_ref, kx_ref, vt_ref, cumt_ref, gate_ref = outs
    n = h.shape[0]

    ff = jnp.dot(h, wff_ref[...], preferred_element_type=F32)
    logf = -_softplus(-(ff + fb_ref[...]))
    cum = _cumsum_rows(_lower_tri(n, BF16), logf, 3) + carry_ref[...]
    cumt_ref[0, :, rows] = jnp.transpose(cum)[0:HEADS, :]
    carry_ref[...] = cum[n - 1:n, :]

    n_tok = N_RKV + N_SMALL
    p = jnp.concatenate([jnp.dot(h, w[...], preferred_element_type=F32)
                         for w in (wr_ref, wk_ref, wv_ref, ws_ref)], axis=1)
    shift_ref[8:8 + n, :] = p
    prev = shift_ref[7:7 + n, :]
    shift_ref[7:8, :] = p[n - 1:n, :]
    mixed = p + mu_ref[...] * (prev - p)
    rkv_ref[0, rows, :] = mixed[:, 0:N_RKV].astype(BF16)
    small_ref[0, rows, :] = mixed[:, N_RKV:n_tok]

    pf = jnp.dot(h, wfox_ref[...], preferred_element_type=F32)
    vt_ref[0, :, rows] = jnp.transpose(pf[:, 2 * WIDTH:N_FOX]).astype(BF16)

    parts = jnp.concatenate(_bf16_parts(cum * (-LOG2E), 3), axis=1)
    feats = jnp.dot(parts, route_ref[...], preferred_element_type=F32).astype(BF16)
    lane_b = lax.broadcasted_iota(jnp.int32, (n, LANES), 1)
    for pair in range(N_PAIRS):
        group = slice(pair * PAIR, (pair + 1) * PAIR)
        q_pair = (pf[:, group] * (LOG2E * HEAD_DIM ** -0.5)).astype(BF16)
        k_pair = pf[:, WIDTH + pair * PAIR:WIDTH + (pair + 1) * PAIR].astype(BF16)
        for hh in range(2):
            own = (lane_b // HEAD_DIM) == hh
            bias0 = HEAD_DIM * (1 - hh)
            is_bias = (lane_b >= bias0) & (lane_b < bias0 + 3)
            kx_ref[0, 2 * pair + hh, rows, :] = jnp.where(own, k_pair, feats[:, group])
            qx_ref[0, 2 * pair + hh, rows, :] = jnp.where(
                own, q_pair, jnp.where(is_bias, 1.0, 0.0).astype(BF16))

    pg = jnp.dot(h, wgate_ref[...], preferred_element_type=F32)
    gate_ref[0, rows, :] = _sigmoid(pg).astype(BF16)


def _proj_kernel(x_ref, x_next_ref, mod_ref, g_ref, wr_ref, wk_ref, wv_ref, ws_ref, wfox_ref,
                 wff_ref, wgate_ref, mu_ref, fb_ref, route_ref,
                 rkv_ref, small_ref, qx_ref, kx_ref, vt_ref, cumt_ref, gate_ref,
                 shift_ref, carry_ref, h_ref):
    tm = PROJ_ROWS
    weights = (wr_ref, wk_ref, wv_ref, ws_ref, wfox_ref, wff_ref, wgate_ref)
    outs = (rkv_ref, small_ref, qx_ref, kx_ref, vt_ref, cumt_ref, gate_ref)

    @pl.when(pl.program_id(1) == 0)
    def _():
        shift_ref[0:8, :] = jnp.zeros((8, N_RKV + N_SMALL), F32)
        carry_ref[...] = jnp.zeros_like(carry_ref)
        h_ref[0] = _proj_normalise(x_ref[0, 0:tm, :], mod_ref, g_ref)

    h_ref[1] = _proj_normalise(x_ref[0, tm:2 * tm, :], mod_ref, g_ref)
    _proj_tile(h_ref[0], slice(0, tm), weights, mu_ref, fb_ref, route_ref, outs,
               shift_ref, carry_ref)
    h_ref[0] = _proj_normalise(x_next_ref[0], mod_ref, g_ref)
    _proj_tile(h_ref[1], slice(tm, 2 * tm), weights, mu_ref, fb_ref, route_ref, outs,
               shift_ref, carry_ref)


def _projection(x, mod3, norm_g, weights, mu, f_bias):
    src = jnp.arange(3 * LANES)[:, None]
    dst = jnp.arange(WIDTH)[None, :]
    route = jnp.zeros((3 * LANES, WIDTH), jnp.bool_)
    for p in range(N_PAIRS):
        for h in range(2):
            for n in range(3):
                route |= ((src == n * LANES + 2 * p + h)
                          & (dst == p * PAIR + HEAD_DIM * (1 - h) + n))
    route = route.astype(BF16)
    batch, seq, _ = x.shape
    tm = PROJ_ROWS
    n_tiles = seq // tm
    grid = (batch, n_tiles // 2)
    const = lambda b, s: (0, 0)
    tile = lambda b, s: (b, s, 0)
    lanes = lambda b, s: (b, 0, s)
    ahead = lambda b, s: (b, jnp.minimum(2 * s + 2, n_tiles - 1), 0)
    return pl.pallas_call(
        _proj_kernel,
        grid=grid,
        in_specs=[pl.BlockSpec((1, 2 * tm, D_MODEL), tile),
                  pl.BlockSpec((1, tm, D_MODEL), ahead),
                  pl.BlockSpec((1, N_MOD, D_MODEL), lambda b, s: (b, 0, 0)),
                  pl.BlockSpec((1, D_MODEL), const),
                  *[pl.BlockSpec(w.shape, const, pipeline_mode=pl.Buffered(1)) for w in weights],
                  pl.BlockSpec((1, N_RKV + N_SMALL), const),
                  pl.BlockSpec((1, LANES), const),
                  pl.BlockSpec(route.shape, const)],
        out_specs=[pl.BlockSpec((1, 2 * tm, N_RKV), tile),
                   pl.BlockSpec((1, 2 * tm, N_SMALL), tile),
                   pl.BlockSpec((1, HEADS, 2 * tm, LANES), lambda b, s: (b, 0, s, 0)),
                   pl.BlockSpec((1, HEADS, 2 * tm, LANES), lambda b, s: (b, 0, s, 0)),
                   pl.BlockSpec((1, WIDTH, 2 * tm), lanes),
                   pl.BlockSpec((1, HEADS, 2 * tm), lanes),
                   pl.BlockSpec((1, 2 * tm, N_GATE), tile)],
        out_shape=[jax.ShapeDtypeStruct((batch, seq, N_RKV), BF16),
                   jax.ShapeDtypeStruct((batch, seq, N_SMALL), F32),
                   jax.ShapeDtypeStruct((batch, HEADS, seq, LANES), BF16),
                   jax.ShapeDtypeStruct((batch, HEADS, seq, LANES), BF16),
                   jax.ShapeDtypeStruct((batch, WIDTH, seq), BF16),
                   jax.ShapeDtypeStruct((batch, HEADS, seq), F32),
                   jax.ShapeDtypeStruct((batch, seq, N_GATE), BF16)],
        scratch_shapes=[pltpu.VMEM((tm + 8, N_RKV + N_SMALL), F32),
                        pltpu.VMEM((1, LANES), F32),
                        pltpu.VMEM((2, tm, D_MODEL), BF16)],
        compiler_params=pltpu.CompilerParams(dimension_semantics=("arbitrary", "arbitrary"),
                                             vmem_limit_bytes=VMEM_LIMIT_BYTES),
        name="proj",
    )(x, x, mod3, norm_g, *weights, mu, f_bias, route)


def _rwkv_prepare(rkv, small, w_lr_ref, w_gate_ref, vec_ref, seg_ref, slot, prep):
    am_ref, rm_ref, vb_ref, bt_ref, kt_ref, bh_ref, kh_ref, dend_ref, g_ref, bv_ref = prep
    c = CHUNK
    r = rkv[:, 0:WIDTH].astype(F32)
    k_raw = rkv[:, WIDTH:2 * WIDTH].astype(F32)
    v = rkv[:, 2 * WIDTH:3 * WIDTH].astype(F32)
    decay_base, iclr_base, kk_scale, k_mix, r_bonus = (vec_ref[i:i + 1, :] for i in range(5))

    lane = lax.broadcasted_iota(jnp.int32, (1, LANES), 1)
    lr_in = jnp.where(lane < DECAY_RANK, jnp.tanh(small[:, 0:LANES]), small[:, 0:LANES])
    lr = _dot(lr_in, w_lr_ref[...])
    w_log = -_softplus(-(decay_base + lr[:, 0:WIDTH])) - 0.5
    log_decay = -jnp.exp(w_log)
    a = _sigmoid(iclr_base + lr[:, WIDTH:2 * WIDTH])
    g_ref[slot] = _dot(_sigmoid(small[:, LANES:2 * LANES]), w_gate_ref[...])
    yield

    seg = seg_ref[...]
    kk = k_raw * kk_scale
    kk = kk * lax.rsqrt(jnp.maximum(_head_sums(kk * kk, seg), 1e-24))
    k = k_raw * (1.0 + (a - 1.0) * k_mix)
    b_vec = kk * a
    yield

    cs = _cumsum_rows(_lower_tri(c, BF16), log_decay, 2)
    cs_end = cs[c - 1:c, :]
    am_ref[slot] = (-kk * jnp.exp(cs - log_decay)).astype(BF16)
    rm_ref[slot] = (r * jnp.exp(cs)).astype(BF16)
    vb_ref[slot] = v.astype(BF16)
    yield
    w_inv = jnp.exp(-cs)
    bt_ref[slot] = jnp.transpose(b_vec * w_inv).astype(BF16)
    yield
    kt_ref[slot] = jnp.transpose(k * w_inv).astype(BF16)
    yield
    w_end = jnp.exp(cs_end - cs)
    bh_ref[slot] = jnp.transpose(b_vec * w_end).astype(BF16)
    yield
    kh_ref[slot] = jnp.transpose(k * w_end).astype(BF16)
    dend_ref[slot] = jnp.broadcast_to(jnp.exp(cs_end), (8, WIDTH))
    yield
    bv_ref[slot] = _head_sums(r * k * r_bonus, seg) * v


def _rwkv_chunk_matrices(slot, mid_slot, prep, mid):
    am_ref, rm_ref, vb_ref, bt_ref, kt_ref = prep[:5]
    pq_ref, lhs_ref = mid
    c = CHUNK
    lane = lax.broadcasted_iota(jnp.int32, (1, LANES), 1)
    row = lax.broadcasted_iota(jnp.int32, (c, c), 0)
    col = lax.broadcasted_iota(jnp.int32, (c, c), 1)
    strict = col < row
    incl = col <= row
    eye = (col == row).astype(F32)

    sls = [slice(p * PAIR, (p + 1) * PAIR) for p in range(N_PAIRS)]
    heads = [(p, h) for p in range(N_PAIRS) for h in range(2)]
    a_m, a_ab, a_ak = {}, {}, {}
    for i, (p, h) in enumerate(heads):
        head = (lane // HEAD_DIM) == h
        a_m[p, h] = jnp.where(head, am_ref[slot, :, sls[p]].astype(F32), 0.0).astype(BF16)
        r_m = jnp.where(head, rm_ref[slot, :, sls[p]].astype(F32), 0.0).astype(BF16)
        rhs = jnp.concatenate([bt_ref[slot, sls[p], :], kt_ref[slot, sls[p], :]], axis=1)
        big = jnp.dot(jnp.concatenate([a_m[p, h], r_m], axis=0), rhs,
                      preferred_element_type=F32)
        a_ab[p, h] = jnp.where(strict, big[0:c, 0:c], 0.0)
        a_ak[p, h] = jnp.where(strict, big[0:c, c:2 * c], 0.0).astype(BF16)
        lhs_ref[mid_slot, i] = jnp.concatenate(
            [r_m, jnp.where(incl, big[c:2 * c, 0:c], 0.0).astype(BF16),
             jnp.where(incl, big[c:2 * c, c:2 * c], 0.0).astype(BF16)], axis=1)
    yield
    t_inv = {hd: eye + a_ab[hd] for hd in heads}
    m_pow = {hd: _dot(a_ab[hd], a_ab[hd]) for hd in heads}
    yield
    for k in range(1, 7):
        skip = (2 ** k) // 16 * 16
        last = k == 6
        for hd in heads:
            rhs = t_inv[hd] if last else jnp.concatenate([m_pow[hd], t_inv[hd]], axis=1)
            upd = _dot(m_pow[hd][skip:c, :], rhs)
            if skip:
                upd = jnp.concatenate([jnp.zeros((skip, upd.shape[1]), F32), upd], axis=0)
            if last:
                t_inv[hd] = t_inv[hd] + upd
            else:
                m_pow[hd] = upd[:, 0:c]
                t_inv[hd] = t_inv[hd] + upd[:, c:2 * c]
        yield
    ak_v = {(p, h): _dot(a_ak[p, h], vb_ref[slot, :, sls[p]]) for (p, h) in heads}
    yield
    for i, hd in enumerate(heads):
        pq_ref[mid_slot, i] = _dot(t_inv[hd],
                               jnp.concatenate([a_m[hd].astype(F32), ak_v[hd]], axis=1))


def _rwkv_chunk_state(slot, mid_slot, prep, mid, vec_ref, seg_ref, z_ref, out_ref, rows):
    vb_ref, bh_ref, kh_ref, dend_ref, g_ref, bv_ref = (prep[i] for i in (2, 5, 6, 7, 8, 9))
    pq_ref, lhs_ref = mid
    c = CHUNK
    ln_w, ln_b = vec_ref[5:6, :], vec_ref[6:7, :]
    lane = lax.broadcasted_iota(jnp.int32, (1, LANES), 1)
    row = lax.broadcasted_iota(jnp.int32, (c, c), 0)
    col = lax.broadcasted_iota(jnp.int32, (c, c), 1)
    eye = (col == row).astype(F32)
    same_head = (row // HEAD_DIM) == (col // HEAD_DIM)
    pairs = range(N_PAIRS)
    sls = [slice(p * PAIR, (p + 1) * PAIR) for p in pairs]
    heads = [(p, h) for p in pairs for h in range(2)]
    v_b = [vb_ref[slot, :, sl] for sl in sls]
    z_f = [z_ref[p] for p in pairs]
    z_b = [z.astype(BF16) for z in z_f]

    u = {(p, h): (_dot(pq_ref[mid_slot, i, :, 0:PAIR], z_b[p])
                  + pq_ref[mid_slot, i, :, PAIR:2 * PAIR])
         for i, (p, h) in enumerate(heads)}
    yield
    y_hd = {(p, h): jnp.dot(
        lhs_ref[mid_slot, i], jnp.concatenate([z_b[p], u[p, h].astype(BF16), v_b[p]], axis=0),
        preferred_element_type=F32) for i, (p, h) in enumerate(heads)}
    first = lane < HEAD_DIM
    y_pairs = []
    for p in pairs:
        u_p = jnp.where(first, u[p, 0], u[p, 1])
        y_pairs.append(jnp.where(first, y_hd[p, 0], y_hd[p, 1]))
        d_col = jnp.sum(eye * dend_ref[slot, 0:1, sls[p]], axis=1, keepdims=True)
        z_new = d_col * z_f[p] + jnp.dot(
            jnp.concatenate([bh_ref[slot, sls[p], :], kh_ref[slot, sls[p], :]], axis=1),
            jnp.concatenate([u_p.astype(BF16), v_b[p]], axis=0), preferred_element_type=F32)
        z_ref[p] = jnp.where(same_head, z_new, 0.0)
    yield

    seg = seg_ref[...]
    y = jnp.concatenate(y_pairs, axis=1)
    mean = _head_sums(y, seg) * (1.0 / HEAD_DIM)
    yc = y - mean
    var = _head_sums(yc * yc, seg) * (1.0 / HEAD_DIM)
    y = yc * lax.rsqrt(var + GN_EPS) * ln_w + ln_b
    out_ref[0, rows, :] = ((y + bv_ref[slot]) * g_ref[slot]).astype(BF16)


def _interleave(*stages, late=()):
    live = list(stages)
    late = list(late)
    while live:
        for gen in list(live):
            try:
                next(gen)
            except StopIteration:
                live.remove(gen)
                if late and gen is stages[0]:
                    live.extend(late)
                    late = []


def _in_turn(*stages):
    for gen in stages:
        yield from gen


def _rwkv_kernel(rkv_ref, small_ref, rkv_next_ref, small_next_ref,
                 w_lr_ref, w_gate_ref, vec_ref, seg_ref, out_ref, z_ref, pq_ref, lhs_ref, *prep):
    c = CHUNK
    params = (w_lr_ref, w_gate_ref, vec_ref, seg_ref)
    mid = (pq_ref, lhs_ref)
    rows = [slice(i * c, (i + 1) * c) for i in range(RWKV_STEP_CHUNKS)]

    def prepare(r_ref, s_ref, i, slot):
        return _rwkv_prepare(r_ref[0, rows[i], :], s_ref[0, rows[i], :], *params, slot, prep)

    def matrices(i):
        return _rwkv_chunk_matrices(i, i % RWKV_MID_SLOTS, prep, mid)

    def state(i):
        return _rwkv_chunk_state(i, i % RWKV_MID_SLOTS, prep, mid, vec_ref, seg_ref, z_ref,
                                 out_ref, rows[i])

    @pl.when(pl.program_id(1) == 0)
    def _():
        z_ref[...] = jnp.zeros_like(z_ref)
        _interleave(prepare(rkv_ref, small_ref, 0, 0), prepare(rkv_ref, small_ref, 1, 1))

    n_pairs = RWKV_STEP_CHUNKS // 2
    for j in range(n_pairs):
        a, b = 2 * j, 2 * j + 1
        streams = [matrices(a), matrices(b)]
        if j > 0:
            streams.insert(0, _in_turn(state(a - 2), state(b - 2)))
        if j + 1 < n_pairs:
            ahead = [prepare(rkv_ref, small_ref, a + 2, a + 2),
                     prepare(rkv_ref, small_ref, b + 2, b + 2)]
        else:
            ahead = [prepare(rkv_next_ref, small_next_ref, 0, 0),
                     prepare(rkv_next_ref, small_next_ref, 1, 1)]
        if ahead and j == 1 and j + 1 == n_pairs:
            _interleave(*streams, late=ahead)
        else:
            _interleave(*streams, *ahead)
    _interleave(_in_turn(state(RWKV_STEP_CHUNKS - 2), state(RWKV_STEP_CHUNKS - 1)))


def _rwkv_mix(rkv, small, w_lr, w_gate, vecs, seg):
    batch, seq, _ = rkv.shape
    c = CHUNK
    n = RWKV_STEP_CHUNKS
    n_pairs_total = seq // (2 * c)
    grid = (batch, seq // (n * c))
    const = lambda b, s: (0, 0)
    tile = lambda b, s: (b, s, 0)
    ahead = lambda b, s: (b, jnp.minimum(n // 2 * (s + 1), n_pairs_total - 1), 0)
    slots = lambda shape, dtype: pltpu.VMEM((n,) + shape, dtype)
    mid_slots = lambda shape, dtype: pltpu.VMEM((RWKV_MID_SLOTS,) + shape, dtype)
    return pl.pallas_call(
        _rwkv_kernel,
        grid=grid,
        in_specs=[pl.BlockSpec((1, n * c, N_RKV), tile),
                  pl.BlockSpec((1, n * c, N_SMALL), tile),
                  pl.BlockSpec((1, 2 * c, N_RKV), ahead),
                  pl.BlockSpec((1, 2 * c, N_SMALL), ahead),
                  pl.BlockSpec(w_lr.shape, const),
                  pl.BlockSpec(w_gate.shape, const),
                  pl.BlockSpec(vecs.shape, const),
                  pl.BlockSpec(seg.shape, const)],
        out_specs=pl.BlockSpec((1, n * c, WIDTH), tile),
        out_shape=jax.ShapeDtypeStruct((batch, seq, WIDTH), BF16),
        scratch_shapes=[pltpu.VMEM((N_PAIRS, PAIR, PAIR), F32),
                        mid_slots((HEADS, c, 2 * PAIR), F32),
                        mid_slots((HEADS, c, 3 * PAIR), BF16),
                        slots((c, WIDTH), BF16),
                        slots((c, WIDTH), BF16),
                        slots((c, WIDTH), BF16),
                        slots((WIDTH, c), BF16),
                        slots((WIDTH, c), BF16),
                        slots((WIDTH, c), BF16),
                        slots((WIDTH, c), BF16),
                        slots((8, WIDTH), F32),
                        slots((c, WIDTH), F32),
                        slots((c, WIDTH), F32)],
        compiler_params=pltpu.CompilerParams(dimension_semantics=("arbitrary", "arbitrary"),
                                             vmem_limit_bytes=VMEM_LIMIT_BYTES),
        name="rwkv",
    )(rkv, small, rkv, small, w_lr, w_gate, vecs, seg)


def _fox_kernel(qx_ref, kx_ref, vt_ref, cumt_ref, mask_ref, o_ref, vx_ref, t_ref, p_ref):
    heads = range(FOX_STEP_HEADS)
    head0 = FOX_STEP_HEADS * pl.program_id(1)
    blk = FOX_BLOCK
    seq = kx_ref.shape[2]
    n_blk = seq // blk
    ones_row = lax.broadcasted_iota(jnp.int32, (FOX_VROWS - HEAD_DIM, seq), 0) == 0
    for h in heads:
        vx_ref[h, 0:HEAD_DIM, :] = vt_ref[0, h * HEAD_DIM:(h + 1) * HEAD_DIM, :]
        vx_ref[h, HEAD_DIM:FOX_VROWS, :] = ones_row.astype(BF16)

    def rows_of(b):
        return pl.ds(pl.multiple_of(b * blk, blk), blk)

    def following(qi, kj):
        wrap = kj == qi
        return jnp.where(wrap, qi + 1, qi), jnp.where(wrap, 0, kj + 1)

    def scores(pair, slot, out):
        qi, kj = pair
        qi = jnp.minimum(qi, n_blk - 1)
        causal = mask_ref[(kj == qi).astype(jnp.int32)]
        mx = []
        for h in heads:
            t = _dot_nt(kx_ref[0, h, rows_of(kj), :], qx_ref[0, h, rows_of(qi), :]) + causal
            t_ref[slot, h] = t
            c_q = cumt_ref[0, pl.ds(head0 + h, 1), rows_of(qi)] * LOG2E
            mx.append(jnp.max(t, axis=0, keepdims=True) + c_q)
            yield
        out["mx"] = tuple(mx)

    def softmax(pair, slot, m, mx, out):
        qi, kj = pair
        sub = FOX_SUB
        q_rows = rows_of(jnp.minimum(qi, n_blk - 1))
        m_out, alpha_out = [], []
        for h in heads:
            m_old = jnp.where(kj == 0, -jnp.inf, m[h])
            c_q = cumt_ref[0, pl.ds(head0 + h, 1), q_rows] * LOG2E
            m_new = jnp.maximum(m_old, mx[h])
            shift = c_q - m_new
            for i in range(blk // sub):
                rows = slice(i * sub, (i + 1) * sub)
                p_ref[slot, h, rows, :] = jnp.exp2(t_ref[slot, h, rows, :] + shift).astype(BF16)
                if i % 2 == 1:
                    yield
            m_out.append(m_new)
            alpha_out.append(jnp.exp2(m_old - m_new))
        out["m"], out["alpha"] = tuple(m_out), tuple(alpha_out)

    def accumulate(pair, slot, alpha, acc, out):
        qi, kj = pair
        new = []
        for h in heads:
            new.append(alpha[h] * acc[h] + jnp.dot(vx_ref[h, :, rows_of(kj)], p_ref[slot, h],
                                                   preferred_element_type=F32))
            yield
        o_t = jnp.concatenate([a[0:HEAD_DIM] * (1.0 / a[HEAD_DIM:HEAD_DIM + 1]) for a in new], axis=0)
        o_ref[0, :, rows_of(qi)] = o_t.astype(BF16)
        out["acc"] = tuple(new)

    def step(pair_s, pair_p, pair_a, slot_s, m, mx, alpha, acc):
        out = {}
        _interleave(softmax(pair_p, 1 - slot_s, m, mx, out),
                    scores(pair_s, slot_s, out),
                    accumulate(pair_a, slot_s, alpha, acc, out))
        return out

    def two_pairs(_, carry):
        pair0, pair1, m, mx1, alpha0, acc = carry
        pair2 = following(*pair1)
        a = step(pair2, pair1, pair0, 0, m, mx1, alpha0, acc)
        pair3 = following(*pair2)
        b = step(pair3, pair2, pair1, 1, a["m"], a["mx"], a["alpha"], a["acc"])
        return pair2, pair3, b["m"], b["mx"], b["alpha"], b["acc"]

    n_pairs = n_blk * (n_blk + 1) // 2
    assert n_pairs % 2 == 0, "two pairs per trip"
    zero = jnp.int32(0)
    pair0 = (zero, zero)
    pair1 = following(*pair0)
    first, second = {}, {}
    _interleave(scores(pair0, 0, first))
    _interleave(scores(pair1, 1, second))
    m = tuple(jnp.full((1, blk), -jnp.inf, F32) for _ in heads)
    _interleave(softmax(pair0, 0, m, first["mx"], first))
    acc = tuple(jnp.zeros((FOX_VROWS, blk), F32) for _ in heads)
    lax.fori_loop(0, n_pairs // 2, two_pairs,
                  (pair0, pair1, first["m"], second["mx"], first["alpha"], acc))


def _forgetting_attention(qx, kx, v_t, cum_t):
    batch, _, seq, _ = qx.shape
    blk = FOX_BLOCK
    key = lax.broadcasted_iota(jnp.int32, (blk, blk), 0)
    qry = lax.broadcasted_iota(jnp.int32, (blk, blk), 1)
    mask = jnp.stack([jnp.zeros((blk, blk), F32), jnp.where(key <= qry, 0.0, FOX_MASKED)])
    nh = FOX_STEP_HEADS
    width = nh * HEAD_DIM
    whole = lambda b, g: (b, 0, 0)
    n_groups = HEADS // nh
    return pl.pallas_call(
        _fox_kernel,
        grid=(batch, n_groups),
        in_specs=[pl.BlockSpec((1, nh, seq, LANES), lambda b, g: (b, g, 0, 0)),
                  pl.BlockSpec((1, nh, seq, LANES), lambda b, g: (b, g, 0, 0)),
                  pl.BlockSpec((1, width, seq), lambda b, g: (b, g, 0)),
                  pl.BlockSpec((1, HEADS, seq), whole),
                  pl.BlockSpec((2, blk, blk), lambda b, g: (0, 0, 0))],
        out_specs=pl.BlockSpec((1, width, seq), lambda b, g: (b, g, 0)),
        out_shape=jax.ShapeDtypeStruct((batch, WIDTH, seq), BF16),
        scratch_shapes=[pltpu.VMEM((nh, FOX_VROWS, seq), BF16),
                        pltpu.VMEM((2, nh, blk, blk), F32),
                        pltpu.VMEM((2, nh, blk, blk), BF16)],
        compiler_params=pltpu.CompilerParams(
            dimension_semantics=("arbitrary", "arbitrary"),
            vmem_limit_bytes=VMEM_LIMIT_BYTES),
        name="fox",
    )(qx, kx, v_t, cum_t, mask)


def _tail_kernel(x_ref, ya_ref, ybt_ref, gate_ref, mod_ref, g2_ref, gf_ref,
                 woa_ref, wob_ref, wout_ref, w1_ref, w2_ref, o_ref):
    x = x_ref[0]
    gate1 = mod_ref[0, 2:3, :]
    shift2 = mod_ref[0, 3:4, :]
    scale2 = mod_ref[0, 4:5, :]
    gate2 = mod_ref[0, 5:6, :]

    merged = (gate_ref[0, :, 0:D_MODEL].astype(F32)
              * jnp.dot(ya_ref[0], woa_ref[...], preferred_element_type=F32)
              + gate_ref[0, :, D_MODEL:N_GATE].astype(F32)
              * lax.dot_general(ybt_ref[0], wob_ref[...], (((0,), (0,)), ((), ())),
                                preferred_element_type=F32))
    x = x + gate1 * _dot(merged, wout_ref[...])

    inv = lax.rsqrt(jnp.mean(x * x, axis=-1, keepdims=True) + NORM_EPS)
    h2 = ((x * inv) * g2_ref[...] * (1.0 + scale2) + shift2).astype(BF16)
    ff = jnp.zeros_like(x)
    for j in range(D_FF // FF_CHUNK):
        cols = slice(j * FF_CHUNK, (j + 1) * FF_CHUNK)
        hid = jnp.maximum(jnp.dot(h2, w1_ref[:, cols], preferred_element_type=F32), 0.0)
        ff = ff + _dot(hid * hid, w2_ref[cols, :])
    x = x + gate2 * ff

    inv = lax.rsqrt(jnp.mean(x * x, axis=-1, keepdims=True) + NORM_EPS)
    o_ref[0] = (x * inv) * gf_ref[...]


def _tail(x, y_a, y_b, gates, mod3, norm2_g, final_g, w_oa, w_ob, w_out, w_ff1, w_ff2):
    batch, seq, _ = x.shape
    tm = TAIL_ROWS
    grid = (batch, seq // tm)
    const = lambda b, s: (0, 0)
    tile = lambda b, s: (b, s, 0)
    resident = lambda a: pl.BlockSpec(a.shape, const, pipeline_mode=pl.Buffered(1))
    return pl.pallas_call(
        _tail_kernel,
        grid=grid,
        in_specs=[pl.BlockSpec((1, tm, D_MODEL), tile),
                  pl.BlockSpec((1, tm, WIDTH), tile),
                  pl.BlockSpec((1, WIDTH, tm), lambda b, s: (b, 0, s)),
                  pl.BlockSpec((1, tm, N_GATE), tile),
                  pl.BlockSpec((1, N_MOD, D_MODEL), lambda b, s: (b, 0, 0)),
                  pl.BlockSpec((1, D_MODEL), const),
                  pl.BlockSpec((1, D_MODEL), const),
                  resident(w_oa), resident(w_ob), resident(w_out),
                  resident(w_ff1), resident(w_ff2)],
        out_specs=pl.BlockSpec((1, tm, D_MODEL), tile),
        out_shape=jax.ShapeDtypeStruct((batch, seq, D_MODEL), F32),
        compiler_params=pltpu.CompilerParams(dimension_semantics=("arbitrary", "arbitrary"),
                                             vmem_limit_bytes=VMEM_LIMIT_BYTES),
        name="tail",
    )(x, y_a, y_b, gates, mod3, norm2_g, final_g, w_oa, w_ob, w_out, w_ff1, w_ff2)


def _drop_depth_axis(t):
    return t.reshape(t.shape[1:])


def _reorder_rwkv_cols(t):
    o = 0
    r = t[..., o:o + WIDTH]; o += WIDTH
    wd = t[..., o:o + DECAY_RANK]; o += DECAY_RANK
    k = t[..., o:o + WIDTH]; o += WIDTH
    v = t[..., o:o + WIDTH]; o += WIDTH
    ad = t[..., o:o + ICLR_RANK]; o += ICLR_RANK
    gd = t[..., o:o + GATE_RANK]
    return jnp.concatenate([r, k, v, wd, ad, gd], axis=-1)


def kernel(x, c, w_ada, b_ada, norm1_g, w_in, mu_shift, w_decay_up, decay_base, w_iclr_up, iclr_base, w_gate_up, kk_scale, k_iclr_mix, r_bonus, lnx_w, lnx_b, fox_f_bias, w_o_rwkv, w_o_fox, w_out, norm2_g, w_ff1, w_ff2, final_g):
    assert w_ada.shape[0] == 1, "the tail kernel fuses the final norm: single layer only"
    layer = _drop_depth_axis
    n_rwkv = N_RKV + N_SMALL
    seg_id = jnp.arange(WIDTH // 2) // HEAD_DIM
    seg = (seg_id[:, None] == seg_id[None, :]).astype(BF16)

    mod3 = _modulation(c, layer(w_ada), layer(b_ada)).reshape(-1, N_MOD, D_MODEL)

    w = layer(w_in).astype(BF16)
    o_wd, o_k, o_v, o_ad = WIDTH, WIDTH + DECAY_RANK, 2 * WIDTH + DECAY_RANK, 3 * WIDTH + DECAY_RANK
    o_ff = n_rwkv + N_FOX
    weights = [w[:, 0:o_wd], w[:, o_k:o_v], w[:, o_v:o_ad],
               jnp.concatenate([w[:, o_wd:o_k], w[:, o_ad:n_rwkv]], axis=1),
               w[:, n_rwkv:o_ff],
               jnp.pad(w[:, o_ff:o_ff + HEADS], ((0, 0), (0, LANES - HEADS))),
               w[:, o_ff + HEADS:]]
    mu = _reorder_rwkv_cols(layer(mu_shift)).reshape(1, n_rwkv)
    f_bias = jnp.pad(layer(fox_f_bias), (0, LANES - HEADS)).reshape(1, LANES)
    rkv, small, qx, kx, v_t, cum_t, gates = _projection(
        x, mod3, layer(norm1_g).reshape(1, D_MODEL), weights, mu, f_bias)

    zeros = jnp.zeros((DECAY_RANK, WIDTH), F32)
    w_lr = jnp.concatenate(
        [jnp.concatenate([layer(w_decay_up), zeros], axis=1),
         jnp.concatenate([zeros, layer(w_iclr_up)], axis=1)], axis=0).astype(BF16)
    vecs = jnp.stack([layer(decay_base), layer(iclr_base), layer(kk_scale), layer(k_iclr_mix),
                      layer(r_bonus).reshape(WIDTH), layer(lnx_w), layer(lnx_b),
                      jnp.zeros((WIDTH,), F32)], axis=0)
    y_a = _rwkv_mix(rkv, small, w_lr, layer(w_gate_up).astype(BF16), vecs, seg)

    y_b = _forgetting_attention(qx, kx, v_t, cum_t)

    return _tail(x, y_a, y_b, gates, mod3, layer(norm2_g).reshape(1, D_MODEL),
                 final_g.reshape(1, D_MODEL),
                 layer(w_o_rwkv).astype(BF16), layer(w_o_fox).astype(BF16),
                 layer(w_out).astype(BF16), layer(w_ff1).astype(BF16), layer(w_ff2).astype(BF16))
```

```python
import jax
import jax.numpy as jnp
from jax import lax
from jax.experimental import pallas as pl
from jax.experimental.pallas import tpu as pltpu

F32 = jnp.float32
BF16 = jnp.bfloat16

D_MODEL = 1024
HEAD_DIM = 64
HEADS = 8
WIDTH = HEADS * HEAD_DIM
DECAY_RANK = 64
ICLR_RANK = 64
GATE_RANK = 128
D_FF = 4 * D_MODEL
N_MOD = 6
NORM_EPS = 1e-6
GN_EPS = 64e-5

LANES = 128
PAIR = 2 * HEAD_DIM
N_PAIRS = HEADS // 2

LOG2E = 1.4426950408889634

N_SMALL = DECAY_RANK + ICLR_RANK + GATE_RANK
N_RKV = 3 * WIDTH
N_FOX = 3 * WIDTH
N_GATE = 2 * D_MODEL

PROJ_ROWS = 256
CHUNK = 128
RWKV_STEP_CHUNKS = 4
RWKV_MID_SLOTS = 4
FOX_BLOCK = 512
FOX_SUB = 32
FOX_VROWS = HEAD_DIM + 16
FOX_STEP_HEADS = 4
FOX_MASKED = -1e30
TAIL_ROWS = 512
FF_CHUNK = 1024

VMEM_LIMIT_BYTES = 56 * 1024 * 1024


def _dot(a, b):
    return jnp.dot(a.astype(BF16), b.astype(BF16), preferred_element_type=F32)


def _dot_nt(a, b):
    return lax.dot_general(a.astype(BF16), b.astype(BF16), (((1,), (1,)), ((), ())),
                           preferred_element_type=F32)


def _softplus(z):
    return jnp.maximum(z, 0.0) + jnp.log(1.0 + jnp.exp(-jnp.abs(z)))


def _sigmoid(z):
    return 1.0 / (1.0 + jnp.exp(-z))


def _bf16_parts(x, n):
    parts = []
    for _ in range(n):
        p = x.astype(BF16)
        parts.append(p)
        x = x - p.astype(F32)
    return parts


def _cumsum_rows(tri, x, n_parts):
    return sum(jnp.dot(tri, p, preferred_element_type=F32) for p in _bf16_parts(x, n_parts))


def _head_sums(x, seg):
    half = seg.shape[0]
    return jnp.concatenate([_dot(x[:, 0:half], seg), _dot(x[:, half:2 * half], seg)], axis=1)


def _lower_tri(n, dtype):
    row = lax.broadcasted_iota(jnp.int32, (n, n), 0)
    col = lax.broadcasted_iota(jnp.int32, (n, n), 1)
    return (col <= row).astype(dtype)


def _mod_kernel(c_ref, w_ref, b_ref, o_ref):
    c = c_ref[...]
    c_act = c * _sigmoid(c)
    o_ref[...] = _dot(c_act, w_ref[...]) + b_ref[...]


def _modulation(c, w_ada, b_ada):
    batch = c.shape[0]
    n = w_ada.shape[1]
    tn = 1536
    return pl.pallas_call(
        _mod_kernel,
        grid=(n // tn,),
        in_specs=[pl.BlockSpec((batch, D_MODEL), lambda j: (0, 0)),
                  pl.BlockSpec((D_MODEL, tn), lambda j: (0, j)),
                  pl.BlockSpec((1, tn), lambda j: (0, j))],
        out_specs=pl.BlockSpec((batch, tn), lambda j: (0, j)),
        out_shape=jax.ShapeDtypeStruct((batch, n), F32),
        compiler_params=pltpu.CompilerParams(dimension_semantics=("arbitrary",),
                                             vmem_limit_bytes=VMEM_LIMIT_BYTES),
        name="mod",
    )(c, w_ada, b_ada.reshape(1, n))


def _proj_normalise(x, mod_ref, g_ref):
    shift1 = mod_ref[0, 0:1, :]
    scale1 = mod_ref[0, 1:2, :]
    inv = lax.rsqrt(jnp.mean(x * x, axis=-1, keepdims=True) + NORM_EPS)
    return ((x * inv) * (g_ref[...] * (1.0 + scale1)) + shift1).astype(BF16)


def _proj_tile(h, rows, weights, mu_ref, fb_ref, route_ref, outs, shift_ref, carry_ref):
    wr_ref, wk_ref, wv_ref, ws_ref, wfox_ref, wff_ref, wgate_ref = weights
    rkv_ref, small_ref, qx_ref, kx_ref, vt_ref, cumt_ref, gate_ref = outs
    n = h.shape[0]

    ff = jnp.dot(h, wff_ref[...], preferred_element_type=F32)
    logf = -_softplus(-(ff + fb_ref[...]))
    cum = _cumsum_rows(_lower_tri(n, BF16), logf, 3) + carry_ref[...]
    cumt_ref[0, :, rows] = jnp.transpose(cum)[0:HEADS, :]
    carry_ref[...] = cum[n - 1:n, :]

    n_tok = N_RKV + N_SMALL
    p = jnp.concatenate([jnp.dot(h, w[...], preferred_element_type=F32)
                         for w in (wr_ref, wk_ref, wv_ref, ws_ref)], axis=1)
    shift_ref[8:8 + n, :] = p
    prev = shift_ref[7:7 + n, :]
    shift_ref[7:8, :] = p[n - 1:n, :]
    mixed = p + mu_ref[...] * (prev - p)
    rkv_ref[0, rows, :] = mixed[:, 0:N_RKV].astype(BF16)
    small_ref[0, rows, :] = mixed[:, N_RKV:n_tok]

    pg = jnp.dot(h, wgate_ref[...], preferred_element_type=F32)
    gate_ref[0, rows, :] = _sigmoid(pg).astype(BF16)

    pf = jnp.dot(h, wfox_ref[...], preferred_element_type=F32)
    vt_ref[0, :, rows] = jnp.transpose(pf[:, 2 * WIDTH:N_FOX]).astype(BF16)

    parts = jnp.concatenate(_bf16_parts(cum * (-LOG2E), 3), axis=1)
    feats = jnp.dot(parts, route_ref[...], preferred_element_type=F32).astype(BF16)
    lane_b = lax.broadcasted_iota(jnp.int32, (n, LANES), 1)
    for pair in range(N_PAIRS):
        group = slice(pair * PAIR, (pair + 1) * PAIR)
        q_pair = (pf[:, group] * (LOG2E * HEAD_DIM ** -0.5)).astype(BF16)
        k_pair = pf[:, WIDTH + pair * PAIR:WIDTH + (pair + 1) * PAIR].astype(BF16)
        for hh in range(2):
            own = (lane_b // HEAD_DIM) == hh
            bias0 = HEAD_DIM * (1 - hh)
            is_bias = (lane_b >= bias0) & (lane_b < bias0 + 3)
            kx_ref[0, 2 * pair + hh, rows, :] = jnp.where(own, k_pair, feats[:, group])
            qx_ref[0, 2 * pair + hh, rows, :] = jnp.where(
                own, q_pair, jnp.where(is_bias, 1.0, 0.0).astype(BF16))


def _proj_kernel(x_ref, x_next_ref, mod_ref, g_ref, wr_ref, wk_ref, wv_ref, ws_ref, wfox_ref,
                 wff_ref, wgate_ref, mu_ref, fb_ref, route_ref,
                 rkv_ref, small_ref, qx_ref, kx_ref, vt_ref, cumt_ref, gate_ref,
                 shift_ref, carry_ref, h_ref):
    tm = PROJ_ROWS
    weights = (wr_ref, wk_ref, wv_ref, ws_ref, wfox_ref, wff_ref, wgate_ref)
    outs = (rkv_ref, small_ref, qx_ref, kx_ref, vt_ref, cumt_ref, gate_ref)

    @pl.when(pl.program_id(1) == 0)
    def _():
        shift_ref[0:8, :] = jnp.zeros((8, N_RKV + N_SMALL), F32)
        carry_ref[...] = jnp.zeros_like(carry_ref)
        h_ref[0] = _proj_normalise(x_ref[0, 0:tm, :], mod_ref, g_ref)

    h_ref[1] = _proj_normalise(x_ref[0, tm:2 * tm, :], mod_ref, g_ref)
    _proj_tile(h_ref[0], slice(0, tm), weights, mu_ref, fb_ref, route_ref, outs,
               shift_ref, carry_ref)
    h_ref[0] = _proj_normalise(x_next_ref[0], mod_ref, g_ref)
    _proj_tile(h_ref[1], slice(tm, 2 * tm), weights, mu_ref, fb_ref, route_ref, outs,
               shift_ref, carry_ref)


def _projection(x, mod3, norm_g, weights, mu, f_bias):
    src = jnp.arange(3 * LANES)[:, None]
    dst = jnp.arange(WIDTH)[None, :]
    route = jnp.zeros((3 * LANES, WIDTH), jnp.bool_)
    for p in range(N_PAIRS):
        for h in range(2):
            for n in range(3):
                route |= ((src == n * LANES + 2 * p + h)
                          & (dst == p * PAIR + HEAD_DIM * (1 - h) + n))
    route = route.astype(BF16)
    batch, seq, _ = x.shape
    tm = PROJ_ROWS
    n_tiles = seq // tm
    grid = (batch, n_tiles // 2)
    const = lambda b, s: (0, 0)
    tile = lambda b, s: (b, s, 0)
    lanes = lambda b, s: (b, 0, s)
    ahead = lambda b, s: (b, jnp.minimum(2 * s + 2, n_tiles - 1), 0)
    return pl.pallas_call(
        _proj_kernel,
        grid=grid,
        in_specs=[pl.BlockSpec((1, 2 * tm, D_MODEL), tile),
                  pl.BlockSpec((1, tm, D_MODEL), ahead),
                  pl.BlockSpec((1, N_MOD, D_MODEL), lambda b, s: (b, 0, 0)),
                  pl.BlockSpec((1, D_MODEL), const),
                  *[pl.BlockSpec(w.shape, const, pipeline_mode=pl.Buffered(1)) for w in weights],
                  pl.BlockSpec((1, N_RKV + N_SMALL), const),
                  pl.BlockSpec((1, LANES), const),
                  pl.BlockSpec(route.shape, const)],
        out_specs=[pl.BlockSpec((1, 2 * tm, N_RKV), tile),
                   pl.BlockSpec((1, 2 * tm, N_SMALL), tile),
                   pl.BlockSpec((1, HEADS, 2 * tm, LANES), lambda b, s: (b, 0, s, 0)),
                   pl.BlockSpec((1, HEADS, 2 * tm, LANES), lambda b, s: (b, 0, s, 0)),
                   pl.BlockSpec((1, WIDTH, 2 * tm), lanes),
                   pl.BlockSpec((1, HEADS, 2 * tm), lanes),
                   pl.BlockSpec((1, 2 * tm, N_GATE), tile)],
        out_shape=[jax.ShapeDtypeStruct((batch, seq, N_RKV), BF16),
                   jax.ShapeDtypeStruct((batch, seq, N_SMALL), F32),
                   jax.ShapeDtypeStruct((batch, HEADS, seq, LANES), BF16),
                   jax.ShapeDtypeStruct((batch, HEADS, seq, LANES), BF16),
                   jax.ShapeDtypeStruct((batch, WIDTH, seq), BF16),
                   jax.ShapeDtypeStruct((batch, HEADS, seq), F32),
                   jax.ShapeDtypeStruct((batch, seq, N_GATE), BF16)],
        scratch_shapes=[pltpu.VMEM((tm + 8, N_RKV + N_SMALL), F32),
                        pltpu.VMEM((1, LANES), F32),
                        pltpu.VMEM((2, tm, D_MODEL), BF16)],
        compiler_params=pltpu.CompilerParams(dimension_semantics=("arbitrary", "arbitrary"),
                                             vmem_limit_bytes=VMEM_LIMIT_BYTES),
        name="proj",
    )(x, x, mod3, norm_g, *weights, mu, f_bias, route)


def _rwkv_prepare(rkv, small, w_lr_ref, w_gate_ref, vec_ref, seg_ref, slot, prep):
    am_ref, rm_ref, vb_ref, bt_ref, kt_ref, bh_ref, kh_ref, dend_ref, g_ref, bv_ref = prep
    c = CHUNK
    r = rkv[:, 0:WIDTH].astype(F32)
    k_raw = rkv[:, WIDTH:2 * WIDTH].astype(F32)
    v = rkv[:, 2 * WIDTH:3 * WIDTH].astype(F32)
    decay_base, iclr_base, kk_scale, k_mix, r_bonus = (vec_ref[i:i + 1, :] for i in range(5))

    lane = lax.broadcasted_iota(jnp.int32, (1, LANES), 1)
    lr_in = jnp.where(lane < DECAY_RANK, jnp.tanh(small[:, 0:LANES]), small[:, 0:LANES])
    lr = _dot(lr_in, w_lr_ref[...])
    w_log = -_softplus(-(decay_base + lr[:, 0:WIDTH])) - 0.5
    log_decay = -jnp.exp(w_log)
    a = _sigmoid(iclr_base + lr[:, WIDTH:2 * WIDTH])
    g_ref[slot] = _dot(_sigmoid(small[:, LANES:2 * LANES]), w_gate_ref[...])
    yield

    seg = seg_ref[...]
    kk = k_raw * kk_scale
    kk = kk * lax.rsqrt(jnp.maximum(_head_sums(kk * kk, seg), 1e-24))
    k = k_raw * (1.0 + (a - 1.0) * k_mix)
    b_vec = kk * a
    yield

    cs = _cumsum_rows(_lower_tri(c, BF16), log_decay, 2)
    cs_end = cs[c - 1:c, :]
    am_ref[slot] = (-kk * jnp.exp(cs - log_decay)).astype(BF16)
    rm_ref[slot] = (r * jnp.exp(cs)).astype(BF16)
    vb_ref[slot] = v.astype(BF16)
    yield
    w_inv = jnp.exp(-cs)
    bt_ref[slot] = jnp.transpose(b_vec * w_inv).astype(BF16)
    yield
    kt_ref[slot] = jnp.transpose(k * w_inv).astype(BF16)
    yield
    w_end = jnp.exp(cs_end - cs)
    bh_ref[slot] = jnp.transpose(b_vec * w_end).astype(BF16)
    yield
    kh_ref[slot] = jnp.transpose(k * w_end).astype(BF16)
    dend_ref[slot] = jnp.broadcast_to(jnp.exp(cs_end), (8, WIDTH))
    yield
    bv_ref[slot] = _head_sums(r * k * r_bonus, seg) * v


def _rwkv_chunk_matrices(slot, mid_slot, prep, mid):
    am_ref, rm_ref, vb_ref, bt_ref, kt_ref = prep[:5]
    pq_ref, lhs_ref = mid
    c = CHUNK
    lane = lax.broadcasted_iota(jnp.int32, (1, LANES), 1)
    row = lax.broadcasted_iota(jnp.int32, (c, c), 0)
    col = lax.broadcasted_iota(jnp.int32, (c, c), 1)
    strict = col < row
    incl = col <= row
    eye = (col == row).astype(F32)

    sls = [slice(p * PAIR, (p + 1) * PAIR) for p in range(N_PAIRS)]
    heads = [(p, h) for p in range(N_PAIRS) for h in range(2)]
    a_m, a_ab, a_ak = {}, {}, {}
    for i, (p, h) in enumerate(heads):
        head = (lane // HEAD_DIM) == h
        a_m[p, h] = jnp.where(head, am_ref[slot, :, sls[p]].astype(F32), 0.0).astype(BF16)
        r_m = jnp.where(head, rm_ref[slot, :, sls[p]].astype(F32), 0.0).astype(BF16)
        rhs = jnp.concatenate([bt_ref[slot, sls[p], :], kt_ref[slot, sls[p], :]], axis=1)
        big = jnp.dot(jnp.concatenate([a_m[p, h], r_m], axis=0), rhs,
                      preferred_element_type=F32)
        a_ab[p, h] = jnp.where(strict, big[0:c, 0:c], 0.0)
        a_ak[p, h] = jnp.where(strict, big[0:c, c:2 * c], 0.0).astype(BF16)
        lhs_ref[mid_slot, i] = jnp.concatenate(
            [r_m, jnp.where(incl, big[c:2 * c, 0:c], 0.0).astype(BF16),
             jnp.where(incl, big[c:2 * c, c:2 * c], 0.0).astype(BF16)], axis=1)
    yield
    t_inv = {hd: eye + a_ab[hd] for hd in heads}
    m_pow = {hd: _dot(a_ab[hd], a_ab[hd]) for hd in heads}
    yield
    for k in range(1, 7):
        skip = (2 ** k) // 16 * 16
        last = k == 6
        for hd in heads:
            rhs = t_inv[hd] if last else jnp.concatenate([m_pow[hd], t_inv[hd]], axis=1)
            upd = _dot(m_pow[hd][skip:c, :], rhs)
            if skip:
                upd = jnp.concatenate([jnp.zeros((skip, upd.shape[1]), F32), upd], axis=0)
            if last:
                t_inv[hd] = t_inv[hd] + upd
            else:
                m_pow[hd] = upd[:, 0:c]
                t_inv[hd] = t_inv[hd] + upd[:, c:2 * c]
        yield
    ak_v = {(p, h): _dot(a_ak[p, h], vb_ref[slot, :, sls[p]]) for (p, h) in heads}
    yield
    for i, hd in enumerate(heads):
        pq_ref[mid_slot, i] = _dot(t_inv[hd],
                               jnp.concatenate([a_m[hd].astype(F32), ak_v[hd]], axis=1))


def _rwkv_chunk_state(slot, mid_slot, prep, mid, vec_ref, seg_ref, z_ref, out_ref, rows):
    vb_ref, bh_ref, kh_ref, dend_ref, g_ref, bv_ref = (prep[i] for i in (2, 5, 6, 7, 8, 9))
    pq_ref, lhs_ref = mid
    c = CHUNK
    ln_w, ln_b = vec_ref[5:6, :], vec_ref[6:7, :]
    lane = lax.broadcasted_iota(jnp.int32, (1, LANES), 1)
    row = lax.broadcasted_iota(jnp.int32, (c, c), 0)
    col = lax.broadcasted_iota(jnp.int32, (c, c), 1)
    eye = (col == row).astype(F32)
    same_head = (row // HEAD_DIM) == (col // HEAD_DIM)
    pairs = range(N_PAIRS)
    sls = [slice(p * PAIR, (p + 1) * PAIR) for p in pairs]
    heads = [(p, h) for p in pairs for h in range(2)]
    v_b = [vb_ref[slot, :, sl] for sl in sls]
    z_f = [z_ref[p] for p in pairs]
    z_b = [z.astype(BF16) for z in z_f]

    u = {(p, h): (_dot(pq_ref[mid_slot, i, :, 0:PAIR], z_b[p])
                  + pq_ref[mid_slot, i, :, PAIR:2 * PAIR])
         for i, (p, h) in enumerate(heads)}
    yield
    y_hd = {(p, h): jnp.dot(
        lhs_ref[mid_slot, i], jnp.concatenate([z_b[p], u[p, h].astype(BF16), v_b[p]], axis=0),
        preferred_element_type=F32) for i, (p, h) in enumerate(heads)}
    first = lane < HEAD_DIM
    y_pairs = []
    for p in pairs:
        u_p = jnp.where(first, u[p, 0], u[p, 1])
        y_pairs.append(jnp.where(first, y_hd[p, 0], y_hd[p, 1]))
        d_col = jnp.sum(eye * dend_ref[slot, 0:1, sls[p]], axis=1, keepdims=True)
        z_new = d_col * z_f[p] + jnp.dot(
            jnp.concatenate([bh_ref[slot, sls[p], :], kh_ref[slot, sls[p], :]], axis=1),
            jnp.concatenate([u_p.astype(BF16), v_b[p]], axis=0), preferred_element_type=F32)
        z_ref[p] = jnp.where(same_head, z_new, 0.0)
    yield

    seg = seg_ref[...]
    y = jnp.concatenate(y_pairs, axis=1)
    mean = _head_sums(y, seg) * (1.0 / HEAD_DIM)
    yc = y - mean
    var = _head_sums(yc * yc, seg) * (1.0 / HEAD_DIM)
    y = yc * lax.rsqrt(var + GN_EPS) * ln_w + ln_b
    out_ref[0, rows, :] = ((y + bv_ref[slot]) * g_ref[slot]).astype(BF16)


def _interleave(*stages, late=()):
    live = list(stages)
    late = list(late)
    while live:
        for gen in list(live):
            try:
                next(gen)
            except StopIteration:
                live.remove(gen)
                if late and gen is stages[0]:
                    live.extend(late)
                    late = []


def _in_turn(*stages):
    for gen in stages:
        yield from gen


def _rwkv_kernel(rkv_ref, small_ref, rkv_next_ref, small_next_ref,
                 w_lr_ref, w_gate_ref, vec_ref, seg_ref, out_ref, z_ref, pq_ref, lhs_ref, *prep):
    c = CHUNK
    params = (w_lr_ref, w_gate_ref, vec_ref, seg_ref)
    mid = (pq_ref, lhs_ref)
    rows = [slice(i * c, (i + 1) * c) for i in range(RWKV_STEP_CHUNKS)]

    def prepare(r_ref, s_ref, i, slot):
        return _rwkv_prepare(r_ref[0, rows[i], :], s_ref[0, rows[i], :], *params, slot, prep)

    def matrices(i):
        return _rwkv_chunk_matrices(i, i % RWKV_MID_SLOTS, prep, mid)

    def state(i):
        return _rwkv_chunk_state(i, i % RWKV_MID_SLOTS, prep, mid, vec_ref, seg_ref, z_ref,
                                 out_ref, rows[i])

    @pl.when(pl.program_id(1) == 0)
    def _():
        z_ref[...] = jnp.zeros_like(z_ref)
        _interleave(prepare(rkv_ref, small_ref, 0, 0), prepare(rkv_ref, small_ref, 1, 1))

    n_pairs = RWKV_STEP_CHUNKS // 2
    for j in range(n_pairs):
        a, b = 2 * j, 2 * j + 1
        streams = [matrices(a), matrices(b)]
        if j > 0:
            streams.insert(0, _in_turn(state(a - 2), state(b - 2)))
        if j + 1 < n_pairs:
            ahead = [prepare(rkv_ref, small_ref, a + 2, a + 2),
                     prepare(rkv_ref, small_ref, b + 2, b + 2)]
        else:
            ahead = [prepare(rkv_next_ref, small_next_ref, 0, 0),
                     prepare(rkv_next_ref, small_next_ref, 1, 1)]
        if ahead and j == 1 and j + 1 == n_pairs:
            _interleave(*streams, late=ahead)
        else:
            _interleave(*streams, *ahead)
    _interleave(_in_turn(state(RWKV_STEP_CHUNKS - 2), state(RWKV_STEP_CHUNKS - 1)))


def _rwkv_mix(rkv, small, w_lr, w_gate, vecs, seg):
    batch, seq, _ = rkv.shape
    c = CHUNK
    n = RWKV_STEP_CHUNKS
    n_pairs_total = seq // (2 * c)
    grid = (batch, seq // (n * c))
    const = lambda b, s: (0, 0)
    tile = lambda b, s: (b, s, 0)
    ahead = lambda b, s: (b, jnp.minimum(n // 2 * (s + 1), n_pairs_total - 1), 0)
    slots = lambda shape, dtype: pltpu.VMEM((n,) + shape, dtype)
    mid_slots = lambda shape, dtype: pltpu.VMEM((RWKV_MID_SLOTS,) + shape, dtype)
    return pl.pallas_call(
        _rwkv_kernel,
        grid=grid,
        in_specs=[pl.BlockSpec((1, n * c, N_RKV), tile),
                  pl.BlockSpec((1, n * c, N_SMALL), tile),
                  pl.BlockSpec((1, 2 * c, N_RKV), ahead),
                  pl.BlockSpec((1, 2 * c, N_SMALL), ahead),
                  pl.BlockSpec(w_lr.shape, const),
                  pl.BlockSpec(w_gate.shape, const),
                  pl.BlockSpec(vecs.shape, const),
                  pl.BlockSpec(seg.shape, const)],
        out_specs=pl.BlockSpec((1, n * c, WIDTH), tile),
        out_shape=jax.ShapeDtypeStruct((batch, seq, WIDTH), BF16),
        scratch_shapes=[pltpu.VMEM((N_PAIRS, PAIR, PAIR), F32),
                        mid_slots((HEADS, c, 2 * PAIR), F32),
                        mid_slots((HEADS, c, 3 * PAIR), BF16),
                        slots((c, WIDTH), BF16),
                        slots((c, WIDTH), BF16),
                        slots((c, WIDTH), BF16),
                        slots((WIDTH, c), BF16),
                        slots((WIDTH, c), BF16),
                        slots((WIDTH, c), BF16),
                        slots((WIDTH, c), BF16),
                        slots((8, WIDTH), F32),
                        slots((c, WIDTH), F32),
                        slots((c, WIDTH), F32)],
        compiler_params=pltpu.CompilerParams(dimension_semantics=("arbitrary", "arbitrary"),
                                             vmem_limit_bytes=VMEM_LIMIT_BYTES),
        name="rwkv",
    )(rkv, small, rkv, small, w_lr, w_gate, vecs, seg)


def _fox_kernel(qx_ref, kx_ref, vt_ref, cumt_ref, mask_ref, o_ref, vx_ref, t_ref, p_ref):
    heads = range(FOX_STEP_HEADS)
    head0 = FOX_STEP_HEADS * pl.program_id(1)
    blk = FOX_BLOCK
    seq = kx_ref.shape[2]
    n_blk = seq // blk
    ones_row = lax.broadcasted_iota(jnp.int32, (FOX_VROWS - HEAD_DIM, seq), 0) == 0
    for h in heads:
        vx_ref[h, 0:HEAD_DIM, :] = vt_ref[0, h * HEAD_DIM:(h + 1) * HEAD_DIM, :]
        vx_ref[h, HEAD_DIM:FOX_VROWS, :] = ones_row.astype(BF16)

    def rows_of(b):
        return pl.ds(pl.multiple_of(b * blk, blk), blk)

    def following(qi, kj):
        wrap = kj == qi
        return jnp.where(wrap, qi + 1, qi), jnp.where(wrap, 0, kj + 1)

    def scores(pair, slot, out):
        qi, kj = pair
        qi = jnp.minimum(qi, n_blk - 1)
        causal = mask_ref[(kj == qi).astype(jnp.int32)]
        mx = []
        for h in heads:
            t = _dot_nt(kx_ref[0, h, rows_of(kj), :], qx_ref[0, h, rows_of(qi), :]) + causal
            t_ref[slot, h] = t
            c_q = cumt_ref[0, pl.ds(head0 + h, 1), rows_of(qi)] * LOG2E
            mx.append(jnp.max(t, axis=0, keepdims=True) + c_q)
            yield
        out["mx"] = tuple(mx)

    def softmax(pair, slot, m, mx, out):
        qi, kj = pair
        sub = FOX_SUB
        q_rows = rows_of(jnp.minimum(qi, n_blk - 1))
        m_out, alpha_out = [], []
        for h in heads:
            m_old = jnp.where(kj == 0, -jnp.inf, m[h])
            c_q = cumt_ref[0, pl.ds(head0 + h, 1), q_rows] * LOG2E
            m_new = jnp.maximum(m_old, mx[h])
            shift = c_q - m_new
            for i in range(blk // sub):
                rows = slice(i * sub, (i + 1) * sub)
                p_ref[slot, h, rows, :] = jnp.exp2(t_ref[slot, h, rows, :] + shift).astype(BF16)
                if i % 2 == 1:
                    yield
            m_out.append(m_new)
            alpha_out.append(jnp.exp2(m_old - m_new))
        out["m"], out["alpha"] = tuple(m_out), tuple(alpha_out)

    def accumulate(pair, slot, alpha, acc, out):
        qi, kj = pair
        new = []
        for h in heads:
            new.append(alpha[h] * acc[h] + jnp.dot(vx_ref[h, :, rows_of(kj)], p_ref[slot, h],
                                                   preferred_element_type=F32))
            yield
        o_t = jnp.concatenate([a[0:HEAD_DIM] / a[HEAD_DIM:HEAD_DIM + 1] for a in new], axis=0)
        o_ref[0, :, rows_of(qi)] = o_t.astype(BF16)
        out["acc"] = tuple(new)

    def step(pair_s, pair_p, pair_a, slot_s, m, mx, alpha, acc):
        out = {}
        _interleave(softmax(pair_p, 1 - slot_s, m, mx, out),
                    scores(pair_s, slot_s, out),
                    accumulate(pair_a, slot_s, alpha, acc, out))
        return out

    def two_pairs(_, carry):
        pair0, pair1, m, mx1, alpha0, acc = carry
        pair2 = following(*pair1)
        a = step(pair2, pair1, pair0, 0, m, mx1, alpha0, acc)
        pair3 = following(*pair2)
        b = step(pair3, pair2, pair1, 1, a["m"], a["mx"], a["alpha"], a["acc"])
        return pair2, pair3, b["m"], b["mx"], b["alpha"], b["acc"]

    n_pairs = n_blk * (n_blk + 1) // 2
    assert n_pairs % 2 == 0, "two pairs per trip"
    zero = jnp.int32(0)
    pair0 = (zero, zero)
    pair1 = following(*pair0)
    first, second = {}, {}
    _interleave(scores(pair0, 0, first))
    _interleave(scores(pair1, 1, second))
    m = tuple(jnp.full((1, blk), -jnp.inf, F32) for _ in heads)
    _interleave(softmax(pair0, 0, m, first["mx"], first))
    acc = tuple(jnp.zeros((FOX_VROWS, blk), F32) for _ in heads)
    lax.fori_loop(0, n_pairs // 2, two_pairs,
                  (pair0, pair1, first["m"], second["mx"], first["alpha"], acc))


def _forgetting_attention(qx, kx, v_t, cum_t):
    batch, _, seq, _ = qx.shape
    blk = FOX_BLOCK
    key = lax.broadcasted_iota(jnp.int32, (blk, blk), 0)
    qry = lax.broadcasted_iota(jnp.int32, (blk, blk), 1)
    mask = jnp.stack([jnp.zeros((blk, blk), F32), jnp.where(key <= qry, 0.0, FOX_MASKED)])
    nh = FOX_STEP_HEADS
    width = nh * HEAD_DIM
    whole = lambda b, g: (b, 0, 0)
    n_groups = HEADS // nh
    return pl.pallas_call(
        _fox_kernel,
        grid=(batch, n_groups),
        in_specs=[pl.BlockSpec((1, nh, seq, LANES), lambda b, g: (b, g, 0, 0)),
                  pl.BlockSpec((1, nh, seq, LANES), lambda b, g: (b, g, 0, 0)),
                  pl.BlockSpec((1, width, seq), lambda b, g: (b, g, 0)),
                  pl.BlockSpec((1, HEADS, seq), whole),
                  pl.BlockSpec((2, blk, blk), lambda b, g: (0, 0, 0))],
        out_specs=pl.BlockSpec((1, width, seq), lambda b, g: (b, g, 0)),
        out_shape=jax.ShapeDtypeStruct((batch, WIDTH, seq), BF16),
        scratch_shapes=[pltpu.VMEM((nh, FOX_VROWS, seq), BF16),
                        pltpu.VMEM((2, nh, blk, blk), F32),
                        pltpu.VMEM((2, nh, blk, blk), BF16)],
        compiler_params=pltpu.CompilerParams(
            dimension_semantics=("arbitrary", "arbitrary"),
            vmem_limit_bytes=VMEM_LIMIT_BYTES),
        name="fox",
    )(qx, kx, v_t, cum_t, mask)


def _tail_kernel(x_ref, ya_ref, ybt_ref, gate_ref, mod_ref, g2_ref, gf_ref,
                 woa_ref, wob_ref, wout_ref, w1_ref, w2_ref, o_ref):
    x = x_ref[0]
    gate1 = mod_ref[0, 2:3, :]
    shift2 = mod_ref[0, 3:4, :]
    scale2 = mod_ref[0, 4:5, :]
    gate2 = mod_ref[0, 5:6, :]

    merged = (gate_ref[0, :, 0:D_MODEL].astype(F32)
              * jnp.dot(ya_ref[0], woa_ref[...], preferred_element_type=F32)
              + gate_ref[0, :, D_MODEL:N_GATE].astype(F32)
              * lax.dot_general(ybt_ref[0], wob_ref[...], (((0,), (0,)), ((), ())),
                                preferred_element_type=F32))
    x = x + gate1 * _dot(merged, wout_ref[...])

    inv = lax.rsqrt(jnp.mean(x * x, axis=-1, keepdims=True) + NORM_EPS)
    h2 = ((x * inv) * g2_ref[...] * (1.0 + scale2) + shift2).astype(BF16)
    ff = jnp.zeros_like(x)
    for j in range(D_FF // FF_CHUNK):
        cols = slice(j * FF_CHUNK, (j + 1) * FF_CHUNK)
        hid = jnp.maximum(jnp.dot(h2, w1_ref[:, cols], preferred_element_type=F32), 0.0)
        ff = ff + _dot(hid * hid, w2_ref[cols, :])
    x = x + gate2 * ff

    inv = lax.rsqrt(jnp.mean(x * x, axis=-1, keepdims=True) + NORM_EPS)
    o_ref[0] = (x * inv) * gf_ref[...]


def _tail(x, y_a, y_b, gates, mod3, norm2_g, final_g, w_oa, w_ob, w_out, w_ff1, w_ff2):
    batch, seq, _ = x.shape
    tm = TAIL_ROWS
    grid = (batch, seq // tm)
    const = lambda b, s: (0, 0)
    tile = lambda b, s: (b, s, 0)
    resident = lambda a: pl.BlockSpec(a.shape, const, pipeline_mode=pl.Buffered(1))
    return pl.pallas_call(
        _tail_kernel,
        grid=grid,
        in_specs=[pl.BlockSpec((1, tm, D_MODEL), tile),
                  pl.BlockSpec((1, tm, WIDTH), tile),
                  pl.BlockSpec((1, WIDTH, tm), lambda b, s: (b, 0, s)),
                  pl.BlockSpec((1, tm, N_GATE), tile),
                  pl.BlockSpec((1, N_MOD, D_MODEL), lambda b, s: (b, 0, 0)),
                  pl.BlockSpec((1, D_MODEL), const),
                  pl.BlockSpec((1, D_MODEL), const),
                  resident(w_oa), resident(w_ob), resident(w_out),
                  resident(w_ff1), resident(w_ff2)],
        out_specs=pl.BlockSpec((1, tm, D_MODEL), tile),
        out_shape=jax.ShapeDtypeStruct((batch, seq, D_MODEL), F32),
        compiler_params=pltpu.CompilerParams(dimension_semantics=("arbitrary", "arbitrary"),
                                             vmem_limit_bytes=VMEM_LIMIT_BYTES),
        name="tail",
    )(x, y_a, y_b, gates, mod3, norm2_g, final_g, w_oa, w_ob, w_out, w_ff1, w_ff2)


def _drop_depth_axis(t):
    return t.reshape(t.shape[1:])


def _reorder_rwkv_cols(t):
    o = 0
    r = t[..., o:o + WIDTH]; o += WIDTH
    wd = t[..., o:o + DECAY_RANK]; o += DECAY_RANK
    k = t[..., o:o + WIDTH]; o += WIDTH
    v = t[..., o:o + WIDTH]; o += WIDTH
    ad = t[..., o:o + ICLR_RANK]; o += ICLR_RANK
    gd = t[..., o:o + GATE_RANK]
    return jnp.concatenate([r, k, v, wd, ad, gd], axis=-1)


def kernel(x, c, w_ada, b_ada, norm1_g, w_in, mu_shift, w_decay_up, decay_base, w_iclr_up, iclr_base, w_gate_up, kk_scale, k_iclr_mix, r_bonus, lnx_w, lnx_b, fox_f_bias, w_o_rwkv, w_o_fox, w_out, norm2_g, w_ff1, w_ff2, final_g):
    assert w_ada.shape[0] == 1, "the tail kernel fuses the final norm: single layer only"
    layer = _drop_depth_axis
    n_rwkv = N_RKV + N_SMALL
    seg_id = jnp.arange(WIDTH // 2) // HEAD_DIM
    seg = (seg_id[:, None] == seg_id[None, :]).astype(BF16)

    mod3 = _modulation(c, layer(w_ada), layer(b_ada)).reshape(-1, N_MOD, D_MODEL)

    w = layer(w_in).astype(BF16)
    o_wd, o_k, o_v, o_ad = WIDTH, WIDTH + DECAY_RANK, 2 * WIDTH + DECAY_RANK, 3 * WIDTH + DECAY_RANK
    o_ff = n_rwkv + N_FOX
    weights = [w[:, 0:o_wd], w[:, o_k:o_v], w[:, o_v:o_ad],
               jnp.concatenate([w[:, o_wd:o_k], w[:, o_ad:n_rwkv]], axis=1),
               w[:, n_rwkv:o_ff],
               jnp.pad(w[:, o_ff:o_ff + HEADS], ((0, 0), (0, LANES - HEADS))),
               w[:, o_ff + HEADS:]]
    mu = _reorder_rwkv_cols(layer(mu_shift)).reshape(1, n_rwkv)
    f_bias = jnp.pad(layer(fox_f_bias), (0, LANES - HEADS)).reshape(1, LANES)
    rkv, small, qx, kx, v_t, cum_t, gates = _projection(
        x, mod3, layer(norm1_g).reshape(1, D_MODEL), weights, mu, f_bias)

    zeros = jnp.zeros((DECAY_RANK, WIDTH), F32)
    w_lr = jnp.concatenate(
        [jnp.concatenate([layer(w_decay_up), zeros], axis=1),
         jnp.concatenate([zeros, layer(w_iclr_up)], axis=1)], axis=0).astype(BF16)
    vecs = jnp.stack([layer(decay_base), layer(iclr_base), layer(kk_scale), layer(k_iclr_mix),
                      layer(r_bonus).reshape(WIDTH), layer(lnx_w), layer(lnx_b),
                      jnp.zeros((WIDTH,), F32)], axis=0)
    y_a = _rwkv_mix(rkv, small, w_lr, layer(w_gate_up).astype(BF16), vecs, seg)

    y_b = _forgetting_attention(qx, kx, v_t, cum_t)

    return _tail(x, y_a, y_b, gates, mod3, layer(norm2_g).reshape(1, D_MODEL),
                 final_g.reshape(1, D_MODEL),
                 layer(w_o_rwkv).astype(BF16), layer(w_o_fox).astype(BF16),
                 layer(w_out).astype(BF16), layer(w_ff1).astype(BF16), layer(w_ff2).astype(BF16))
```
